```python
import jax, jax.numpy as jnp
from jax import lax
import numpy as np

D_MODEL = 1024
BATCH = 1
SEQ = 16384
DEPTH = 4

D_MIX = 1024
MLA_HEADS = 8
QK_NOPE = 64
QK_ROPE = 32
QK_HEAD = QK_NOPE + QK_ROPE
V_HEAD = 64
Q_LORA = 384
KV_LORA = 256
MLA_WIDTH = MLA_HEADS * V_HEAD
ROPE_THETA = 10000.0
Q_BLOCK = 128
POOL_WINDOWS = (2, 4, 8, 16)
POOL_GROUPS = len(POOL_WINDOWS)
POOL_WIDTH = D_MIX - MLA_WIDTH
POOL_GROUP_DIM = POOL_WIDTH // POOL_GROUPS
D_IN = Q_LORA + KV_LORA + QK_ROPE + POOL_WIDTH
D_FF = 2816
EPS = 1e-6

kernel_name = "hybrid_mla_pool_macaron_encoder"


def rmsnorm(x, g):
    xf = x.astype(jnp.float32)
    y = xf * lax.rsqrt(jnp.mean(xf * xf, axis=-1, keepdims=True) + EPS)
    return (y * g.astype(jnp.float32)).astype(x.dtype)


def swiglu(h, w_gu, w_down):
    gu = h @ w_gu
    g, u = gu[..., :D_FF], gu[..., D_FF:]
    return (jax.nn.silu(g) * u) @ w_down


def rope_tables(seq):
    pos = jnp.arange(seq, dtype=jnp.float32)
    inv = ROPE_THETA ** (-jnp.arange(0, QK_ROPE, 2, dtype=jnp.float32) / QK_ROPE)
    ang = pos[:, None] * inv[None, :]
    return jnp.cos(ang), jnp.sin(ang)


def apply_rope(t, cos, sin):
    tf = t.astype(jnp.float32)
    half = QK_ROPE // 2
    t1, t2 = tf[..., :half], tf[..., half:]
    c, s = cos[None, :, None, :], sin[None, :, None, :]
    out = jnp.concatenate([t1 * c - t2 * s, t2 * c + t1 * s], axis=-1)
    return out.astype(t.dtype)


def bidir_attention(q, k, v):
    B, S, H, Dq = q.shape
    nb = S // Q_BLOCK
    qb = jnp.moveaxis((q * (Dq ** -0.5)).reshape(B, nb, Q_BLOCK, H, Dq), 1, 0)

    def block(q_blk):
        s = jnp.einsum('bqhd,bkhd->bhqk', q_blk, k, preferred_element_type=jnp.float32)
        p = jax.nn.softmax(s, axis=-1).astype(v.dtype)
        return jnp.einsum('bhqk,bkhd->bqhd', p, v)

    o = lax.map(block, qb)
    return jnp.moveaxis(o, 0, 1).reshape(B, S, H * V_HEAD)


def mla_mixer(c_q, c_kv, k_pe, q_lat_norm, kv_lat_norm, w_uq, w_uk, w_uv, q_norm, k_norm, cos, sin):
    B, S, _ = c_q.shape
    c_q = rmsnorm(c_q, q_lat_norm)
    c_kv = rmsnorm(c_kv, kv_lat_norm)
    q = (c_q @ w_uq).reshape(B, S, MLA_HEADS, QK_HEAD)
    k_nope = (c_kv @ w_uk).reshape(B, S, MLA_HEADS, QK_NOPE)
    v = (c_kv @ w_uv).reshape(B, S, MLA_HEADS, V_HEAD)
    k_pe_h = jnp.broadcast_to(k_pe[:, :, None, :], (B, S, MLA_HEADS, QK_ROPE))
    k = jnp.concatenate([k_nope, k_pe_h], axis=-1)
    q = rmsnorm(q, q_norm)
    k = rmsnorm(k, k_norm)
    q = jnp.concatenate([q[..., :QK_NOPE], apply_rope(q[..., QK_NOPE:], cos, sin)], axis=-1)
    k = jnp.concatenate([k[..., :QK_NOPE], apply_rope(k[..., QK_NOPE:], cos, sin)], axis=-1)
    return bidir_attention(q, k, v)


def pool_mixer(p, w_pool, pool_scale):
    B, S, C = p.shape
    pf = p.astype(jnp.float32)
    cs = jnp.concatenate([jnp.zeros((B, 1, C), jnp.float32), jnp.cumsum(pf, axis=1)], axis=1)
    idx = jnp.arange(S)
    outs = []
    for g, w in enumerate(POOL_WINDOWS):
        left = w // 2
        right = w - 1 - left
        lo = jnp.clip(idx - left, 0, S)
        hi = jnp.clip(idx + right + 1, 0, S)
        csg = cs[..., g * POOL_GROUP_DIM:(g + 1) * POOL_GROUP_DIM]
        wsum = jnp.take(csg, hi, axis=1) - jnp.take(csg, lo, axis=1)
        cnt = (hi - lo).astype(jnp.float32)[None, :, None]
        outs.append(wsum / cnt)
    pooled = jnp.stack(outs, axis=2)
    mixed = (pooled - pf.reshape(B, S, POOL_GROUPS, POOL_GROUP_DIM)).astype(p.dtype)
    y = jnp.einsum('bsgc,gcd->bsgd', mixed, w_pool).reshape(B, S, C)
    return y * pool_scale


def setup_inputs(seed: int = 0) -> dict:
    key = jax.random.key(seed)
    ks = jax.random.split(key, 20)
    f32 = jnp.float32

    def w(k, shape, fan_in):
        return jax.random.normal(k, shape, f32) * (fan_in ** -0.5)

    def gain(k, shape):
        return 1.0 + 0.02 * jax.random.normal(k, shape, f32)

    return {
        "x": jax.random.normal(ks[0], (BATCH, SEQ, D_MODEL), f32),
        "ffn1_norm": gain(ks[1], (DEPTH, D_MODEL)),
        "ffn1_w_gu": w(ks[2], (DEPTH, D_MODEL, 2 * D_FF), D_MODEL),
        "ffn1_w_down": w(ks[3], (DEPTH, D_FF, D_MODEL), D_FF),
        "mix_norm": gain(ks[4], (DEPTH, D_MODEL)),
        "w_in": w(ks[5], (DEPTH, D_MODEL, D_IN), D_MODEL),
        "q_lat_norm": gain(ks[6], (DEPTH, Q_LORA)),
        "kv_lat_norm": gain(ks[7], (DEPTH, KV_LORA)),
        "w_uq": w(ks[8], (DEPTH, Q_LORA, MLA_HEADS * QK_HEAD), Q_LORA),
        "w_uk": w(ks[9], (DEPTH, KV_LORA, MLA_HEADS * QK_NOPE), KV_LORA),
        "w_uv": w(ks[10], (DEPTH, KV_LORA, MLA_HEADS * V_HEAD), KV_LORA),
        "q_norm": gain(ks[11], (DEPTH, QK_HEAD)),
        "k_norm": gain(ks[12], (DEPTH, QK_HEAD)),
        "w_pool": w(ks[13], (DEPTH, POOL_GROUPS, POOL_GROUP_DIM, POOL_GROUP_DIM), POOL_GROUP_DIM),
        "pool_scale": gain(ks[14], (DEPTH, POOL_WIDTH)),
        "w_out": w(ks[15], (DEPTH, D_MIX, D_MODEL), D_MIX),
        "ffn2_norm": gain(ks[16], (DEPTH, D_MODEL)),
        "ffn2_w_gu": w(ks[17], (DEPTH, D_MODEL, 2 * D_FF), D_MODEL),
        "ffn2_w_down": w(ks[18], (DEPTH, D_FF, D_MODEL), D_FF),
    }


def reference(x, ffn1_norm, ffn1_w_gu, ffn1_w_down, mix_norm, w_in, q_lat_norm, kv_lat_norm,
              w_uq, w_uk, w_uv, q_norm, k_norm, w_pool, pool_scale, w_out,
              ffn2_norm, ffn2_w_gu, ffn2_w_down):
    S = x.shape[1]
    cos, sin = rope_tables(S)
    o_kv = Q_LORA
    o_pe = Q_LORA + KV_LORA
    o_pool = Q_LORA + KV_LORA + QK_ROPE
    for l in range(DEPTH):
        x = x + 0.5 * swiglu(rmsnorm(x, ffn1_norm[l]), ffn1_w_gu[l], ffn1_w_down[l])
        z = rmsnorm(x, mix_norm[l]) @ w_in[l]
        a = mla_mixer(z[..., :o_kv], z[..., o_kv:o_pe], z[..., o_pe:o_pool],
                      q_lat_norm[l], kv_lat_norm[l], w_uq[l], w_uk[l], w_uv[l],
                      q_norm[l], k_norm[l], cos, sin)
        b = pool_mixer(z[..., o_pool:], w_pool[l], pool_scale[l])
        x = x + jnp.concatenate([a, b], axis=-1) @ w_out[l]
        x = x + 0.5 * swiglu(rmsnorm(x, ffn2_norm[l]), ffn2_w_gu[l], ffn2_w_down[l])
    return x
```

```python
import functools

import jax
import jax.numpy as jnp
from jax import lax
from jax.experimental import pallas as pl
from jax.experimental.pallas import tpu as pltpu

D_MODEL = 1024
MLA_HEADS = 8
QK_NOPE = 64
QK_ROPE = 32
QK_HEAD = QK_NOPE + QK_ROPE
V_HEAD = 64
Q_LORA = 384
KV_LORA = 256
MLA_WIDTH = MLA_HEADS * V_HEAD
ROPE_THETA = 10000.0
POOL_WINDOWS = (2, 4, 8, 16)
POOL_GROUP_DIM = 128
POOL_WIDTH = 512
D_FF = 2816
EPS = 1e-6

LANES = 128
SUBLANES = 8
HEAD_PAD = LANES
HEADS_W = MLA_HEADS * HEAD_PAD
HALF_ROPE = QK_ROPE // 2
POOL_HALO = 8
C_Q, C_KV, C_POOL = 0, Q_LORA, Q_LORA + KV_LORA
C_PEA = C_POOL + POOL_WIDTH
C_PEB = C_PEA + LANES
D_IN_R = C_PEB + LANES

VMEM_LIMIT = 56 * 1024 * 1024

F32 = jnp.float32
BF16 = jnp.bfloat16


def _rmsnorm(x, g):
    ms = jnp.mean(x * x, axis=-1, keepdims=True)
    return x * lax.rsqrt(ms + EPS) * g


def _dot(a, b):
    return jnp.dot(a, b, preferred_element_type=F32)


def _const_spec(shape):
    nd = len(shape)
    return pl.BlockSpec(shape, lambda *_: (0,) * nd, pipeline_mode=pl.Buffered(1))


def _params(*sem):
    return pltpu.CompilerParams(dimension_semantics=sem, vmem_limit_bytes=VMEM_LIMIT)


def _ffn_kernel(x_ref, g_ref, wgu_ref, wd_ref, o_ref, *, tf):
    x = x_ref[...]
    h = _rmsnorm(x, g_ref[...]).astype(BF16)
    acc = jnp.zeros(x.shape, F32)
    for c in range(D_FF // tf):
        gate = _dot(h, wgu_ref[:, c * tf:(c + 1) * tf])
        up = _dot(h, wgu_ref[:, D_FF + c * tf:D_FF + (c + 1) * tf])
        act = (gate * jax.nn.sigmoid(gate) * up).astype(BF16)
        acc = acc + _dot(act, wd_ref[c * tf:(c + 1) * tf, :])
    o_ref[...] = x + 0.5 * acc


def _ffn(x, gain, w_gu, w_down, *, tm, tf):
    seq = x.shape[0]
    row = pl.BlockSpec((tm, D_MODEL), lambda i: (i, 0))
    return pl.pallas_call(
        functools.partial(_ffn_kernel, tf=tf),
        grid=(seq // tm,),
        in_specs=[row, _const_spec((1, D_MODEL)), _const_spec(w_gu.shape), _const_spec(w_down.shape)],
        out_specs=row,
        out_shape=jax.ShapeDtypeStruct(x.shape, F32),
        compiler_params=_params("parallel"),
        name="ffn",
    )(x, gain, w_gu, w_down)


def _pre_kernel(x_ref, g_ref, win_ref, qlat_ref, kvlat_ref, wuq_ref, wukv_ref, gains_ref,
                cos_ref, sin_ref, q_ref, k_ref, v_ref, zp_ref):
    h = _rmsnorm(x_ref[...], g_ref[...]).astype(BF16)
    z = _dot(h, win_ref[...])
    cq = _rmsnorm(z[:, C_Q:C_KV], qlat_ref[...]).astype(BF16)
    ckv = _rmsnorm(z[:, C_KV:C_POOL], kvlat_ref[...]).astype(BF16)
    zp_ref[...] = z[:, C_POOL:C_PEA]
    pe = z[:, C_PEA:C_PEB]
    pe_sw = z[:, C_PEB:D_IN_R]
    qq = _dot(cq, wuq_ref[...])
    kv = _dot(ckv, wukv_ref[...])
    cos = cos_ref[...]
    sin = sin_ref[...]
    gq, gq_sw = gains_ref[0:1, :], gains_ref[1:2, :]
    gk, gk_sw = gains_ref[2:3, :], gains_ref[3:4, :]
    one_at_vhead = (lax.broadcasted_iota(jnp.int32, (1, LANES), 1) == V_HEAD).astype(F32)
    scale = QK_HEAD ** -0.5
    for hd in range(MLA_HEADS):
        lo = hd * HEAD_PAD
        qh = qq[:, lo:lo + HEAD_PAD]
        qs = qq[:, HEADS_W + lo:HEADS_W + lo + HEAD_PAD]
        r = lax.rsqrt(jnp.sum(qh * qh, axis=-1, keepdims=True) * (1.0 / QK_HEAD) + EPS)
        qo = (qh * r * gq) * cos + (qs * r * gq_sw) * sin
        q_ref[:, lo:lo + HEAD_PAD] = (qo * scale).astype(BF16)
        kh = kv[:, lo:lo + HEAD_PAD] + pe
        r = lax.rsqrt(jnp.sum(kh * kh, axis=-1, keepdims=True) * (1.0 / QK_HEAD) + EPS)
        ko = (kh * r * gk) * cos + (pe_sw * r * gk_sw) * sin
        k_ref[:, lo:lo + HEAD_PAD] = ko.astype(BF16)
        vh = kv[:, HEADS_W + lo:HEADS_W + lo + HEAD_PAD] + one_at_vhead
        v_ref[:, lo:lo + HEAD_PAD] = vh.astype(BF16)


def _pre(x, gain, w_in_r, q_lat, kv_lat, w_uq, w_ukv, gains, cos, sin, *, tm):
    seq = x.shape[0]
    row = pl.BlockSpec((tm, D_MODEL), lambda i: (i, 0))
    heads = pl.BlockSpec((tm, HEADS_W), lambda i: (i, 0))
    tab = pl.BlockSpec((tm, LANES), lambda i: (i, 0))
    return pl.pallas_call(
        _pre_kernel,
        grid=(seq // tm,),
        in_specs=[row, _const_spec((1, D_MODEL)), _const_spec(w_in_r.shape),
                  _const_spec((1, Q_LORA)), _const_spec((1, KV_LORA)),
                  _const_spec(w_uq.shape), _const_spec(w_ukv.shape), _const_spec(gains.shape),
                  tab, tab],
        out_specs=[heads, heads, heads, pl.BlockSpec((tm, POOL_WIDTH), lambda i: (i, 0))],
        out_shape=[jax.ShapeDtypeStruct((seq, HEADS_W), BF16)] * 3
        + [jax.ShapeDtypeStruct((seq, POOL_WIDTH), F32)],
        compiler_params=_params("parallel"),
        name="pre",
    )(x, gain, w_in_r, q_lat, kv_lat, w_uq, w_ukv, gains, cos, sin)


def _attn_kernel(q_ref, k_ref, v_ref, o_ref, *, tk):
    q = q_ref[...]
    tq = q.shape[0]
    nk = k_ref.shape[0] // tk

    def body(j, carry):
        m, acc = carry
        off = pl.multiple_of(j * tk, tk)
        kc = k_ref[pl.ds(off, tk), :]
        vc = v_ref[pl.ds(off, tk), :]
        s = lax.dot_general(q, kc, (((1,), (1,)), ((), ())), preferred_element_type=F32)
        m_new = jnp.maximum(m, jnp.max(s, axis=1, keepdims=True))
        p = jnp.exp(s - m_new)
        alpha = jnp.exp(m - m_new)
        acc = alpha * acc + _dot(p.astype(BF16), vc)
        return m_new, acc

    m0 = jnp.full((tq, 1), -jnp.inf, F32)
    acc0 = jnp.zeros((tq, HEAD_PAD), F32)
    _, acc = lax.fori_loop(0, nk, body, (m0, acc0))
    denom = acc[:, V_HEAD:V_HEAD + 1]
    o_ref[...] = (acc / denom).astype(BF16)


def _attn(q, k, v, *, tq, tk):
    seq = q.shape[0]
    qspec = pl.BlockSpec((tq, HEAD_PAD), lambda h, i: (i, h))
    kvspec = pl.BlockSpec((seq, HEAD_PAD), lambda h, i: (0, h))
    return pl.pallas_call(
        functools.partial(_attn_kernel, tk=tk),
        grid=(MLA_HEADS, seq // tq),
        in_specs=[qspec, kvspec, kvspec],
        out_specs=qspec,
        out_shape=jax.ShapeDtypeStruct((seq, HEADS_W), BF16),
        compiler_params=_params("parallel", "parallel"),
        name="attn",
    )(q, k, v)


def _post_kernel(x_ref, o_ref, zc_ref, zprev_ref, znext_ref, wpool_ref, ps_ref, woa_ref, wob_ref,
                 out_ref, ext_ref, *, tm, seq):
    i = pl.program_id(0)
    last = pl.num_programs(0) - 1
    zc = zc_ref[...]
    ext_ref[0:POOL_HALO, :] = jnp.where(i > 0, zprev_ref[...], 0.0)
    ext_ref[POOL_HALO:POOL_HALO + tm, :] = zc
    ext_ref[POOL_HALO + tm:2 * POOL_HALO + tm, :] = jnp.where(i < last, znext_ref[...], 0.0)
    pos = i * tm + lax.broadcasted_iota(jnp.int32, (tm, 1), 0)
    parts = []
    for g, w in enumerate(POOL_WINDOWS):
        left = w // 2
        right = w - 1 - left
        c0 = g * POOL_GROUP_DIM
        wsum = ext_ref[POOL_HALO - left:POOL_HALO - left + tm, c0:c0 + POOL_GROUP_DIM]
        for d in range(-left + 1, right + 1):
            wsum = wsum + ext_ref[POOL_HALO + d:POOL_HALO + d + tm, c0:c0 + POOL_GROUP_DIM]
        cnt = (jnp.minimum(pos + right + 1, seq) - jnp.maximum(pos - left, 0)).astype(F32)
        mixed = (wsum / cnt - zc[:, c0:c0 + POOL_GROUP_DIM]).astype(BF16)
        y = _dot(mixed, wpool_ref[g]) * ps_ref[:, c0:c0 + POOL_GROUP_DIM]
        parts.append(y.astype(BF16))
    b = jnp.concatenate(parts, axis=1)
    out_ref[...] = x_ref[...] + (_dot(o_ref[...], woa_ref[...]) + _dot(b, wob_ref[...]))


def _post(x, o, zp, w_pool, pool_scale, w_out_a, w_out_b, *, tm):
    seq = x.shape[0]
    per = tm // POOL_HALO
    nblk = seq // POOL_HALO
    row = pl.BlockSpec((tm, D_MODEL), lambda i: (i, 0))
    return pl.pallas_call(
        functools.partial(_post_kernel, tm=tm, seq=seq),
        grid=(seq // tm,),
        in_specs=[row,
                  pl.BlockSpec((tm, HEADS_W), lambda i: (i, 0)),
                  pl.BlockSpec((tm, POOL_WIDTH), lambda i: (i, 0)),
                  pl.BlockSpec((POOL_HALO, POOL_WIDTH), lambda i: (jnp.maximum(i * per - 1, 0), 0)),
                  pl.BlockSpec((POOL_HALO, POOL_WIDTH), lambda i: (jnp.minimum((i + 1) * per, nblk - 1), 0)),
                  _const_spec(w_pool.shape), _const_spec((1, POOL_WIDTH)),
                  _const_spec(w_out_a.shape), _const_spec(w_out_b.shape)],
        out_specs=row,
        out_shape=jax.ShapeDtypeStruct(x.shape, F32),
        scratch_shapes=[pltpu.VMEM((tm + 2 * POOL_HALO, POOL_WIDTH), F32)],
        compiler_params=_params("parallel"),
        name="post",
    )(x, o, zp, zp, zp, w_pool, pool_scale, w_out_a, w_out_b)


def _swap_halves(t):
    return jnp.concatenate([t[..., HALF_ROPE:], t[..., :HALF_ROPE]], axis=-1)


def _head_block(nope, rope):
    pad = jnp.zeros(nope.shape[:-1] + (HEAD_PAD - QK_HEAD,), nope.dtype)
    return jnp.concatenate([nope, rope, pad], axis=-1)


def _layout_weights(w_in, w_uq, w_uk, w_uv, q_norm, k_norm, w_out):
    depth = w_in.shape[0]
    o_pe = Q_LORA + KV_LORA
    o_pool = o_pe + QK_ROPE
    w_pe = w_in[..., o_pe:o_pool]
    zero_nope = jnp.zeros(w_pe.shape[:-1] + (QK_NOPE,), w_pe.dtype)
    w_in_r = jnp.concatenate(
        [w_in[..., :o_pe], w_in[..., o_pool:], _head_block(zero_nope, w_pe),
         _head_block(zero_nope, _swap_halves(w_pe))], axis=-1).astype(BF16)

    uq = w_uq.reshape(depth, Q_LORA, MLA_HEADS, QK_HEAD)
    uq_main = _head_block(uq[..., :QK_NOPE], uq[..., QK_NOPE:])
    uq_swap = _head_block(jnp.zeros_like(uq[..., :QK_NOPE]), _swap_halves(uq[..., QK_NOPE:]))
    w_uq_r = jnp.concatenate([uq_main.reshape(depth, Q_LORA, HEADS_W),
                              uq_swap.reshape(depth, Q_LORA, HEADS_W)], axis=-1).astype(BF16)

    def pad_heads(w, width):
        w = w.reshape(depth, KV_LORA, MLA_HEADS, width)
        w = jnp.pad(w, ((0, 0), (0, 0), (0, 0), (0, HEAD_PAD - width)))
        return w.reshape(depth, KV_LORA, HEADS_W)

    w_ukv_r = jnp.concatenate([pad_heads(w_uk, QK_NOPE), pad_heads(w_uv, V_HEAD)], axis=-1).astype(BF16)

    def gain_rows(g):
        main = _head_block(g[..., :QK_NOPE], g[..., QK_NOPE:])
        swap = _head_block(jnp.zeros_like(g[..., :QK_NOPE]), _swap_halves(g[..., QK_NOPE:]))
        return main, swap

    gq, gq_sw = gain_rows(q_norm)
    gk, gk_sw = gain_rows(k_norm)
    zero = jnp.zeros_like(gq)
    gains = jnp.stack([gq, gq_sw, gk, gk_sw, zero, zero, zero, zero], axis=1)

    oa = w_out[:, :MLA_WIDTH].reshape(depth, MLA_HEADS, V_HEAD, D_MODEL)
    oa = jnp.pad(oa, ((0, 0), (0, 0), (0, HEAD_PAD - V_HEAD), (0, 0))).reshape(depth, HEADS_W, D_MODEL)
    return w_in_r, w_uq_r, w_ukv_r, gains, oa.astype(BF16), w_out[:, MLA_WIDTH:].astype(BF16)


def _rope_tables(seq):
    pos = jnp.arange(seq, dtype=F32)
    inv = ROPE_THETA ** (-jnp.arange(0, QK_ROPE, 2, dtype=F32) / QK_ROPE)
    ang = pos[:, None] * inv[None, :]
    cos, sin = jnp.cos(ang), jnp.sin(ang)
    ones = jnp.ones((seq, QK_NOPE), F32)
    zeros = jnp.zeros((seq, QK_NOPE), F32)
    return (_head_block(ones, jnp.concatenate([cos, cos], axis=-1)),
            _head_block(zeros, jnp.concatenate([-sin, sin], axis=-1)))


def _tiles(seq):
    tm = min(512, seq)
    return dict(tm=tm, tq=min(512, seq), tk=min(512, seq), tf=D_FF // 2)


def kernel(x, ffn1_norm, ffn1_w_gu, ffn1_w_down, mix_norm, w_in, q_lat_norm, kv_lat_norm, w_uq, w_uk, w_uv,
           q_norm, k_norm, w_pool, pool_scale, w_out, ffn2_norm, ffn2_w_gu, ffn2_w_down):
    batch, seq, _ = x.shape
    depth = w_in.shape[0]
    t = _tiles(seq)
    cos, sin = _rope_tables(seq)
    w_in_r, w_uq_r, w_ukv_r, gains, w_out_a, w_out_b = _layout_weights(
        w_in, w_uq, w_uk, w_uv, q_norm, k_norm, w_out)
    f1_gu, f1_d = ffn1_w_gu.astype(BF16), ffn1_w_down.astype(BF16)
    f2_gu, f2_d = ffn2_w_gu.astype(BF16), ffn2_w_down.astype(BF16)
    w_pool_b = w_pool.astype(BF16)

    outs = []
    for b in range(batch):
        xb = x[b]
        for l in range(depth):
            xb = _ffn(xb, ffn1_norm[l][None], f1_gu[l], f1_d[l], tm=t["tm"], tf=t["tf"])
            q, k, v, zp = _pre(xb, mix_norm[l][None], w_in_r[l], q_lat_norm[l][None], kv_lat_norm[l][None],
                               w_uq_r[l], w_ukv_r[l], gains[l], cos, sin, tm=t["tm"])
            o = _attn(q, k, v, tq=t["tq"], tk=t["tk"])
            xb = _post(xb, o, zp, w_pool_b[l], pool_scale[l][None], w_out_a[l], w_out_b[l], tm=t["tm"])
            xb = _ffn(xb, ffn2_norm[l][None], f2_gu[l], f2_d[l], tm=t["tm"], tf=t["tf"])
        outs.append(xb)
    return jnp.stack(outs, axis=0)
```

```python
import functools

import jax
import jax.numpy as jnp
from jax import lax
from jax.experimental import pallas as pl
from jax.experimental.pallas import tpu as pltpu

D_MODEL = 1024
MLA_HEADS = 8
QK_NOPE = 64
QK_ROPE = 32
QK_HEAD = QK_NOPE + QK_ROPE
V_HEAD = 64
Q_LORA = 384
KV_LORA = 256
MLA_WIDTH = MLA_HEADS * V_HEAD
ROPE_THETA = 10000.0
POOL_WINDOWS = (2, 4, 8, 16)
POOL_GROUP_DIM = 128
POOL_WIDTH = 512
D_FF = 2816
EPS = 1e-6
LOG2_E = 1.4426950408889634

LANES = 128
SUBLANES = 8
HEAD_PAD = LANES
HEADS_W = MLA_HEADS * HEAD_PAD
HALF_ROPE = QK_ROPE // 2
POOL_HALO = 8
C_Q, C_KV, C_POOL = 0, Q_LORA, Q_LORA + KV_LORA
C_PEA = C_POOL + POOL_WIDTH
C_PEB = C_PEA + LANES
D_IN_R = C_PEB + LANES

VMEM_LIMIT = 56 * 1024 * 1024

F32 = jnp.float32
BF16 = jnp.bfloat16


def _rmsnorm(x, g):
    ms = jnp.mean(x * x, axis=-1, keepdims=True)
    return x * lax.rsqrt(ms + EPS) * g


def _dot(a, b):
    return jnp.dot(a, b, preferred_element_type=F32)


def _const_spec(shape):
    nd = len(shape)
    return pl.BlockSpec(shape, lambda *_: (0,) * nd, pipeline_mode=pl.Buffered(1))


def _params(*sem):
    return pltpu.CompilerParams(dimension_semantics=sem, vmem_limit_bytes=VMEM_LIMIT)


def _ffn_kernel(x_ref, g_ref, wgu_ref, wd_ref, o_ref, *, tf):
    x = x_ref[...]
    h = _rmsnorm(x, g_ref[...]).astype(BF16)
    acc = jnp.zeros(x.shape, F32)
    for c in range(D_FF // tf):
        gate = _dot(h, wgu_ref[:, c * tf:(c + 1) * tf])
        up = _dot(h, wgu_ref[:, D_FF + c * tf:D_FF + (c + 1) * tf])
        act = (gate * jax.nn.sigmoid(gate) * up).astype(BF16)
        acc = acc + _dot(act, wd_ref[c * tf:(c + 1) * tf, :])
    o_ref[...] = x + 0.5 * acc


def _ffn(x, gain, w_gu, w_down, *, tm, tf):
    seq = x.shape[0]
    row = pl.BlockSpec((tm, D_MODEL), lambda i: (i, 0))
    return pl.pallas_call(
        functools.partial(_ffn_kernel, tf=tf),
        grid=(seq // tm,),
        in_specs=[row, _const_spec((1, D_MODEL)), _const_spec(w_gu.shape), _const_spec(w_down.shape)],
        out_specs=row,
        out_shape=jax.ShapeDtypeStruct(x.shape, F32),
        compiler_params=_params("parallel"),
        name="ffn",
    )(x, gain, w_gu, w_down)


def _pre_kernel(x_ref, g_ref, win_ref, qlat_ref, kvlat_ref, wuq_ref, wukv_ref, gains_ref,
                cos_ref, sin_ref, q_ref, k_ref, v_ref, zp_ref):
    h = _rmsnorm(x_ref[...], g_ref[...]).astype(BF16)
    z = _dot(h, win_ref[...])
    cq = _rmsnorm(z[:, C_Q:C_KV], qlat_ref[...]).astype(BF16)
    ckv = _rmsnorm(z[:, C_KV:C_POOL], kvlat_ref[...]).astype(BF16)
    zp_ref[...] = z[:, C_POOL:C_PEA]
    pe = z[:, C_PEA:C_PEB]
    pe_sw = z[:, C_PEB:D_IN_R]
    qq = _dot(cq, wuq_ref[...])
    kv = _dot(ckv, wukv_ref[...])
    cos = cos_ref[...]
    sin = sin_ref[...]
    gq, gq_sw = gains_ref[0:1, :], gains_ref[1:2, :]
    gk, gk_sw = gains_ref[2:3, :], gains_ref[3:4, :]
    one_at_vhead = (lax.broadcasted_iota(jnp.int32, (1, LANES), 1) == V_HEAD).astype(F32)
    scale = QK_HEAD ** -0.5 * LOG2_E
    for hd in range(MLA_HEADS):
        lo = hd * HEAD_PAD
        qh = qq[:, lo:lo + HEAD_PAD]
        qs = qq[:, HEADS_W + lo:HEADS_W + lo + HEAD_PAD]
        r = lax.rsqrt(jnp.sum(qh * qh, axis=-1, keepdims=True) * (1.0 / QK_HEAD) + EPS)
        qo = (qh * r * gq) * cos + (qs * r * gq_sw) * sin
        q_ref[:, lo:lo + HEAD_PAD] = (qo * scale).astype(BF16)
        kh = kv[:, lo:lo + HEAD_PAD] + pe
        r = lax.rsqrt(jnp.sum(kh * kh, axis=-1, keepdims=True) * (1.0 / QK_HEAD) + EPS)
        ko = (kh * r * gk) * cos + (pe_sw * r * gk_sw) * sin
        k_ref[:, lo:lo + HEAD_PAD] = ko.astype(BF16)
        vh = kv[:, HEADS_W + lo:HEADS_W + lo + HEAD_PAD] + one_at_vhead
        v_ref[:, lo:lo + HEAD_PAD] = vh.astype(BF16)


def _pre(x, gain, w_in_r, q_lat, kv_lat, w_uq, w_ukv, gains, cos, sin, *, tm):
    seq = x.shape[0]
    row = pl.BlockSpec((tm, D_MODEL), lambda i: (i, 0))
    heads = pl.BlockSpec((tm, HEADS_W), lambda i: (i, 0))
    tab = pl.BlockSpec((tm, LANES), lambda i: (i, 0))
    return pl.pallas_call(
        _pre_kernel,
        grid=(seq // tm,),
        in_specs=[row, _const_spec((1, D_MODEL)), _const_spec(w_in_r.shape),
                  _const_spec((1, Q_LORA)), _const_spec((1, KV_LORA)),
                  _const_spec(w_uq.shape), _const_spec(w_ukv.shape), _const_spec(gains.shape),
                  tab, tab],
        out_specs=[heads, heads, heads, pl.BlockSpec((tm, POOL_WIDTH), lambda i: (i, 0))],
        out_shape=[jax.ShapeDtypeStruct((seq, HEADS_W), BF16)] * 3
        + [jax.ShapeDtypeStruct((seq, POOL_WIDTH), F32)],
        compiler_params=_params("parallel"),
        name="pre",
    )(x, gain, w_in_r, q_lat, kv_lat, w_uq, w_ukv, gains, cos, sin)


def _attn_kernel(q_ref, k_ref, v_ref, o_ref, *, tk):
    q = q_ref[...]
    tq = q.shape[0]
    nk = k_ref.shape[0] // tk

    def body(j, carry):
        m, acc = carry
        off = pl.multiple_of(j * tk, tk)
        kc = k_ref[pl.ds(off, tk), :]
        vc = v_ref[pl.ds(off, tk), :]
        s = lax.dot_general(q, kc, (((1,), (1,)), ((), ())), preferred_element_type=F32)
        m_new = jnp.maximum(m, jnp.max(s, axis=1, keepdims=True))
        p = jnp.exp2(s - m_new)
        alpha = jnp.exp2(m - m_new)
        return m_new, alpha * acc + _dot(p.astype(BF16), vc)

    m0 = jnp.full((tq, 1), -jnp.inf, F32)
    acc0 = jnp.zeros((tq, HEAD_PAD), F32)
    _, acc = lax.fori_loop(0, nk, body, (m0, acc0))
    denom = acc[:, V_HEAD:V_HEAD + 1]
    o_ref[...] = (acc / denom).astype(BF16)


def _attn(q, k, v, *, tq, tk):
    seq = q.shape[0]
    qspec = pl.BlockSpec((tq, HEAD_PAD), lambda h, i: (i, h))
    kvspec = pl.BlockSpec((seq, HEAD_PAD), lambda h, i: (0, h))
    return pl.pallas_call(
        functools.partial(_attn_kernel, tk=tk),
        grid=(MLA_HEADS, seq // tq),
        in_specs=[qspec, kvspec, kvspec],
        out_specs=qspec,
        out_shape=jax.ShapeDtypeStruct((seq, HEADS_W), BF16),
        compiler_params=_params("parallel", "parallel"),
        name="attn",
    )(q, k, v)


def _post_kernel(x_ref, o_ref, zc_ref, zprev_ref, znext_ref, wpool_ref, ps_ref, woa_ref, wob_ref,
                 out_ref, ext_ref, *, tm, seq):
    i = pl.program_id(0)
    last = pl.num_programs(0) - 1
    zc = zc_ref[...]
    ext_ref[0:POOL_HALO, :] = jnp.where(i > 0, zprev_ref[...], 0.0)
    ext_ref[POOL_HALO:POOL_HALO + tm, :] = zc
    ext_ref[POOL_HALO + tm:2 * POOL_HALO + tm, :] = jnp.where(i < last, znext_ref[...], 0.0)
    pos = i * tm + lax.broadcasted_iota(jnp.int32, (tm, 1), 0)
    parts = []
    for g, w in enumerate(POOL_WINDOWS):
        left = w // 2
        right = w - 1 - left
        c0 = g * POOL_GROUP_DIM
        wsum = ext_ref[POOL_HALO - left:POOL_HALO - left + tm, c0:c0 + POOL_GROUP_DIM]
        for d in range(-left + 1, right + 1):
            wsum = wsum + ext_ref[POOL_HALO + d:POOL_HALO + d + tm, c0:c0 + POOL_GROUP_DIM]
        cnt = (jnp.minimum(pos + right + 1, seq) - jnp.maximum(pos - left, 0)).astype(F32)
        mixed = (wsum / cnt - zc[:, c0:c0 + POOL_GROUP_DIM]).astype(BF16)
        y = _dot(mixed, wpool_ref[g]) * ps_ref[:, c0:c0 + POOL_GROUP_DIM]
        parts.append(y.astype(BF16))
    b = jnp.concatenate(parts, axis=1)
    out_ref[...] = x_ref[...] + (_dot(o_ref[...], woa_ref[...]) + _dot(b, wob_ref[...]))


def _post(x, o, zp, w_pool, pool_scale, w_out_a, w_out_b, *, tm):
    seq = x.shape[0]
    per = tm // POOL_HALO
    nblk = seq // POOL_HALO
    row = pl.BlockSpec((tm, D_MODEL), lambda i: (i, 0))
    return pl.pallas_call(
        functools.partial(_post_kernel, tm=tm, seq=seq),
        grid=(seq // tm,),
        in_specs=[row,
                  pl.BlockSpec((tm, HEADS_W), lambda i: (i, 0)),
                  pl.BlockSpec((tm, POOL_WIDTH), lambda i: (i, 0)),
                  pl.BlockSpec((POOL_HALO, POOL_WIDTH), lambda i: (jnp.maximum(i * per - 1, 0), 0)),
                  pl.BlockSpec((POOL_HALO, POOL_WIDTH), lambda i: (jnp.minimum((i + 1) * per, nblk - 1), 0)),
                  _const_spec(w_pool.shape), _const_spec((1, POOL_WIDTH)),
                  _const_spec(w_out_a.shape), _const_spec(w_out_b.shape)],
        out_specs=row,
        out_shape=jax.ShapeDtypeStruct(x.shape, F32),
        scratch_shapes=[pltpu.VMEM((tm + 2 * POOL_HALO, POOL_WIDTH), F32)],
        compiler_params=_params("parallel"),
        name="post",
    )(x, o, zp, zp, zp, w_pool, pool_scale, w_out_a, w_out_b)


def _swap_halves(t):
    return jnp.concatenate([t[..., HALF_ROPE:], t[..., :HALF_ROPE]], axis=-1)


def _head_block(nope, rope):
    pad = jnp.zeros(nope.shape[:-1] + (HEAD_PAD - QK_HEAD,), nope.dtype)
    return jnp.concatenate([nope, rope, pad], axis=-1)


def _layout_weights(w_in, w_uq, w_uk, w_uv, q_norm, k_norm, w_out):
    depth = w_in.shape[0]
    o_pe = Q_LORA + KV_LORA
    o_pool = o_pe + QK_ROPE
    w_pe = w_in[..., o_pe:o_pool]
    zero_nope = jnp.zeros(w_pe.shape[:-1] + (QK_NOPE,), w_pe.dtype)
    w_in_r = jnp.concatenate(
        [w_in[..., :o_pe], w_in[..., o_pool:], _head_block(zero_nope, w_pe),
         _head_block(zero_nope, _swap_halves(w_pe))], axis=-1).astype(BF16)

    uq = w_uq.reshape(depth, Q_LORA, MLA_HEADS, QK_HEAD)
    uq_main = _head_block(uq[..., :QK_NOPE], uq[..., QK_NOPE:])
    uq_swap = _head_block(jnp.zeros_like(uq[..., :QK_NOPE]), _swap_halves(uq[..., QK_NOPE:]))
    w_uq_r = jnp.concatenate([uq_main.reshape(depth, Q_LORA, HEADS_W),
                              uq_swap.reshape(depth, Q_LORA, HEADS_W)], axis=-1).astype(BF16)

    def pad_heads(w, width):
        w = w.reshape(depth, KV_LORA, MLA_HEADS, width)
        w = jnp.pad(w, ((0, 0), (0, 0), (0, 0), (0, HEAD_PAD - width)))
        return w.reshape(depth, KV_LORA, HEADS_W)

    w_ukv_r = jnp.concatenate([pad_heads(w_uk, QK_NOPE), pad_heads(w_uv, V_HEAD)], axis=-1).astype(BF16)

    def gain_rows(g):
        main = _head_block(g[..., :QK_NOPE], g[..., QK_NOPE:])
        swap = _head_block(jnp.zeros_like(g[..., :QK_NOPE]), _swap_halves(g[..., QK_NOPE:]))
        return main, swap

    gq, gq_sw = gain_rows(q_norm)
    gk, gk_sw = gain_rows(k_norm)
    zero = jnp.zeros_like(gq)
    gains = jnp.stack([gq, gq_sw, gk, gk_sw, zero, zero, zero, zero], axis=1)

    oa = w_out[:, :MLA_WIDTH].reshape(depth, MLA_HEADS, V_HEAD, D_MODEL)
    oa = jnp.pad(oa, ((0, 0), (0, 0), (0, HEAD_PAD - V_HEAD), (0, 0))).reshape(depth, HEADS_W, D_MODEL)
    return w_in_r, w_uq_r, w_ukv_r, gains, oa.astype(BF16), w_out[:, MLA_WIDTH:].astype(BF16)


def _rope_tables(seq):
    pos = jnp.arange(seq, dtype=F32)
    inv = ROPE_THETA ** (-jnp.arange(0, QK_ROPE, 2, dtype=F32) / QK_ROPE)
    ang = pos[:, None] * inv[None, :]
    cos, sin = jnp.cos(ang), jnp.sin(ang)
    ones = jnp.ones((seq, QK_NOPE), F32)
    zeros = jnp.zeros((seq, QK_NOPE), F32)
    return (_head_block(ones, jnp.concatenate([cos, cos], axis=-1)),
            _head_block(zeros, jnp.concatenate([-sin, sin], axis=-1)))


def _tiles(seq):
    tm = min(512, seq)
    return dict(tm=tm, tq=min(1024, seq), tk=min(2048, seq), tf=D_FF // 2)


def kernel(x, ffn1_norm, ffn1_w_gu, ffn1_w_down, mix_norm, w_in, q_lat_norm, kv_lat_norm, w_uq, w_uk, w_uv,
           q_norm, k_norm, w_pool, pool_scale, w_out, ffn2_norm, ffn2_w_gu, ffn2_w_down):
    batch, seq, _ = x.shape
    depth = w_in.shape[0]
    t = _tiles(seq)
    cos, sin = _rope_tables(seq)
    w_in_r, w_uq_r, w_ukv_r, gains, w_out_a, w_out_b = _layout_weights(
        w_in, w_uq, w_uk, w_uv, q_norm, k_norm, w_out)
    f1_gu, f1_d = ffn1_w_gu.astype(BF16), ffn1_w_down.astype(BF16)
    f2_gu, f2_d = ffn2_w_gu.astype(BF16), ffn2_w_down.astype(BF16)
    w_pool_b = w_pool.astype(BF16)

    outs = []
    for b in range(batch):
        xb = x[b]
        for l in range(depth):
            xb = _ffn(xb, ffn1_norm[l][None], f1_gu[l], f1_d[l], tm=t["tm"], tf=t["tf"])
            q, k, v, zp = _pre(xb, mix_norm[l][None], w_in_r[l], q_lat_norm[l][None], kv_lat_norm[l][None],
                               w_uq_r[l], w_ukv_r[l], gains[l], cos, sin, tm=t["tm"])
            o = _attn(q, k, v, tq=t["tq"], tk=t["tk"])
            xb = _post(xb, o, zp, w_pool_b[l], pool_scale[l][None], w_out_a[l], w_out_b[l], tm=t["tm"])
            xb = _ffn(xb, ffn2_norm[l][None], f2_gu[l], f2_d[l], tm=t["tm"], tf=t["tf"])
        outs.append(xb)
    return jnp.stack(outs, axis=0)
```

```python
import functools

import jax
import jax.numpy as jnp
from jax import lax
from jax.experimental import pallas as pl
from jax.experimental.pallas import tpu as pltpu

D_MODEL = 1024
MLA_HEADS = 8
QK_NOPE = 64
QK_ROPE = 32
QK_HEAD = QK_NOPE + QK_ROPE
V_HEAD = 64
Q_LORA = 384
KV_LORA = 256
MLA_WIDTH = MLA_HEADS * V_HEAD
ROPE_THETA = 10000.0
POOL_WINDOWS = (2, 4, 8, 16)
POOL_GROUP_DIM = 128
POOL_WIDTH = 512
D_FF = 2816
EPS = 1e-6
LOG2_E = 1.4426950408889634

LANES = 128
SUBLANES = 8
HEAD_PAD = LANES
HEADS_W = MLA_HEADS * HEAD_PAD
HALF_ROPE = QK_ROPE // 2
POOL_HALO = 8
KSUB = 256
VT_ROWS = 80
C_Q, C_KV, C_POOL = 0, Q_LORA, Q_LORA + KV_LORA
C_PEA = C_POOL + POOL_WIDTH
C_PEB = C_PEA + LANES
D_IN_R = C_PEB + LANES

VMEM_LIMIT = 56 * 1024 * 1024

F32 = jnp.float32
BF16 = jnp.bfloat16


def _rmsnorm(x, g):
    ms = jnp.mean(x * x, axis=-1, keepdims=True)
    return x * lax.rsqrt(ms + EPS) * g


def _dot(a, b):
    return jnp.dot(a, b, preferred_element_type=F32)


def _const_spec(shape):
    nd = len(shape)
    return pl.BlockSpec(shape, lambda *_: (0,) * nd, pipeline_mode=pl.Buffered(1))


def _params(*sem):
    return pltpu.CompilerParams(dimension_semantics=sem, vmem_limit_bytes=VMEM_LIMIT)


def _ffn_kernel(x_ref, g_ref, wgu_ref, wd_ref, o_ref, *, tf):
    x = x_ref[...]
    h = _rmsnorm(x, g_ref[...]).astype(BF16)
    acc = jnp.zeros(x.shape, F32)
    for c in range(D_FF // tf):
        gate = _dot(h, wgu_ref[:, c * tf:(c + 1) * tf])
        up = _dot(h, wgu_ref[:, D_FF + c * tf:D_FF + (c + 1) * tf])
        act = (gate * jax.nn.sigmoid(gate) * up).astype(BF16)
        acc = acc + _dot(act, wd_ref[c * tf:(c + 1) * tf, :])
    o_ref[...] = x + 0.5 * acc


def _ffn(x, gain, w_gu, w_down, *, tm, tf):
    seq = x.shape[0]
    row = pl.BlockSpec((tm, D_MODEL), lambda i: (i, 0))
    return pl.pallas_call(
        functools.partial(_ffn_kernel, tf=tf),
        grid=(seq // tm,),
        in_specs=[row, _const_spec((1, D_MODEL)), _const_spec(w_gu.shape), _const_spec(w_down.shape)],
        out_specs=row,
        out_shape=jax.ShapeDtypeStruct(x.shape, F32),
        compiler_params=_params("parallel"),
        name="ffn",
    )(x, gain, w_gu, w_down)


def _pre_kernel(x_ref, g_ref, win_ref, qlat_ref, kvlat_ref, wuq_ref, wuk_ref, wuvt_ref, gains_ref,
                cos_ref, sin_ref, q_ref, k_ref, vt_ref, zp_ref):
    h = _rmsnorm(x_ref[...], g_ref[...]).astype(BF16)
    z = _dot(h, win_ref[...])
    cq = _rmsnorm(z[:, C_Q:C_KV], qlat_ref[...]).astype(BF16)
    ckv = _rmsnorm(z[:, C_KV:C_POOL], kvlat_ref[...]).astype(BF16)
    zp_ref[...] = z[:, C_POOL:C_PEA]
    pe = z[:, C_PEA:C_PEB]
    pe_sw = z[:, C_PEB:D_IN_R]
    qq = _dot(cq, wuq_ref[...])
    kk = _dot(ckv, wuk_ref[...])
    vt = lax.dot_general(wuvt_ref[...], ckv, (((1,), (1,)), ((), ())), preferred_element_type=F32)
    cos = cos_ref[...]
    sin = sin_ref[...]
    gq, gq_sw = gains_ref[0:1, :], gains_ref[1:2, :]
    gk, gk_sw = gains_ref[2:3, :], gains_ref[3:4, :]
    one_at_vhead = (lax.broadcasted_iota(jnp.int32, (VT_ROWS, 1), 0) == V_HEAD).astype(F32)
    scale = QK_HEAD ** -0.5 * LOG2_E
    for hd in range(MLA_HEADS):
        lo = hd * HEAD_PAD
        qh = qq[:, lo:lo + HEAD_PAD]
        qs = qq[:, HEADS_W + lo:HEADS_W + lo + HEAD_PAD]
        r = lax.rsqrt(jnp.sum(qh * qh, axis=-1, keepdims=True) * (1.0 / QK_HEAD) + EPS)
        qo = (qh * r * gq) * cos + (qs * r * gq_sw) * sin
        q_ref[:, lo:lo + HEAD_PAD] = (qo * scale).astype(BF16)
        kh = kk[:, lo:lo + HEAD_PAD] + pe
        r = lax.rsqrt(jnp.sum(kh * kh, axis=-1, keepdims=True) * (1.0 / QK_HEAD) + EPS)
        ko = (kh * r * gk) * cos + (pe_sw * r * gk_sw) * sin
        k_ref[:, lo:lo + HEAD_PAD] = ko.astype(BF16)
        vt_ref[hd] = (vt[hd * VT_ROWS:(hd + 1) * VT_ROWS, :] + one_at_vhead).astype(BF16)


def _pre(x, gain, w_in_r, q_lat, kv_lat, w_uq, w_uk, w_uvt, gains, cos, sin, *, tm, tk):
    seq = x.shape[0]
    per = tk // tm
    row = pl.BlockSpec((tm, D_MODEL), lambda i: (i, 0))
    heads = pl.BlockSpec((tm, HEADS_W), lambda i: (i, 0))
    tab = pl.BlockSpec((tm, LANES), lambda i: (i, 0))
    vt_spec = pl.BlockSpec((MLA_HEADS, None, VT_ROWS, tm), lambda i: (0, i // per, 0, i % per))
    return pl.pallas_call(
        _pre_kernel,
        grid=(seq // tm,),
        in_specs=[row, _const_spec((1, D_MODEL)), _const_spec(w_in_r.shape),
                  _const_spec((1, Q_LORA)), _const_spec((1, KV_LORA)),
                  _const_spec(w_uq.shape), _const_spec(w_uk.shape), _const_spec(w_uvt.shape),
                  _const_spec(gains.shape), tab, tab],
        out_specs=[heads, heads, vt_spec, pl.BlockSpec((tm, POOL_WIDTH), lambda i: (i, 0))],
        out_shape=[jax.ShapeDtypeStruct((seq, HEADS_W), BF16)] * 2
        + [jax.ShapeDtypeStruct((MLA_HEADS, seq // tk, VT_ROWS, tk), BF16),
           jax.ShapeDtypeStruct((seq, POOL_WIDTH), F32)],
        compiler_params=_params("parallel"),
        name="pre",
    )(x, gain, w_in_r, q_lat, kv_lat, w_uq, w_uk, w_uvt, gains, cos, sin)


def _attn_kernel(q_ref, k_ref, vt_ref, o_ref, s_ref, *, tk):
    tq = q_ref.shape[0]
    nk = k_ref.shape[0] // tk
    nsub = tk // KSUB
    qt = q_ref[...].astype(F32).T.astype(BF16)

    def scores_sub(j, c, slot, cmax):
        kc = k_ref[pl.ds(pl.multiple_of(j * tk + c * KSUB, KSUB), KSUB), :]
        st = _dot(kc, qt)
        s_ref[slot, c * KSUB:(c + 1) * KSUB, :] = st
        return jnp.maximum(cmax, jnp.max(st.reshape(KSUB // SUBLANES, SUBLANES, tq), axis=0))

    def step(j, slot, cmax, m, acc, j_next):
        m_new = jnp.maximum(m, jnp.max(cmax, axis=0, keepdims=True))
        acc = jnp.exp2(m - m_new) * acc
        cmax_next = jnp.full((SUBLANES, tq), -jnp.inf, F32)
        for c in range(nsub):
            cmax_next = scores_sub(j_next, c, 1 - slot, cmax_next)
            pt = jnp.exp2(s_ref[slot, c * KSUB:(c + 1) * KSUB, :] - m_new).astype(BF16)
            acc = acc + _dot(vt_ref[j, :, c * KSUB:(c + 1) * KSUB], pt)
        return cmax_next, m_new, acc

    def pair(jj, carry):
        cmax, m, acc = carry
        j = 2 * jj
        cmax, m, acc = step(j, 0, cmax, m, acc, j + 1)
        return step(j + 1, 1, cmax, m, acc, jnp.minimum(j + 2, nk - 1))

    cmax0 = jnp.full((SUBLANES, tq), -jnp.inf, F32)
    for c in range(nsub):
        cmax0 = scores_sub(0, c, 0, cmax0)
    m0 = jnp.full((1, tq), -jnp.inf, F32)
    acc0 = jnp.zeros((VT_ROWS, tq), F32)
    _, _, acc = lax.fori_loop(0, nk // 2, pair, (cmax0, m0, acc0))
    out_t = acc / acc[V_HEAD:V_HEAD + 1, :]
    out_t = jnp.concatenate([out_t, jnp.zeros((HEAD_PAD - VT_ROWS, tq), F32)], axis=0)
    o_ref[...] = out_t.T.astype(BF16)


def _attn(q, k, vt, *, tq, tk):
    seq = q.shape[0]
    qspec = pl.BlockSpec((tq, HEAD_PAD), lambda h, i: (i, h))
    kspec = pl.BlockSpec((seq, HEAD_PAD), lambda h, i: (0, h))
    vspec = pl.BlockSpec((None, seq // tk, VT_ROWS, tk), lambda h, i: (h, 0, 0, 0))
    return pl.pallas_call(
        functools.partial(_attn_kernel, tk=tk),
        grid=(MLA_HEADS, seq // tq),
        in_specs=[qspec, kspec, vspec],
        out_specs=qspec,
        out_shape=jax.ShapeDtypeStruct((seq, HEADS_W), BF16),
        scratch_shapes=[pltpu.VMEM((2, tk, tq), F32)],
        compiler_params=_params("parallel", "parallel"),
        name="attn",
    )(q, k, vt)


def _post_kernel(x_ref, o_ref, zc_ref, zprev_ref, znext_ref, wpool_ref, ps_ref, woa_ref, wob_ref,
                 out_ref, ext_ref, *, tm, seq):
    i = pl.program_id(0)
    last = pl.num_programs(0) - 1
    zc = zc_ref[...]
    ext_ref[0:POOL_HALO, :] = jnp.where(i > 0, zprev_ref[...], 0.0)
    ext_ref[POOL_HALO:POOL_HALO + tm, :] = zc
    ext_ref[POOL_HALO + tm:2 * POOL_HALO + tm, :] = jnp.where(i < last, znext_ref[...], 0.0)
    pos = i * tm + lax.broadcasted_iota(jnp.int32, (tm, 1), 0)
    parts = []
    for g, w in enumerate(POOL_WINDOWS):
        left = w // 2
        right = w - 1 - left
        c0 = g * POOL_GROUP_DIM
        wsum = ext_ref[POOL_HALO - left:POOL_HALO - left + tm, c0:c0 + POOL_GROUP_DIM]
        for d in range(-left + 1, right + 1):
            wsum = wsum + ext_ref[POOL_HALO + d:POOL_HALO + d + tm, c0:c0 + POOL_GROUP_DIM]
        cnt = (jnp.minimum(pos + right + 1, seq) - jnp.maximum(pos - left, 0)).astype(F32)
        mixed = (wsum / cnt - zc[:, c0:c0 + POOL_GROUP_DIM]).astype(BF16)
        y = _dot(mixed, wpool_ref[g]) * ps_ref[:, c0:c0 + POOL_GROUP_DIM]
        parts.append(y.astype(BF16))
    b = jnp.concatenate(parts, axis=1)
    out_ref[...] = x_ref[...] + (_dot(o_ref[...], woa_ref[...]) + _dot(b, wob_ref[...]))


def _post(x, o, zp, w_pool, pool_scale, w_out_a, w_out_b, *, tm):
    seq = x.shape[0]
    per = tm // POOL_HALO
    nblk = seq // POOL_HALO
    row = pl.BlockSpec((tm, D_MODEL), lambda i: (i, 0))
    return pl.pallas_call(
        functools.partial(_post_kernel, tm=tm, seq=seq),
        grid=(seq // tm,),
        in_specs=[row,
                  pl.BlockSpec((tm, HEADS_W), lambda i: (i, 0)),
                  pl.BlockSpec((tm, POOL_WIDTH), lambda i: (i, 0)),
                  pl.BlockSpec((POOL_HALO, POOL_WIDTH), lambda i: (jnp.maximum(i * per - 1, 0), 0)),
                  pl.BlockSpec((POOL_HALO, POOL_WIDTH), lambda i: (jnp.minimum((i + 1) * per, nblk - 1), 0)),
                  _const_spec(w_pool.shape), _const_spec((1, POOL_WIDTH)),
                  _const_spec(w_out_a.shape), _const_spec(w_out_b.shape)],
        out_specs=row,
        out_shape=jax.ShapeDtypeStruct(x.shape, F32),
        scratch_shapes=[pltpu.VMEM((tm + 2 * POOL_HALO, POOL_WIDTH), F32)],
        compiler_params=_params("parallel"),
        name="post",
    )(x, o, zp, zp, zp, w_pool, pool_scale, w_out_a, w_out_b)


def _swap_halves(t):
    return jnp.concatenate([t[..., HALF_ROPE:], t[..., :HALF_ROPE]], axis=-1)


def _head_block(nope, rope):
    pad = jnp.zeros(nope.shape[:-1] + (HEAD_PAD - QK_HEAD,), nope.dtype)
    return jnp.concatenate([nope, rope, pad], axis=-1)


def _layout_weights(w_in, w_uq, w_uk, w_uv, q_norm, k_norm, w_out):
    depth = w_in.shape[0]
    o_pe = Q_LORA + KV_LORA
    o_pool = o_pe + QK_ROPE
    w_pe = w_in[..., o_pe:o_pool]
    zero_nope = jnp.zeros(w_pe.shape[:-1] + (QK_NOPE,), w_pe.dtype)
    w_in_r = jnp.concatenate(
        [w_in[..., :o_pe], w_in[..., o_pool:], _head_block(zero_nope, w_pe),
         _head_block(zero_nope, _swap_halves(w_pe))], axis=-1).astype(BF16)

    uq = w_uq.reshape(depth, Q_LORA, MLA_HEADS, QK_HEAD)
    uq_main = _head_block(uq[..., :QK_NOPE], uq[..., QK_NOPE:])
    uq_swap = _head_block(jnp.zeros_like(uq[..., :QK_NOPE]), _swap_halves(uq[..., QK_NOPE:]))
    w_uq_r = jnp.concatenate([uq_main.reshape(depth, Q_LORA, HEADS_W),
                              uq_swap.reshape(depth, Q_LORA, HEADS_W)], axis=-1).astype(BF16)

    uk = jnp.pad(w_uk.reshape(depth, KV_LORA, MLA_HEADS, QK_NOPE),
                 ((0, 0), (0, 0), (0, 0), (0, HEAD_PAD - QK_NOPE)))
    w_uk_r = uk.reshape(depth, KV_LORA, HEADS_W).astype(BF16)
    uvt = jnp.transpose(w_uv.reshape(depth, KV_LORA, MLA_HEADS, V_HEAD), (0, 2, 3, 1))
    uvt = jnp.pad(uvt, ((0, 0), (0, 0), (0, VT_ROWS - V_HEAD), (0, 0)))
    w_uvt_r = uvt.reshape(depth, MLA_HEADS * VT_ROWS, KV_LORA).astype(BF16)

    def gain_rows(g):
        main = _head_block(g[..., :QK_NOPE], g[..., QK_NOPE:])
        swap = _head_block(jnp.zeros_like(g[..., :QK_NOPE]), _swap_halves(g[..., QK_NOPE:]))
        return main, swap

    gq, gq_sw = gain_rows(q_norm)
    gk, gk_sw = gain_rows(k_norm)
    zero = jnp.zeros_like(gq)
    gains = jnp.stack([gq, gq_sw, gk, gk_sw, zero, zero, zero, zero], axis=1)

    oa = w_out[:, :MLA_WIDTH].reshape(depth, MLA_HEADS, V_HEAD, D_MODEL)
    oa = jnp.pad(oa, ((0, 0), (0, 0), (0, HEAD_PAD - V_HEAD), (0, 0))).reshape(depth, HEADS_W, D_MODEL)
    return w_in_r, w_uq_r, w_uk_r, w_uvt_r, gains, oa.astype(BF16), w_out[:, MLA_WIDTH:].astype(BF16)


def _rope_tables(seq):
    pos = jnp.arange(seq, dtype=F32)
    inv = ROPE_THETA ** (-jnp.arange(0, QK_ROPE, 2, dtype=F32) / QK_ROPE)
    ang = pos[:, None] * inv[None, :]
    cos, sin = jnp.cos(ang), jnp.sin(ang)
    ones = jnp.ones((seq, QK_NOPE), F32)
    zeros = jnp.zeros((seq, QK_NOPE), F32)
    return (_head_block(ones, jnp.concatenate([cos, cos], axis=-1)),
            _head_block(zeros, jnp.concatenate([-sin, sin], axis=-1)))


def _tiles(seq):
    tm = min(512, seq)
    return dict(tm=tm, tq=min(1024, seq), tk=min(2048, seq // 2), tf=D_FF // 2)


def kernel(x, ffn1_norm, ffn1_w_gu, ffn1_w_down, mix_norm, w_in, q_lat_norm, kv_lat_norm, w_uq, w_uk, w_uv,
           q_norm, k_norm, w_pool, pool_scale, w_out, ffn2_norm, ffn2_w_gu, ffn2_w_down):
    batch, seq, _ = x.shape
    depth = w_in.shape[0]
    t = _tiles(seq)
    cos, sin = _rope_tables(seq)
    w_in_r, w_uq_r, w_uk_r, w_uvt_r, gains, w_out_a, w_out_b = _layout_weights(
        w_in, w_uq, w_uk, w_uv, q_norm, k_norm, w_out)
    f1_gu, f1_d = ffn1_w_gu.astype(BF16), ffn1_w_down.astype(BF16)
    f2_gu, f2_d = ffn2_w_gu.astype(BF16), ffn2_w_down.astype(BF16)
    w_pool_b = w_pool.astype(BF16)

    outs = []
    for b in range(batch):
        xb = x[b]
        for l in range(depth):
            xb = _ffn(xb, ffn1_norm[l][None], f1_gu[l], f1_d[l], tm=t["tm"], tf=t["tf"])
            q, k, vt, zp = _pre(xb, mix_norm[l][None], w_in_r[l], q_lat_norm[l][None], kv_lat_norm[l][None],
                                w_uq_r[l], w_uk_r[l], w_uvt_r[l], gains[l], cos, sin, tm=t["tm"], tk=t["tk"])
            o = _attn(q, k, vt, tq=t["tq"], tk=t["tk"])
            xb = _post(xb, o, zp, w_pool_b[l], pool_scale[l][None], w_out_a[l], w_out_b[l], tm=t["tm"])
            xb = _ffn(xb, ffn2_norm[l][None], f2_gu[l], f2_d[l], tm=t["tm"], tf=t["tf"])
        outs.append(xb)
    return jnp.stack(outs, axis=0)
```

```python
import functools

import jax
import jax.numpy as jnp
from jax import lax
from jax.experimental import pallas as pl
from jax.experimental.pallas import tpu as pltpu

D_MODEL = 1024
MLA_HEADS = 8
QK_NOPE = 64
QK_ROPE = 32
QK_HEAD = QK_NOPE + QK_ROPE
V_HEAD = 64
Q_LORA = 384
KV_LORA = 256
MLA_WIDTH = MLA_HEADS * V_HEAD
ROPE_THETA = 10000.0
POOL_WINDOWS = (2, 4, 8, 16)
POOL_GROUP_DIM = 128
POOL_WIDTH = 512
D_FF = 2816
EPS = 1e-6
LOG2_E = 1.4426950408889634
MAX_UNSHIFTED_SCORE = 60.0
BF16_NORM_SLACK = 1.01

LANES = 128
SUBLANES = 8
HEAD_PAD = LANES
HEADS_W = MLA_HEADS * HEAD_PAD
HALF_ROPE = QK_ROPE // 2
POOL_HALO = 8
KSUB = 256
CHUNKS_PER_TRIP = 2
AHEAD = 1
RING = 2 * AHEAD
VT_ROWS = 80
C_Q, C_KV, C_POOL = 0, Q_LORA, Q_LORA + KV_LORA
C_PEA = C_POOL + POOL_WIDTH
C_PEB = C_PEA + LANES
D_IN_R = C_PEB + LANES

VMEM_LIMIT = 56 * 1024 * 1024

F32 = jnp.float32
BF16 = jnp.bfloat16


def _rmsnorm(x, g):
    ms = jnp.mean(x * x, axis=-1, keepdims=True)
    return x * lax.rsqrt(ms + EPS) * g


def _dot(a, b):
    return jnp.dot(a, b, preferred_element_type=F32)


def _const_spec(shape):
    nd = len(shape)
    return pl.BlockSpec(shape, lambda *_: (0,) * nd, pipeline_mode=pl.Buffered(1))


def _params(*sem):
    return pltpu.CompilerParams(dimension_semantics=sem, vmem_limit_bytes=VMEM_LIMIT)


def _ffn_kernel(x_ref, g_ref, wgu_ref, wd_ref, o_ref, *, tf):
    x = x_ref[...]
    h = _rmsnorm(x, g_ref[...]).astype(BF16)
    acc = jnp.zeros(x.shape, F32)
    for c in range(D_FF // tf):
        gate = _dot(h, wgu_ref[:, c * tf:(c + 1) * tf])
        up = _dot(h, wgu_ref[:, D_FF + c * tf:D_FF + (c + 1) * tf])
        act = (gate * jax.nn.sigmoid(gate) * up).astype(BF16)
        acc = acc + _dot(act, wd_ref[c * tf:(c + 1) * tf, :])
    o_ref[...] = x + 0.5 * acc


def _ffn(x, gain, w_gu, w_down, *, tm, tf):
    seq = x.shape[0]
    row = pl.BlockSpec((tm, D_MODEL), lambda i: (i, 0))
    return pl.pallas_call(
        functools.partial(_ffn_kernel, tf=tf),
        grid=(seq // tm,),
        in_specs=[row, _const_spec((1, D_MODEL)), _const_spec(w_gu.shape), _const_spec(w_down.shape)],
        out_specs=row,
        out_shape=jax.ShapeDtypeStruct(x.shape, F32),
        compiler_params=_params("parallel"),
        name="ffn",
    )(x, gain, w_gu, w_down)


def _pre_kernel(x_ref, g_ref, win_ref, qlat_ref, kvlat_ref, wuq_ref, wuk_ref, wuvt_ref, gains_ref,
                cos_ref, sin_ref, q_ref, k_ref, vt_ref, zp_ref):
    h = _rmsnorm(x_ref[...], g_ref[...]).astype(BF16)
    z = _dot(h, win_ref[...])
    cq = _rmsnorm(z[:, C_Q:C_KV], qlat_ref[...]).astype(BF16)
    ckv = _rmsnorm(z[:, C_KV:C_POOL], kvlat_ref[...]).astype(BF16)
    zp_ref[...] = z[:, C_POOL:C_PEA]
    pe = z[:, C_PEA:C_PEB]
    pe_sw = z[:, C_PEB:D_IN_R]
    qq = _dot(cq, wuq_ref[...])
    kk = _dot(ckv, wuk_ref[...])
    vt = lax.dot_general(wuvt_ref[...], ckv, (((1,), (1,)), ((), ())), preferred_element_type=F32)
    cos = cos_ref[...]
    sin = sin_ref[...]
    gq, gq_sw = gains_ref[0:1, :], gains_ref[1:2, :]
    gk, gk_sw = gains_ref[2:3, :], gains_ref[3:4, :]
    one_at_vhead = (lax.broadcasted_iota(jnp.int32, (VT_ROWS, 1), 0) == V_HEAD).astype(F32)
    scale = QK_HEAD ** -0.5 * LOG2_E
    for hd in range(MLA_HEADS):
        lo = hd * HEAD_PAD
        qh = qq[:, lo:lo + HEAD_PAD]
        qs = qq[:, HEADS_W + lo:HEADS_W + lo + HEAD_PAD]
        r = lax.rsqrt(jnp.sum(qh * qh, axis=-1, keepdims=True) * (1.0 / QK_HEAD) + EPS)
        qo = (qh * r * gq) * cos + (qs * r * gq_sw) * sin
        q_ref[:, lo:lo + HEAD_PAD] = (qo * scale).astype(BF16)
        kh = kk[:, lo:lo + HEAD_PAD] + pe
        r = lax.rsqrt(jnp.sum(kh * kh, axis=-1, keepdims=True) * (1.0 / QK_HEAD) + EPS)
        ko = (kh * r * gk) * cos + (pe_sw * r * gk_sw) * sin
        k_ref[:, lo:lo + HEAD_PAD] = ko.astype(BF16)
        vt_ref[hd] = (vt[hd * VT_ROWS:(hd + 1) * VT_ROWS, :] + one_at_vhead).astype(BF16)


def _pre(x, gain, w_in_r, q_lat, kv_lat, w_uq, w_uk, w_uvt, gains, cos, sin, *, tm, tk):
    seq = x.shape[0]
    per = tk // tm
    row = pl.BlockSpec((tm, D_MODEL), lambda i: (i, 0))
    heads = pl.BlockSpec((tm, HEADS_W), lambda i: (i, 0))
    tab = pl.BlockSpec((tm, LANES), lambda i: (i, 0))
    vt_spec = pl.BlockSpec((MLA_HEADS, None, VT_ROWS, tm), lambda i: (0, i // per, 0, i % per))
    return pl.pallas_call(
        _pre_kernel,
        grid=(seq // tm,),
        in_specs=[row, _const_spec((1, D_MODEL)), _const_spec(w_in_r.shape),
                  _const_spec((1, Q_LORA)), _const_spec((1, KV_LORA)),
                  _const_spec(w_uq.shape), _const_spec(w_uk.shape), _const_spec(w_uvt.shape),
                  _const_spec(gains.shape), tab, tab],
        out_specs=[heads, heads, vt_spec, pl.BlockSpec((tm, POOL_WIDTH), lambda i: (i, 0))],
        out_shape=[jax.ShapeDtypeStruct((seq, HEADS_W), BF16)] * 2
        + [jax.ShapeDtypeStruct((MLA_HEADS, seq // tk, VT_ROWS, tk), BF16),
           jax.ShapeDtypeStruct((seq, POOL_WIDTH), F32)],
        compiler_params=_params("parallel"),
        name="pre",
    )(x, gain, w_in_r, q_lat, kv_lat, w_uq, w_uk, w_uvt, gains, cos, sin)


def _attn_kernel(q_ref, k_ref, vt_ref, o_ref, s_ref, *, tk):
    tq = q_ref.shape[0]
    nk = k_ref.shape[0] // tk
    nsub = tk // KSUB
    qt = q_ref[...].astype(F32).T.astype(BF16)

    def scores_sub(j, c, slot, cmax):
        kc = k_ref[pl.ds(pl.multiple_of(j * tk + c * KSUB, KSUB), KSUB), :]
        st = _dot(kc, qt)
        s_ref[slot, c * KSUB:(c + 1) * KSUB, :] = st
        return jnp.maximum(cmax, jnp.max(st.reshape(KSUB // SUBLANES, SUBLANES, tq), axis=0))

    def step(j, slot, cmax, m, acc, j_next):
        m_new = jnp.maximum(m, jnp.max(cmax, axis=0, keepdims=True))
        acc = jnp.exp2(m - m_new) * acc
        cmax_next = jnp.full((SUBLANES, tq), -jnp.inf, F32)
        for c in range(nsub):
            cmax_next = scores_sub(j_next, c, 1 - slot, cmax_next)
            pt = jnp.exp2(s_ref[slot, c * KSUB:(c + 1) * KSUB, :] - m_new).astype(BF16)
            acc = acc + _dot(vt_ref[j, :, c * KSUB:(c + 1) * KSUB], pt)
        return cmax_next, m_new, acc

    def pair(jj, carry):
        cmax, m, acc = carry
        j = 2 * jj
        cmax, m, acc = step(j, 0, cmax, m, acc, j + 1)
        return step(j + 1, 1, cmax, m, acc, jnp.minimum(j + 2, nk - 1))

    cmax0 = jnp.full((SUBLANES, tq), -jnp.inf, F32)
    for c in range(nsub):
        cmax0 = scores_sub(0, c, 0, cmax0)
    m0 = jnp.full((1, tq), -jnp.inf, F32)
    acc0 = jnp.zeros((VT_ROWS, tq), F32)
    _, _, acc = lax.fori_loop(0, nk // 2, pair, (cmax0, m0, acc0))
    out_t = acc / acc[V_HEAD:V_HEAD + 1, :]
    out_t = jnp.concatenate([out_t, jnp.zeros((HEAD_PAD - VT_ROWS, tq), F32)], axis=0)
    o_ref[...] = out_t.T.astype(BF16)


def _attn_unshifted_kernel(q_ref, k_ref, vt_ref, o_ref, s_ref, *, tk):
    tq = q_ref.shape[0]
    nk = k_ref.shape[0] // tk
    nsub = tk // KSUB
    last = k_ref.shape[0] // KSUB - 1
    qt = q_ref[...].astype(F32).T.astype(BF16)

    def scores_sub(g, slot):
        kc = k_ref[pl.ds(pl.multiple_of(g * KSUB, KSUB), KSUB), :]
        s_ref[slot] = _dot(kc, qt)

    per = min(CHUNKS_PER_TRIP, nk)

    def trip(jj, acc):
        for u in range(per * nsub):
            j, c = jj * per + u // nsub, u % nsub
            g = j * nsub + c
            scores_sub(jnp.minimum(g + AHEAD, last), (u + AHEAD) % RING)
            pt = jnp.exp2(s_ref[u % RING]).astype(BF16)
            acc = acc + _dot(vt_ref[j, :, c * KSUB:(c + 1) * KSUB], pt)
        return acc

    for g in range(AHEAD):
        scores_sub(g, g % RING)
    acc = lax.fori_loop(0, nk // per, trip, jnp.zeros((VT_ROWS, tq), F32))
    out_t = acc / acc[V_HEAD:V_HEAD + 1, :]
    out_t = jnp.concatenate([out_t, jnp.zeros((HEAD_PAD - VT_ROWS, tq), F32)], axis=0)
    o_ref[...] = out_t.T.astype(BF16)


def _attn(q, k, vt, *, tq, tk, shifted):
    seq = q.shape[0]
    qspec = pl.BlockSpec((tq, HEAD_PAD), lambda h, i: (i, h))
    kspec = pl.BlockSpec((seq, HEAD_PAD), lambda h, i: (0, h))
    vspec = pl.BlockSpec((None, seq // tk, VT_ROWS, tk), lambda h, i: (h, 0, 0, 0))
    body = _attn_kernel if shifted else _attn_unshifted_kernel
    return pl.pallas_call(
        functools.partial(body, tk=tk),
        grid=(MLA_HEADS, seq // tq),
        in_specs=[qspec, kspec, vspec],
        out_specs=qspec,
        out_shape=jax.ShapeDtypeStruct((seq, HEADS_W), BF16),
        scratch_shapes=[pltpu.VMEM((2, tk, tq) if shifted else (RING, KSUB, tq), F32)],
        compiler_params=_params("parallel", "parallel"),
        name="attn" if shifted else "attn_unshifted",
    )(q, k, vt)


def _score_bound(q_gain, k_gain):
    return (LOG2_E * QK_HEAD ** 0.5) * jnp.max(jnp.abs(q_gain)) * jnp.max(jnp.abs(k_gain)) * BF16_NORM_SLACK


def _post_kernel(x_ref, o_ref, zc_ref, zprev_ref, znext_ref, wpool_ref, ps_ref, woa_ref, wob_ref,
                 out_ref, ext_ref, *, tm, seq):
    i = pl.program_id(0)
    last = pl.num_programs(0) - 1
    zc = zc_ref[...]
    ext_ref[0:POOL_HALO, :] = jnp.where(i > 0, zprev_ref[...], 0.0)
    ext_ref[POOL_HALO:POOL_HALO + tm, :] = zc
    ext_ref[POOL_HALO + tm:2 * POOL_HALO + tm, :] = jnp.where(i < last, znext_ref[...], 0.0)
    pos = i * tm + lax.broadcasted_iota(jnp.int32, (tm, 1), 0)
    parts = []
    for g, w in enumerate(POOL_WINDOWS):
        left = w // 2
        right = w - 1 - left
        c0 = g * POOL_GROUP_DIM
        wsum = ext_ref[POOL_HALO - left:POOL_HALO - left + tm, c0:c0 + POOL_GROUP_DIM]
        for d in range(-left + 1, right + 1):
            wsum = wsum + ext_ref[POOL_HALO + d:POOL_HALO + d + tm, c0:c0 + POOL_GROUP_DIM]
        cnt = (jnp.minimum(pos + right + 1, seq) - jnp.maximum(pos - left, 0)).astype(F32)
        mixed = (wsum / cnt - zc[:, c0:c0 + POOL_GROUP_DIM]).astype(BF16)
        y = _dot(mixed, wpool_ref[g]) * ps_ref[:, c0:c0 + POOL_GROUP_DIM]
        parts.append(y.astype(BF16))
    b = jnp.concatenate(parts, axis=1)
    out_ref[...] = x_ref[...] + (_dot(o_ref[...], woa_ref[...]) + _dot(b, wob_ref[...]))


def _post(x, o, zp, w_pool, pool_scale, w_out_a, w_out_b, *, tm):
    seq = x.shape[0]
    per = tm // POOL_HALO
    nblk = seq // POOL_HALO
    row = pl.BlockSpec((tm, D_MODEL), lambda i: (i, 0))
    return pl.pallas_call(
        functools.partial(_post_kernel, tm=tm, seq=seq),
        grid=(seq // tm,),
        in_specs=[row,
                  pl.BlockSpec((tm, HEADS_W), lambda i: (i, 0)),
                  pl.BlockSpec((tm, POOL_WIDTH), lambda i: (i, 0)),
                  pl.BlockSpec((POOL_HALO, POOL_WIDTH), lambda i: (jnp.maximum(i * per - 1, 0), 0)),
                  pl.BlockSpec((POOL_HALO, POOL_WIDTH), lambda i: (jnp.minimum((i + 1) * per, nblk - 1), 0)),
                  _const_spec(w_pool.shape), _const_spec((1, POOL_WIDTH)),
                  _const_spec(w_out_a.shape), _const_spec(w_out_b.shape)],
        out_specs=row,
        out_shape=jax.ShapeDtypeStruct(x.shape, F32),
        scratch_shapes=[pltpu.VMEM((tm + 2 * POOL_HALO, POOL_WIDTH), F32)],
        compiler_params=_params("parallel"),
        name="post",
    )(x, o, zp, zp, zp, w_pool, pool_scale, w_out_a, w_out_b)


def _swap_halves(t):
    return jnp.concatenate([t[..., HALF_ROPE:], t[..., :HALF_ROPE]], axis=-1)


def _head_block(nope, rope):
    pad = jnp.zeros(nope.shape[:-1] + (HEAD_PAD - QK_HEAD,), nope.dtype)
    return jnp.concatenate([nope, rope, pad], axis=-1)


def _layout_weights(w_in, w_uq, w_uk, w_uv, q_norm, k_norm, w_out):
    depth = w_in.shape[0]
    o_pe = Q_LORA + KV_LORA
    o_pool = o_pe + QK_ROPE
    w_pe = w_in[..., o_pe:o_pool]
    zero_nope = jnp.zeros(w_pe.shape[:-1] + (QK_NOPE,), w_pe.dtype)
    w_in_r = jnp.concatenate(
        [w_in[..., :o_pe], w_in[..., o_pool:], _head_block(zero_nope, w_pe),
         _head_block(zero_nope, _swap_halves(w_pe))], axis=-1).astype(BF16)

    uq = w_uq.reshape(depth, Q_LORA, MLA_HEADS, QK_HEAD)
    uq_main = _head_block(uq[..., :QK_NOPE], uq[..., QK_NOPE:])
    uq_swap = _head_block(jnp.zeros_like(uq[..., :QK_NOPE]), _swap_halves(uq[..., QK_NOPE:]))
    w_uq_r = jnp.concatenate([uq_main.reshape(depth, Q_LORA, HEADS_W),
                              uq_swap.reshape(depth, Q_LORA, HEADS_W)], axis=-1).astype(BF16)

    uk = jnp.pad(w_uk.reshape(depth, KV_LORA, MLA_HEADS, QK_NOPE),
                 ((0, 0), (0, 0), (0, 0), (0, HEAD_PAD - QK_NOPE)))
    w_uk_r = uk.reshape(depth, KV_LORA, HEADS_W).astype(BF16)
    uvt = jnp.transpose(w_uv.reshape(depth, KV_LORA, MLA_HEADS, V_HEAD), (0, 2, 3, 1))
    uvt = jnp.pad(uvt, ((0, 0), (0, 0), (0, VT_ROWS - V_HEAD), (0, 0)))
    w_uvt_r = uvt.reshape(depth, MLA_HEADS * VT_ROWS, KV_LORA).astype(BF16)

    def gain_rows(g):
        main = _head_block(g[..., :QK_NOPE], g[..., QK_NOPE:])
        swap = _head_block(jnp.zeros_like(g[..., :QK_NOPE]), _swap_halves(g[..., QK_NOPE:]))
        return main, swap

    gq, gq_sw = gain_rows(q_norm)
    gk, gk_sw = gain_rows(k_norm)
    zero = jnp.zeros_like(gq)
    gains = jnp.stack([gq, gq_sw, gk, gk_sw, zero, zero, zero, zero], axis=1)

    oa = w_out[:, :MLA_WIDTH].reshape(depth, MLA_HEADS, V_HEAD, D_MODEL)
    oa = jnp.pad(oa, ((0, 0), (0, 0), (0, HEAD_PAD - V_HEAD), (0, 0))).reshape(depth, HEADS_W, D_MODEL)
    return w_in_r, w_uq_r, w_uk_r, w_uvt_r, gains, oa.astype(BF16), w_out[:, MLA_WIDTH:].astype(BF16)


def _rope_tables(seq):
    pos = jnp.arange(seq, dtype=F32)
    inv = ROPE_THETA ** (-jnp.arange(0, QK_ROPE, 2, dtype=F32) / QK_ROPE)
    ang = pos[:, None] * inv[None, :]
    cos, sin = jnp.cos(ang), jnp.sin(ang)
    ones = jnp.ones((seq, QK_NOPE), F32)
    zeros = jnp.zeros((seq, QK_NOPE), F32)
    return (_head_block(ones, jnp.concatenate([cos, cos], axis=-1)),
            _head_block(zeros, jnp.concatenate([-sin, sin], axis=-1)))


def _tiles(seq):
    tm = min(512, seq)
    return dict(tm=tm, tq=min(1024, seq), tk=min(2048, seq // 2), tf=D_FF // 2)


def kernel(x, ffn1_norm, ffn1_w_gu, ffn1_w_down, mix_norm, w_in, q_lat_norm, kv_lat_norm, w_uq, w_uk, w_uv,
           q_norm, k_norm, w_pool, pool_scale, w_out, ffn2_norm, ffn2_w_gu, ffn2_w_down):
    batch, seq, _ = x.shape
    depth = w_in.shape[0]
    t = _tiles(seq)
    cos, sin = _rope_tables(seq)
    w_in_r, w_uq_r, w_uk_r, w_uvt_r, gains, w_out_a, w_out_b = _layout_weights(
        w_in, w_uq, w_uk, w_uv, q_norm, k_norm, w_out)
    f1_gu, f1_d = ffn1_w_gu.astype(BF16), ffn1_w_down.astype(BF16)
    f2_gu, f2_d = ffn2_w_gu.astype(BF16), ffn2_w_down.astype(BF16)
    w_pool_b = w_pool.astype(BF16)

    outs = []
    for b in range(batch):
        xb = x[b]
        for l in range(depth):
            xb = _ffn(xb, ffn1_norm[l][None], f1_gu[l], f1_d[l], tm=t["tm"], tf=t["tf"])
            q, k, vt, zp = _pre(xb, mix_norm[l][None], w_in_r[l], q_lat_norm[l][None], kv_lat_norm[l][None],
                                w_uq_r[l], w_uk_r[l], w_uvt_r[l], gains[l], cos, sin, tm=t["tm"], tk=t["tk"])
            o = lax.cond(_score_bound(q_norm[l], k_norm[l]) <= MAX_UNSHIFTED_SCORE,
                         functools.partial(_attn, tq=t["tq"], tk=t["tk"], shifted=False),
                         functools.partial(_attn, tq=t["tq"], tk=t["tk"], shifted=True),
                         q, k, vt)
            xb = _post(xb, o, zp, w_pool_b[l], pool_scale[l][None], w_out_a[l], w_out_b[l], tm=t["tm"])
            xb = _ffn(xb, ffn2_norm[l][None], f2_gu[l], f2_d[l], tm=t["tm"], tf=t["tf"])
        outs.append(xb)
    return jnp.stack(outs, axis=0)
```

```python
import functools

import jax
import jax.numpy as jnp
from jax import lax
from jax.experimental import pallas as pl
from jax.experimental.pallas import tpu as pltpu

D_MODEL = 1024
MLA_HEADS = 8
QK_NOPE = 64
QK_ROPE = 32
QK_HEAD = QK_NOPE + QK_ROPE
V_HEAD = 64
Q_LORA = 384
KV_LORA = 256
MLA_WIDTH = MLA_HEADS * V_HEAD
ROPE_THETA = 10000.0
POOL_WINDOWS = (2, 4, 8, 16)
POOL_GROUP_DIM = 128
POOL_WIDTH = 512
D_FF = 2816
EPS = 1e-6
LOG2_E = 1.4426950408889634
MAX_UNSHIFTED_SCORE = 60.0
BF16_NORM_SLACK = 1.01

LANES = 128
SUBLANES = 8
HEAD_PAD = LANES
HEADS_W = MLA_HEADS * HEAD_PAD
HALF_ROPE = QK_ROPE // 2
POOL_HALO = 8
KSUB = 256
CHUNKS_PER_TRIP = 8
QSPLIT = 2
AHEAD = 1
RING = 2 * AHEAD
VT_ROWS = 80
C_Q, C_KV, C_POOL = 0, Q_LORA, Q_LORA + KV_LORA
C_PEA = C_POOL + POOL_WIDTH
C_PEB = C_PEA + LANES
D_IN_R = C_PEB + LANES

VMEM_LIMIT = 56 * 1024 * 1024

F32 = jnp.float32
BF16 = jnp.bfloat16


def _rmsnorm(x, g):
    ms = jnp.mean(x * x, axis=-1, keepdims=True)
    return x * lax.rsqrt(ms + EPS) * g


def _dot(a, b):
    return jnp.dot(a, b, preferred_element_type=F32)


def _const_spec(shape):
    nd = len(shape)
    return pl.BlockSpec(shape, lambda *_: (0,) * nd, pipeline_mode=pl.Buffered(1))


def _params(*sem):
    return pltpu.CompilerParams(dimension_semantics=sem, vmem_limit_bytes=VMEM_LIMIT)


def _ffn_kernel(x_ref, g_ref, wgu_ref, wd_ref, o_ref, *, tf):
    x = x_ref[...]
    h = _rmsnorm(x, g_ref[...]).astype(BF16)
    acc = jnp.zeros(x.shape, F32)
    for c in range(D_FF // tf):
        gate = _dot(h, wgu_ref[:, c * tf:(c + 1) * tf])
        up = _dot(h, wgu_ref[:, D_FF + c * tf:D_FF + (c + 1) * tf])
        act = (gate * jax.nn.sigmoid(gate) * up).astype(BF16)
        acc = acc + _dot(act, wd_ref[c * tf:(c + 1) * tf, :])
    o_ref[...] = x + 0.5 * acc


def _ffn(x, gain, w_gu, w_down, *, tm, tf):
    seq = x.shape[0]
    row = pl.BlockSpec((tm, D_MODEL), lambda i: (i, 0))
    return pl.pallas_call(
        functools.partial(_ffn_kernel, tf=tf),
        grid=(seq // tm,),
        in_specs=[row, _const_spec((1, D_MODEL)), _const_spec(w_gu.shape), _const_spec(w_down.shape)],
        out_specs=row,
        out_shape=jax.ShapeDtypeStruct(x.shape, F32),
        compiler_params=_params("parallel"),
        name="ffn",
    )(x, gain, w_gu, w_down)


def _pre_kernel(x_ref, g_ref, win_ref, qlat_ref, kvlat_ref, wuq_ref, wuk_ref, wuvt_ref, gains_ref,
                cos_ref, sin_ref, q_ref, k_ref, vt_ref, zp_ref):
    h = _rmsnorm(x_ref[...], g_ref[...]).astype(BF16)
    z = _dot(h, win_ref[...])
    cq = _rmsnorm(z[:, C_Q:C_KV], qlat_ref[...]).astype(BF16)
    ckv = _rmsnorm(z[:, C_KV:C_POOL], kvlat_ref[...]).astype(BF16)
    zp_ref[...] = z[:, C_POOL:C_PEA]
    pe = z[:, C_PEA:C_PEB]
    pe_sw = z[:, C_PEB:D_IN_R]
    qq = _dot(cq, wuq_ref[...])
    kk = _dot(ckv, wuk_ref[...])
    vt = lax.dot_general(wuvt_ref[...], ckv, (((1,), (1,)), ((), ())), preferred_element_type=F32)
    cos = cos_ref[...]
    sin = sin_ref[...]
    gq, gq_sw = gains_ref[0:1, :], gains_ref[1:2, :]
    gk, gk_sw = gains_ref[2:3, :], gains_ref[3:4, :]
    one_at_vhead = (lax.broadcasted_iota(jnp.int32, (VT_ROWS, 1), 0) == V_HEAD).astype(F32)
    scale = QK_HEAD ** -0.5 * LOG2_E
    for hd in range(MLA_HEADS):
        lo = hd * HEAD_PAD
        qh = qq[:, lo:lo + HEAD_PAD]
        qs = qq[:, HEADS_W + lo:HEADS_W + lo + HEAD_PAD]
        r = lax.rsqrt(jnp.sum(qh * qh, axis=-1, keepdims=True) * (1.0 / QK_HEAD) + EPS)
        qo = (qh * r * gq) * cos + (qs * r * gq_sw) * sin
        q_ref[:, lo:lo + HEAD_PAD] = (qo * scale).astype(BF16)
        kh = kk[:, lo:lo + HEAD_PAD] + pe
        r = lax.rsqrt(jnp.sum(kh * kh, axis=-1, keepdims=True) * (1.0 / QK_HEAD) + EPS)
        ko = (kh * r * gk) * cos + (pe_sw * r * gk_sw) * sin
        k_ref[:, lo:lo + HEAD_PAD] = ko.astype(BF16)
        vt_ref[hd] = (vt[hd * VT_ROWS:(hd + 1) * VT_ROWS, :] + one_at_vhead).astype(BF16)


def _pre(x, gain, w_in_r, q_lat, kv_lat, w_uq, w_uk, w_uvt, gains, cos, sin, *, tm, tk):
    seq = x.shape[0]
    per = tk // tm
    row = pl.BlockSpec((tm, D_MODEL), lambda i: (i, 0))
    heads = pl.BlockSpec((tm, HEADS_W), lambda i: (i, 0))
    tab = pl.BlockSpec((tm, LANES), lambda i: (i, 0))
    vt_spec = pl.BlockSpec((MLA_HEADS, None, VT_ROWS, tm), lambda i: (0, i // per, 0, i % per))
    return pl.pallas_call(
        _pre_kernel,
        grid=(seq // tm,),
        in_specs=[row, _const_spec((1, D_MODEL)), _const_spec(w_in_r.shape),
                  _const_spec((1, Q_LORA)), _const_spec((1, KV_LORA)),
                  _const_spec(w_uq.shape), _const_spec(w_uk.shape), _const_spec(w_uvt.shape),
                  _const_spec(gains.shape), tab, tab],
        out_specs=[heads, heads, vt_spec, pl.BlockSpec((tm, POOL_WIDTH), lambda i: (i, 0))],
        out_shape=[jax.ShapeDtypeStruct((seq, HEADS_W), BF16)] * 2
        + [jax.ShapeDtypeStruct((MLA_HEADS, seq // tk, VT_ROWS, tk), BF16),
           jax.ShapeDtypeStruct((seq, POOL_WIDTH), F32)],
        compiler_params=_params("parallel"),
        name="pre",
    )(x, gain, w_in_r, q_lat, kv_lat, w_uq, w_uk, w_uvt, gains, cos, sin)


def _attn_kernel(q_ref, k_ref, vt_ref, o_ref, s_ref, *, tk):
    tq = q_ref.shape[0]
    nk = k_ref.shape[0] // tk
    nsub = tk // KSUB
    qt = q_ref[...].astype(F32).T.astype(BF16)

    def scores_sub(j, c, slot, cmax):
        kc = k_ref[pl.ds(pl.multiple_of(j * tk + c * KSUB, KSUB), KSUB), :]
        st = _dot(kc, qt)
        s_ref[slot, c * KSUB:(c + 1) * KSUB, :] = st
        return jnp.maximum(cmax, jnp.max(st.reshape(KSUB // SUBLANES, SUBLANES, tq), axis=0))

    def step(j, slot, cmax, m, acc, j_next):
        m_new = jnp.maximum(m, jnp.max(cmax, axis=0, keepdims=True))
        acc = jnp.exp2(m - m_new) * acc
        cmax_next = jnp.full((SUBLANES, tq), -jnp.inf, F32)
        for c in range(nsub):
            cmax_next = scores_sub(j_next, c, 1 - slot, cmax_next)
            pt = jnp.exp2(s_ref[slot, c * KSUB:(c + 1) * KSUB, :] - m_new).astype(BF16)
            acc = acc + _dot(vt_ref[j, :, c * KSUB:(c + 1) * KSUB], pt)
        return cmax_next, m_new, acc

    def pair(jj, carry):
        cmax, m, acc = carry
        j = 2 * jj
        cmax, m, acc = step(j, 0, cmax, m, acc, j + 1)
        return step(j + 1, 1, cmax, m, acc, jnp.minimum(j + 2, nk - 1))

    cmax0 = jnp.full((SUBLANES, tq), -jnp.inf, F32)
    for c in range(nsub):
        cmax0 = scores_sub(0, c, 0, cmax0)
    m0 = jnp.full((1, tq), -jnp.inf, F32)
    acc0 = jnp.zeros((VT_ROWS, tq), F32)
    _, _, acc = lax.fori_loop(0, nk // 2, pair, (cmax0, m0, acc0))
    out_t = acc / acc[V_HEAD:V_HEAD + 1, :]
    out_t = jnp.concatenate([out_t, jnp.zeros((HEAD_PAD - VT_ROWS, tq), F32)], axis=0)
    o_ref[...] = out_t.T.astype(BF16)


def _attn_unshifted_kernel(q_ref, k_ref, vt_ref, o_ref, s_ref, *, tk):
    tq = q_ref.shape[0]
    nk = k_ref.shape[0] // tk
    nsub = tk // KSUB
    last = k_ref.shape[0] // KSUB - 1
    qt = q_ref[...].astype(F32).T.astype(BF16)

    nq = tq // QSPLIT

    def scores_sub(g, slot, h):
        kc = k_ref[pl.ds(pl.multiple_of(g * KSUB, KSUB), KSUB), :]
        s_ref[slot, :, h * nq:(h + 1) * nq] = _dot(kc, qt[:, h * nq:(h + 1) * nq])

    per = min(CHUNKS_PER_TRIP, nk)

    def trip(jj, accs):
        accs = list(accs)
        for u in range(per * nsub):
            j, c = jj * per + u // nsub, u % nsub
            g = j * nsub + c
            for h in range(QSPLIT):
                scores_sub(jnp.minimum(g + AHEAD, last), (u + AHEAD) % RING, h)
                pt = jnp.exp2(s_ref[u % RING, :, h * nq:(h + 1) * nq]).astype(BF16)
                accs[h] = accs[h] + _dot(vt_ref[j, :, c * KSUB:(c + 1) * KSUB], pt)
        return tuple(accs)

    for g in range(AHEAD):
        for h in range(QSPLIT):
            scores_sub(g, g % RING, h)
    accs = lax.fori_loop(0, nk // per, trip, tuple(jnp.zeros((VT_ROWS, nq), F32) for _ in range(QSPLIT)))
    acc = jnp.concatenate(accs, axis=1)
    out_t = acc / acc[V_HEAD:V_HEAD + 1, :]
    out_t = jnp.concatenate([out_t, jnp.zeros((HEAD_PAD - VT_ROWS, tq), F32)], axis=0)
    o_ref[...] = out_t.T.astype(BF16)


def _attn(q, k, vt, *, tq, tk, shifted):
    seq = q.shape[0]
    qspec = pl.BlockSpec((tq, HEAD_PAD), lambda h, i: (i, h))
    kspec = pl.BlockSpec((seq, HEAD_PAD), lambda h, i: (0, h))
    vspec = pl.BlockSpec((None, seq // tk, VT_ROWS, tk), lambda h, i: (h, 0, 0, 0))
    body = _attn_kernel if shifted else _attn_unshifted_kernel
    return pl.pallas_call(
        functools.partial(body, tk=tk),
        grid=(MLA_HEADS, seq // tq),
        in_specs=[qspec, kspec, vspec],
        out_specs=qspec,
        out_shape=jax.ShapeDtypeStruct((seq, HEADS_W), BF16),
        scratch_shapes=[pltpu.VMEM((2, tk, tq) if shifted else (RING, KSUB, tq), F32)],
        compiler_params=_params("parallel", "parallel"),
        name="attn" if shifted else "attn_unshifted",
    )(q, k, vt)


def _score_bound(q_gain, k_gain):
    return (LOG2_E * QK_HEAD ** 0.5) * jnp.max(jnp.abs(q_gain)) * jnp.max(jnp.abs(k_gain)) * BF16_NORM_SLACK


def _post_kernel(x_ref, o_ref, zc_ref, zprev_ref, znext_ref, wpool_ref, ps_ref, woa_ref, wob_ref,
                 out_ref, ext_ref, *, tm, seq):
    i = pl.program_id(0)
    last = pl.num_programs(0) - 1
    zc = zc_ref[...]
    ext_ref[0:POOL_HALO, :] = jnp.where(i > 0, zprev_ref[...], 0.0)
    ext_ref[POOL_HALO:POOL_HALO + tm, :] = zc
    ext_ref[POOL_HALO + tm:2 * POOL_HALO + tm, :] = jnp.where(i < last, znext_ref[...], 0.0)
    pos = i * tm + lax.broadcasted_iota(jnp.int32, (tm, 1), 0)
    parts = []
    for g, w in enumerate(POOL_WINDOWS):
        left = w // 2
        right = w - 1 - left
        c0 = g * POOL_GROUP_DIM
        wsum = ext_ref[POOL_HALO - left:POOL_HALO - left + tm, c0:c0 + POOL_GROUP_DIM]
        for d in range(-left + 1, right + 1):
            wsum = wsum + ext_ref[POOL_HALO + d:POOL_HALO + d + tm, c0:c0 + POOL_GROUP_DIM]
        cnt = (jnp.minimum(pos + right + 1, seq) - jnp.maximum(pos - left, 0)).astype(F32)
        mixed = (wsum / cnt - zc[:, c0:c0 + POOL_GROUP_DIM]).astype(BF16)
        y = _dot(mixed, wpool_ref[g]) * ps_ref[:, c0:c0 + POOL_GROUP_DIM]
        parts.append(y.astype(BF16))
    b = jnp.concatenate(parts, axis=1)
    out_ref[...] = x_ref[...] + (_dot(o_ref[...], woa_ref[...]) + _dot(b, wob_ref[...]))


def _post(x, o, zp, w_pool, pool_scale, w_out_a, w_out_b, *, tm):
    seq = x.shape[0]
    per = tm // POOL_HALO
    nblk = seq // POOL_HALO
    row = pl.BlockSpec((tm, D_MODEL), lambda i: (i, 0))
    return pl.pallas_call(
        functools.partial(_post_kernel, tm=tm, seq=seq),
        grid=(seq // tm,),
        in_specs=[row,
                  pl.BlockSpec((tm, HEADS_W), lambda i: (i, 0)),
                  pl.BlockSpec((tm, POOL_WIDTH), lambda i: (i, 0)),
                  pl.BlockSpec((POOL_HALO, POOL_WIDTH), lambda i: (jnp.maximum(i * per - 1, 0), 0)),
                  pl.BlockSpec((POOL_HALO, POOL_WIDTH), lambda i: (jnp.minimum((i + 1) * per, nblk - 1), 0)),
                  _const_spec(w_pool.shape), _const_spec((1, POOL_WIDTH)),
                  _const_spec(w_out_a.shape), _const_spec(w_out_b.shape)],
        out_specs=row,
        out_shape=jax.ShapeDtypeStruct(x.shape, F32),
        scratch_shapes=[pltpu.VMEM((tm + 2 * POOL_HALO, POOL_WIDTH), F32)],
        compiler_params=_params("parallel"),
        name="post",
    )(x, o, zp, zp, zp, w_pool, pool_scale, w_out_a, w_out_b)


def _swap_halves(t):
    return jnp.concatenate([t[..., HALF_ROPE:], t[..., :HALF_ROPE]], axis=-1)


def _head_block(nope, rope):
    pad = jnp.zeros(nope.shape[:-1] + (HEAD_PAD - QK_HEAD,), nope.dtype)
    return jnp.concatenate([nope, rope, pad], axis=-1)


def _layout_weights(w_in, w_uq, w_uk, w_uv, q_norm, k_norm, w_out):
    depth = w_in.shape[0]
    o_pe = Q_LORA + KV_LORA
    o_pool = o_pe + QK_ROPE
    w_pe = w_in[..., o_pe:o_pool]
    zero_nope = jnp.zeros(w_pe.shape[:-1] + (QK_NOPE,), w_pe.dtype)
    w_in_r = jnp.concatenate(
        [w_in[..., :o_pe], w_in[..., o_pool:], _head_block(zero_nope, w_pe),
         _head_block(zero_nope, _swap_halves(w_pe))], axis=-1).astype(BF16)

    uq = w_uq.reshape(depth, Q_LORA, MLA_HEADS, QK_HEAD)
    uq_main = _head_block(uq[..., :QK_NOPE], uq[..., QK_NOPE:])
    uq_swap = _head_block(jnp.zeros_like(uq[..., :QK_NOPE]), _swap_halves(uq[..., QK_NOPE:]))
    w_uq_r = jnp.concatenate([uq_main.reshape(depth, Q_LORA, HEADS_W),
                              uq_swap.reshape(depth, Q_LORA, HEADS_W)], axis=-1).astype(BF16)

    uk = jnp.pad(w_uk.reshape(depth, KV_LORA, MLA_HEADS, QK_NOPE),
                 ((0, 0), (0, 0), (0, 0), (0, HEAD_PAD - QK_NOPE)))
    w_uk_r = uk.reshape(depth, KV_LORA, HEADS_W).astype(BF16)
    uvt = jnp.transpose(w_uv.reshape(depth, KV_LORA, MLA_HEADS, V_HEAD), (0, 2, 3, 1))
    uvt = jnp.pad(uvt, ((0, 0), (0, 0), (0, VT_ROWS - V_HEAD), (0, 0)))
    w_uvt_r = uvt.reshape(depth, MLA_HEADS * VT_ROWS, KV_LORA).astype(BF16)

    def gain_rows(g):
        main = _head_block(g[..., :QK_NOPE], g[..., QK_NOPE:])
        swap = _head_block(jnp.zeros_like(g[..., :QK_NOPE]), _swap_halves(g[..., QK_NOPE:]))
        return main, swap

    gq, gq_sw = gain_rows(q_norm)
    gk, gk_sw = gain_rows(k_norm)
    zero = jnp.zeros_like(gq)
    gains = jnp.stack([gq, gq_sw, gk, gk_sw, zero, zero, zero, zero], axis=1)

    oa = w_out[:, :MLA_WIDTH].reshape(depth, MLA_HEADS, V_HEAD, D_MODEL)
    oa = jnp.pad(oa, ((0, 0), (0, 0), (0, HEAD_PAD - V_HEAD), (0, 0))).reshape(depth, HEADS_W, D_MODEL)
    return w_in_r, w_uq_r, w_uk_r, w_uvt_r, gains, oa.astype(BF16), w_out[:, MLA_WIDTH:].astype(BF16)


def _rope_tables(seq):
    pos = jnp.arange(seq, dtype=F32)
    inv = ROPE_THETA ** (-jnp.arange(0, QK_ROPE, 2, dtype=F32) / QK_ROPE)
    ang = pos[:, None] * inv[None, :]
    cos, sin = jnp.cos(ang), jnp.sin(ang)
    ones = jnp.ones((seq, QK_NOPE), F32)
    zeros = jnp.zeros((seq, QK_NOPE), F32)
    return (_head_block(ones, jnp.concatenate([cos, cos], axis=-1)),
            _head_block(zeros, jnp.concatenate([-sin, sin], axis=-1)))


def _tiles(seq):
    tm = min(512, seq)
    return dict(tm=tm, tq=min(1024, seq), tk=min(2048, seq // 2), tf=D_FF // 2)


def kernel(x, ffn1_norm, ffn1_w_gu, ffn1_w_down, mix_norm, w_in, q_lat_norm, kv_lat_norm, w_uq, w_uk, w_uv,
           q_norm, k_norm, w_pool, pool_scale, w_out, ffn2_norm, ffn2_w_gu, ffn2_w_down):
    batch, seq, _ = x.shape
    depth = w_in.shape[0]
    t = _tiles(seq)
    cos, sin = _rope_tables(seq)
    w_in_r, w_uq_r, w_uk_r, w_uvt_r, gains, w_out_a, w_out_b = _layout_weights(
        w_in, w_uq, w_uk, w_uv, q_norm, k_norm, w_out)
    f1_gu, f1_d = ffn1_w_gu.astype(BF16), ffn1_w_down.astype(BF16)
    f2_gu, f2_d = ffn2_w_gu.astype(BF16), ffn2_w_down.astype(BF16)
    w_pool_b = w_pool.astype(BF16)

    outs = []
    for b in range(batch):
        xb = x[b]
        for l in range(depth):
            xb = _ffn(xb, ffn1_norm[l][None], f1_gu[l], f1_d[l], tm=t["tm"], tf=t["tf"])
            q, k, vt, zp = _pre(xb, mix_norm[l][None], w_in_r[l], q_lat_norm[l][None], kv_lat_norm[l][None],
                                w_uq_r[l], w_uk_r[l], w_uvt_r[l], gains[l], cos, sin, tm=t["tm"], tk=t["tk"])
            o = lax.cond(_score_bound(q_norm[l], k_norm[l]) <= MAX_UNSHIFTED_SCORE,
                         functools.partial(_attn, tq=t["tq"], tk=t["tk"], shifted=False),
                         functools.partial(_attn, tq=t["tq"], tk=t["tk"], shifted=True),
                         q, k, vt)
            xb = _post(xb, o, zp, w_pool_b[l], pool_scale[l][None], w_out_a[l], w_out_b[l], tm=t["tm"])
            xb = _ffn(xb, ffn2_norm[l][None], f2_gu[l], f2_d[l], tm=t["tm"], tf=t["tf"])
        outs.append(xb)
    return jnp.stack(outs, axis=0)
```

```python
import functools

import jax
import jax.numpy as jnp
from jax import lax
from jax.experimental import pallas as pl
from jax.experimental.pallas import tpu as pltpu

D_MODEL = 1024
MLA_HEADS = 8
QK_NOPE = 64
QK_ROPE = 32
QK_HEAD = QK_NOPE + QK_ROPE
V_HEAD = 64
Q_LORA = 384
KV_LORA = 256
MLA_WIDTH = MLA_HEADS * V_HEAD
ROPE_THETA = 10000.0
POOL_WINDOWS = (2, 4, 8, 16)
POOL_GROUP_DIM = 128
POOL_WIDTH = 512
D_FF = 2816
EPS = 1e-6
LOG2_E = 1.4426950408889634
MAX_UNSHIFTED_SCORE = 60.0
BF16_NORM_SLACK = 1.01

LANES = 128
SUBLANES = 8
HEAD_PAD = LANES
HEADS_W = MLA_HEADS * HEAD_PAD
HALF_ROPE = QK_ROPE // 2
POOL_HALO = 8
KSUB = 256
CHUNKS_PER_TRIP = 8
QSPLIT = 2
AHEAD = 1
RING = 2 * AHEAD
C_Q, C_KV, C_POOL = 0, Q_LORA, Q_LORA + KV_LORA
C_PEA = C_POOL + POOL_WIDTH
C_PEB = C_PEA + LANES
D_IN_R = C_PEB + LANES

VMEM_LIMIT = 56 * 1024 * 1024

F32 = jnp.float32
BF16 = jnp.bfloat16


def _rmsnorm(x, g):
    ms = jnp.mean(x * x, axis=-1, keepdims=True)
    return x * lax.rsqrt(ms + EPS) * g


def _dot(a, b):
    return jnp.dot(a, b, preferred_element_type=F32)


def _const_spec(shape):
    nd = len(shape)
    return pl.BlockSpec(shape, lambda *_: (0,) * nd, pipeline_mode=pl.Buffered(1))


def _params(*sem):
    return pltpu.CompilerParams(dimension_semantics=sem, vmem_limit_bytes=VMEM_LIMIT)


def _ffn_kernel(x_ref, g_ref, wgu_ref, wd_ref, o_ref, *, tf):
    x = x_ref[...]
    h = _rmsnorm(x, g_ref[...]).astype(BF16)
    acc = jnp.zeros(x.shape, F32)
    for c in range(D_FF // tf):
        gate = _dot(h, wgu_ref[:, c * tf:(c + 1) * tf])
        up = _dot(h, wgu_ref[:, D_FF + c * tf:D_FF + (c + 1) * tf])
        act = (gate * jax.nn.sigmoid(gate) * up).astype(BF16)
        acc = acc + _dot(act, wd_ref[c * tf:(c + 1) * tf, :])
    o_ref[...] = x + 0.5 * acc


def _ffn(x, gain, w_gu, w_down, *, tm, tf):
    seq = x.shape[0]
    row = pl.BlockSpec((tm, D_MODEL), lambda i: (i, 0))
    return pl.pallas_call(
        functools.partial(_ffn_kernel, tf=tf),
        grid=(seq // tm,),
        in_specs=[row, _const_spec((1, D_MODEL)), _const_spec(w_gu.shape), _const_spec(w_down.shape)],
        out_specs=row,
        out_shape=jax.ShapeDtypeStruct(x.shape, F32),
        compiler_params=_params("parallel"),
        name="ffn",
    )(x, gain, w_gu, w_down)


def _pre_kernel(x_ref, g_ref, win_ref, qlat_ref, kvlat_ref, wuq_ref, wuk_ref, wuvt_ref, gains_ref,
                cos_ref, sin_ref, q_ref, k_ref, vt_ref, zp_ref):
    h = _rmsnorm(x_ref[...], g_ref[...]).astype(BF16)
    z = _dot(h, win_ref[...])
    cq = _rmsnorm(z[:, C_Q:C_KV], qlat_ref[...]).astype(BF16)
    ckv = _rmsnorm(z[:, C_KV:C_POOL], kvlat_ref[...]).astype(BF16)
    zp_ref[...] = z[:, C_POOL:C_PEA]
    pe = z[:, C_PEA:C_PEB]
    pe_sw = z[:, C_PEB:D_IN_R]
    qq = _dot(cq, wuq_ref[...])
    kk = _dot(ckv, wuk_ref[...])
    vt = lax.dot_general(wuvt_ref[...], ckv, (((1,), (1,)), ((), ())), preferred_element_type=F32)
    cos = cos_ref[...]
    sin = sin_ref[...]
    gq, gq_sw = gains_ref[0:1, :], gains_ref[1:2, :]
    gk, gk_sw = gains_ref[2:3, :], gains_ref[3:4, :]
    scale = QK_HEAD ** -0.5 * LOG2_E
    for hd in range(MLA_HEADS):
        lo = hd * HEAD_PAD
        qh = qq[:, lo:lo + HEAD_PAD]
        qs = qq[:, HEADS_W + lo:HEADS_W + lo + HEAD_PAD]
        r = lax.rsqrt(jnp.sum(qh * qh, axis=-1, keepdims=True) * (1.0 / QK_HEAD) + EPS)
        qo = (qh * r * gq) * cos + (qs * r * gq_sw) * sin
        q_ref[:, lo:lo + HEAD_PAD] = (qo * scale).astype(BF16)
        kh = kk[:, lo:lo + HEAD_PAD] + pe
        r = lax.rsqrt(jnp.sum(kh * kh, axis=-1, keepdims=True) * (1.0 / QK_HEAD) + EPS)
        ko = (kh * r * gk) * cos + (pe_sw * r * gk_sw) * sin
        k_ref[:, lo:lo + HEAD_PAD] = ko.astype(BF16)
        vt_ref[hd] = vt[hd * V_HEAD:(hd + 1) * V_HEAD, :].astype(BF16)


def _pre(x, gain, w_in_r, q_lat, kv_lat, w_uq, w_uk, w_uvt, gains, cos, sin, *, tm, tk):
    seq = x.shape[0]
    per = tk // tm
    row = pl.BlockSpec((tm, D_MODEL), lambda i: (i, 0))
    heads = pl.BlockSpec((tm, HEADS_W), lambda i: (i, 0))
    tab = pl.BlockSpec((tm, LANES), lambda i: (i, 0))
    vt_spec = pl.BlockSpec((MLA_HEADS, None, V_HEAD, tm), lambda i: (0, i // per, 0, i % per))
    return pl.pallas_call(
        _pre_kernel,
        grid=(seq // tm,),
        in_specs=[row, _const_spec((1, D_MODEL)), _const_spec(w_in_r.shape),
                  _const_spec((1, Q_LORA)), _const_spec((1, KV_LORA)),
                  _const_spec(w_uq.shape), _const_spec(w_uk.shape), _const_spec(w_uvt.shape),
                  _const_spec(gains.shape), tab, tab],
        out_specs=[heads, heads, vt_spec, pl.BlockSpec((tm, POOL_WIDTH), lambda i: (i, 0))],
        out_shape=[jax.ShapeDtypeStruct((seq, HEADS_W), BF16)] * 2
        + [jax.ShapeDtypeStruct((MLA_HEADS, seq // tk, V_HEAD, tk), BF16),
           jax.ShapeDtypeStruct((seq, POOL_WIDTH), F32)],
        compiler_params=_params("parallel"),
        name="pre",
    )(x, gain, w_in_r, q_lat, kv_lat, w_uq, w_uk, w_uvt, gains, cos, sin)


def _fold_rows(p):
    return jnp.sum(p.reshape(p.shape[0] // SUBLANES, SUBLANES, p.shape[1]), axis=0)


def _store_heads(o_ref, acc, den):
    out_t = acc / jnp.sum(den, axis=0, keepdims=True)
    out_t = jnp.concatenate([out_t, jnp.zeros((HEAD_PAD - V_HEAD, acc.shape[1]), F32)], axis=0)
    o_ref[...] = out_t.T.astype(BF16)


def _attn_kernel(q_ref, k_ref, vt_ref, o_ref, s_ref, *, tk):
    tq = q_ref.shape[0]
    nk = k_ref.shape[0] // tk
    nsub = tk // KSUB
    qt = q_ref[...].astype(F32).T.astype(BF16)

    def scores_sub(j, c, slot, cmax):
        kc = k_ref[pl.ds(pl.multiple_of(j * tk + c * KSUB, KSUB), KSUB), :]
        st = _dot(kc, qt)
        s_ref[slot, c * KSUB:(c + 1) * KSUB, :] = st
        return jnp.maximum(cmax, jnp.max(st.reshape(KSUB // SUBLANES, SUBLANES, tq), axis=0))

    def step(j, slot, cmax, m, den, acc, j_next):
        m_new = jnp.maximum(m, jnp.max(cmax, axis=0, keepdims=True))
        alpha = jnp.exp2(m - m_new)
        den, acc = alpha * den, alpha * acc
        cmax_next = jnp.full((SUBLANES, tq), -jnp.inf, F32)
        for c in range(nsub):
            cmax_next = scores_sub(j_next, c, 1 - slot, cmax_next)
            p = jnp.exp2(s_ref[slot, c * KSUB:(c + 1) * KSUB, :] - m_new)
            den = den + _fold_rows(p)
            acc = acc + _dot(vt_ref[j, :, c * KSUB:(c + 1) * KSUB], p.astype(BF16))
        return cmax_next, m_new, den, acc

    def pair(jj, carry):
        j = 2 * jj
        carry = step(j, 0, *carry, j + 1)
        return step(j + 1, 1, *carry, jnp.minimum(j + 2, nk - 1))

    cmax0 = jnp.full((SUBLANES, tq), -jnp.inf, F32)
    for c in range(nsub):
        cmax0 = scores_sub(0, c, 0, cmax0)
    m0 = jnp.full((1, tq), -jnp.inf, F32)
    den0 = jnp.zeros((SUBLANES, tq), F32)
    acc0 = jnp.zeros((V_HEAD, tq), F32)
    _, _, den, acc = lax.fori_loop(0, nk // 2, pair, (cmax0, m0, den0, acc0))
    _store_heads(o_ref, acc, den)


def _attn_unshifted_kernel(q_ref, k_ref, vt_ref, o_ref, s_ref, *, tk):
    tq = q_ref.shape[0]
    nk = k_ref.shape[0] // tk
    nsub = tk // KSUB
    last = k_ref.shape[0] // KSUB - 1
    qt = q_ref[...].astype(F32).T.astype(BF16)

    nq = tq // QSPLIT

    def scores_sub(g, slot, h):
        kc = k_ref[pl.ds(pl.multiple_of(g * KSUB, KSUB), KSUB), :]
        s_ref[slot, :, h * nq:(h + 1) * nq] = _dot(kc, qt[:, h * nq:(h + 1) * nq])

    per = min(CHUNKS_PER_TRIP, nk)

    def trip(jj, carry):
        dens, accs = list(carry[0]), list(carry[1])
        for u in range(per * nsub):
            j, c = jj * per + u // nsub, u % nsub
            g = j * nsub + c
            for h in range(QSPLIT):
                scores_sub(jnp.minimum(g + AHEAD, last), (u + AHEAD) % RING, h)
                p = jnp.exp2(s_ref[u % RING, :, h * nq:(h + 1) * nq])
                dens[h] = dens[h] + _fold_rows(p)
                accs[h] = accs[h] + _dot(vt_ref[j, :, c * KSUB:(c + 1) * KSUB], p.astype(BF16))
        return tuple(dens), tuple(accs)

    for g in range(AHEAD):
        for h in range(QSPLIT):
            scores_sub(g, g % RING, h)
    dens, accs = lax.fori_loop(0, nk // per, trip,
                               (tuple(jnp.zeros((SUBLANES, nq), F32) for _ in range(QSPLIT)),
                                tuple(jnp.zeros((V_HEAD, nq), F32) for _ in range(QSPLIT))))
    _store_heads(o_ref, jnp.concatenate(accs, axis=1), jnp.concatenate(dens, axis=1))


def _attn(q, k, vt, *, tq, tk, shifted):
    seq = q.shape[0]
    qspec = pl.BlockSpec((tq, HEAD_PAD), lambda h, i: (i, h))
    kspec = pl.BlockSpec((seq, HEAD_PAD), lambda h, i: (0, h))
    vspec = pl.BlockSpec((None, seq // tk, V_HEAD, tk), lambda h, i: (h, 0, 0, 0))
    body = _attn_kernel if shifted else _attn_unshifted_kernel
    return pl.pallas_call(
        functools.partial(body, tk=tk),
        grid=(MLA_HEADS, seq // tq),
        in_specs=[qspec, kspec, vspec],
        out_specs=qspec,
        out_shape=jax.ShapeDtypeStruct((seq, HEADS_W), BF16),
        scratch_shapes=[pltpu.VMEM((2, tk, tq) if shifted else (RING, KSUB, tq), F32)],
        compiler_params=_params("parallel", "parallel"),
        name="attn" if shifted else "attn_unshifted",
    )(q, k, vt)


def _score_bound(q_gain, k_gain):
    return (LOG2_E * QK_HEAD ** 0.5) * jnp.max(jnp.abs(q_gain)) * jnp.max(jnp.abs(k_gain)) * BF16_NORM_SLACK


def _post_kernel(x_ref, o_ref, zc_ref, zprev_ref, znext_ref, wpool_ref, ps_ref, woa_ref, wob_ref,
                 out_ref, ext_ref, *, tm, seq):
    i = pl.program_id(0)
    last = pl.num_programs(0) - 1
    zc = zc_ref[...]
    ext_ref[0:POOL_HALO, :] = jnp.where(i > 0, zprev_ref[...], 0.0)
    ext_ref[POOL_HALO:POOL_HALO + tm, :] = zc
    ext_ref[POOL_HALO + tm:2 * POOL_HALO + tm, :] = jnp.where(i < last, znext_ref[...], 0.0)
    pos = i * tm + lax.broadcasted_iota(jnp.int32, (tm, 1), 0)
    parts = []
    for g, w in enumerate(POOL_WINDOWS):
        left = w // 2
        right = w - 1 - left
        c0 = g * POOL_GROUP_DIM
        wsum = ext_ref[POOL_HALO - left:POOL_HALO - left + tm, c0:c0 + POOL_GROUP_DIM]
        for d in range(-left + 1, right + 1):
            wsum = wsum + ext_ref[POOL_HALO + d:POOL_HALO + d + tm, c0:c0 + POOL_GROUP_DIM]
        cnt = (jnp.minimum(pos + right + 1, seq) - jnp.maximum(pos - left, 0)).astype(F32)
        mixed = (wsum / cnt - zc[:, c0:c0 + POOL_GROUP_DIM]).astype(BF16)
        y = _dot(mixed, wpool_ref[g]) * ps_ref[:, c0:c0 + POOL_GROUP_DIM]
        parts.append(y.astype(BF16))
    b = jnp.concatenate(parts, axis=1)
    out_ref[...] = x_ref[...] + (_dot(o_ref[...], woa_ref[...]) + _dot(b, wob_ref[...]))


def _post(x, o, zp, w_pool, pool_scale, w_out_a, w_out_b, *, tm):
    seq = x.shape[0]
    per = tm // POOL_HALO
    nblk = seq // POOL_HALO
    row = pl.BlockSpec((tm, D_MODEL), lambda i: (i, 0))
    return pl.pallas_call(
        functools.partial(_post_kernel, tm=tm, seq=seq),
        grid=(seq // tm,),
        in_specs=[row,
                  pl.BlockSpec((tm, HEADS_W), lambda i: (i, 0)),
                  pl.BlockSpec((tm, POOL_WIDTH), lambda i: (i, 0)),
                  pl.BlockSpec((POOL_HALO, POOL_WIDTH), lambda i: (jnp.maximum(i * per - 1, 0), 0)),
                  pl.BlockSpec((POOL_HALO, POOL_WIDTH), lambda i: (jnp.minimum((i + 1) * per, nblk - 1), 0)),
                  _const_spec(w_pool.shape), _const_spec((1, POOL_WIDTH)),
                  _const_spec(w_out_a.shape), _const_spec(w_out_b.shape)],
        out_specs=row,
        out_shape=jax.ShapeDtypeStruct(x.shape, F32),
        scratch_shapes=[pltpu.VMEM((tm + 2 * POOL_HALO, POOL_WIDTH), F32)],
        compiler_params=_params("parallel"),
        name="post",
    )(x, o, zp, zp, zp, w_pool, pool_scale, w_out_a, w_out_b)


def _swap_halves(t):
    return jnp.concatenate([t[..., HALF_ROPE:], t[..., :HALF_ROPE]], axis=-1)


def _head_block(nope, rope):
    pad = jnp.zeros(nope.shape[:-1] + (HEAD_PAD - QK_HEAD,), nope.dtype)
    return jnp.concatenate([nope, rope, pad], axis=-1)


def _layout_weights(w_in, w_uq, w_uk, w_uv, q_norm, k_norm, w_out):
    depth = w_in.shape[0]
    o_pe = Q_LORA + KV_LORA
    o_pool = o_pe + QK_ROPE
    w_pe = w_in[..., o_pe:o_pool]
    zero_nope = jnp.zeros(w_pe.shape[:-1] + (QK_NOPE,), w_pe.dtype)
    w_in_r = jnp.concatenate(
        [w_in[..., :o_pe], w_in[..., o_pool:], _head_block(zero_nope, w_pe),
         _head_block(zero_nope, _swap_halves(w_pe))], axis=-1).astype(BF16)

    uq = w_uq.reshape(depth, Q_LORA, MLA_HEADS, QK_HEAD)
    uq_main = _head_block(uq[..., :QK_NOPE], uq[..., QK_NOPE:])
    uq_swap = _head_block(jnp.zeros_like(uq[..., :QK_NOPE]), _swap_halves(uq[..., QK_NOPE:]))
    w_uq_r = jnp.concatenate([uq_main.reshape(depth, Q_LORA, HEADS_W),
                              uq_swap.reshape(depth, Q_LORA, HEADS_W)], axis=-1).astype(BF16)

    uk = jnp.pad(w_uk.reshape(depth, KV_LORA, MLA_HEADS, QK_NOPE),
                 ((0, 0), (0, 0), (0, 0), (0, HEAD_PAD - QK_NOPE)))
    w_uk_r = uk.reshape(depth, KV_LORA, HEADS_W).astype(BF16)
    uvt = jnp.transpose(w_uv.reshape(depth, KV_LORA, MLA_HEADS, V_HEAD), (0, 2, 3, 1))
    w_uvt_r = uvt.reshape(depth, MLA_HEADS * V_HEAD, KV_LORA).astype(BF16)

    def gain_rows(g):
        main = _head_block(g[..., :QK_NOPE], g[..., QK_NOPE:])
        swap = _head_block(jnp.zeros_like(g[..., :QK_NOPE]), _swap_halves(g[..., QK_NOPE:]))
        return main, swap

    gq, gq_sw = gain_rows(q_norm)
    gk, gk_sw = gain_rows(k_norm)
    zero = jnp.zeros_like(gq)
    gains = jnp.stack([gq, gq_sw, gk, gk_sw, zero, zero, zero, zero], axis=1)

    oa = w_out[:, :MLA_WIDTH].reshape(depth, MLA_HEADS, V_HEAD, D_MODEL)
    oa = jnp.pad(oa, ((0, 0), (0, 0), (0, HEAD_PAD - V_HEAD), (0, 0))).reshape(depth, HEADS_W, D_MODEL)
    return w_in_r, w_uq_r, w_uk_r, w_uvt_r, gains, oa.astype(BF16), w_out[:, MLA_WIDTH:].astype(BF16)


def _rope_tables(seq):
    pos = jnp.arange(seq, dtype=F32)
    inv = ROPE_THETA ** (-jnp.arange(0, QK_ROPE, 2, dtype=F32) / QK_ROPE)
    ang = pos[:, None] * inv[None, :]
    cos, sin = jnp.cos(ang), jnp.sin(ang)
    ones = jnp.ones((seq, QK_NOPE), F32)
    zeros = jnp.zeros((seq, QK_NOPE), F32)
    return (_head_block(ones, jnp.concatenate([cos, cos], axis=-1)),
            _head_block(zeros, jnp.concatenate([-sin, sin], axis=-1)))


def _tiles(seq):
    tm = min(512, seq)
    return dict(tm=tm, tq=min(1024, seq), tk=min(2048, seq // 2), tf=D_FF)


def kernel(x, ffn1_norm, ffn1_w_gu, ffn1_w_down, mix_norm, w_in, q_lat_norm, kv_lat_norm, w_uq, w_uk, w_uv,
           q_norm, k_norm, w_pool, pool_scale, w_out, ffn2_norm, ffn2_w_gu, ffn2_w_down):
    batch, seq, _ = x.shape
    depth = w_in.shape[0]
    t = _tiles(seq)
    cos, sin = _rope_tables(seq)
    w_in_r, w_uq_r, w_uk_r, w_uvt_r, gains, w_out_a, w_out_b = _layout_weights(
        w_in, w_uq, w_uk, w_uv, q_norm, k_norm, w_out)
    f1_gu, f1_d = ffn1_w_gu.astype(BF16), ffn1_w_down.astype(BF16)
    f2_gu, f2_d = ffn2_w_gu.astype(BF16), ffn2_w_down.astype(BF16)
    w_pool_b = w_pool.astype(BF16)

    outs = []
    for b in range(batch):
        xb = x[b]
        for l in range(depth):
            xb = _ffn(xb, ffn1_norm[l][None], f1_gu[l], f1_d[l], tm=t["tm"], tf=t["tf"])
            q, k, vt, zp = _pre(xb, mix_norm[l][None], w_in_r[l], q_lat_norm[l][None], kv_lat_norm[l][None],
                                w_uq_r[l], w_uk_r[l], w_uvt_r[l], gains[l], cos, sin, tm=t["tm"], tk=t["tk"])
            o = lax.cond(_score_bound(q_norm[l], k_norm[l]) <= MAX_UNSHIFTED_SCORE,
                         functools.partial(_attn, tq=t["tq"], tk=t["tk"], shifted=False),
                         functools.partial(_attn, tq=t["tq"], tk=t["tk"], shifted=True),
                         q, k, vt)
            xb = _post(xb, o, zp, w_pool_b[l], pool_scale[l][None], w_out_a[l], w_out_b[l], tm=t["tm"])
            xb = _ffn(xb, ffn2_norm[l][None], f2_gu[l], f2_d[l], tm=t["tm"], tf=t["tf"])
        outs.append(xb)
    return jnp.stack(outs, axis=0)
```

```python
import functools

import jax
import jax.numpy as jnp
from jax import lax
from jax.experimental import pallas as pl
from jax.experimental.pallas import tpu as pltpu

D_MODEL = 1024
MLA_HEADS = 8
QK_NOPE = 64
QK_ROPE = 32
QK_HEAD = QK_NOPE + QK_ROPE
V_HEAD = 64
Q_LORA = 384
KV_LORA = 256
MLA_WIDTH = MLA_HEADS * V_HEAD
ROPE_THETA = 10000.0
POOL_WINDOWS = (2, 4, 8, 16)
POOL_GROUP_DIM = 128
POOL_WIDTH = 512
D_FF = 2816
EPS = 1e-6
LOG2_E = 1.4426950408889634
MAX_UNSHIFTED_SCORE = 60.0
BF16_NORM_SLACK = 1.01

LANES = 128
SUBLANES = 8
HEAD_PAD = LANES
HEADS_W = MLA_HEADS * HEAD_PAD
HALF_ROPE = QK_ROPE // 2
POOL_HALO = 8
KSUB = 256
CHUNKS_PER_TRIP = 8
QSPLIT = 2
AHEAD = 1
RING = 2 * AHEAD
C_Q, C_KV, C_POOL = 0, Q_LORA, Q_LORA + KV_LORA
C_PEA = C_POOL + POOL_WIDTH
C_PEB = C_PEA + LANES
D_IN_R = C_PEB + LANES

VMEM_LIMIT = 56 * 1024 * 1024

F32 = jnp.float32
BF16 = jnp.bfloat16


def _rmsnorm(x, g):
    ms = jnp.mean(x * x, axis=-1, keepdims=True)
    return x * lax.rsqrt(ms + EPS) * g


def _dot(a, b):
    return jnp.dot(a, b, preferred_element_type=F32)


def _layer_spec(stacked, layer):
    tail = stacked.shape[1:]
    return pl.BlockSpec((None,) + tail, lambda *_: (layer,) + (0,) * len(tail), pipeline_mode=pl.Buffered(1))


def _params(*sem):
    return pltpu.CompilerParams(dimension_semantics=sem, vmem_limit_bytes=VMEM_LIMIT)


def _ffn_kernel(x_ref, g_ref, wgu_ref, wd_ref, o_ref, *, tf):
    x = x_ref[...]
    h = _rmsnorm(x, g_ref[...]).astype(BF16)
    acc = jnp.zeros(x.shape, F32)
    for c in range(D_FF // tf):
        gate = _dot(h, wgu_ref[:, c * tf:(c + 1) * tf])
        up = _dot(h, wgu_ref[:, D_FF + c * tf:D_FF + (c + 1) * tf])
        act = (gate * jax.nn.sigmoid(gate) * up).astype(BF16)
        acc = acc + _dot(act, wd_ref[c * tf:(c + 1) * tf, :])
    o_ref[...] = x + 0.5 * acc


def _ffn(x, gain, w_gu, w_down, layer, *, tm, tf):
    seq = x.shape[0]
    row = pl.BlockSpec((tm, D_MODEL), lambda i: (i, 0))
    return pl.pallas_call(
        functools.partial(_ffn_kernel, tf=tf),
        grid=(seq // tm,),
        in_specs=[row] + [_layer_spec(a, layer) for a in (gain, w_gu, w_down)],
        out_specs=row,
        out_shape=jax.ShapeDtypeStruct(x.shape, F32),
        compiler_params=_params("parallel"),
        name="ffn",
    )(x, gain, w_gu, w_down)


def _pre_kernel(x_ref, g_ref, win_ref, qlat_ref, kvlat_ref, wuq_ref, wuk_ref, wuvt_ref, gains_ref,
                cos_ref, sin_ref, q_ref, k_ref, vt_ref, zp_ref):
    h = _rmsnorm(x_ref[...], g_ref[...]).astype(BF16)
    z = _dot(h, win_ref[...])
    cq = _rmsnorm(z[:, C_Q:C_KV], qlat_ref[...]).astype(BF16)
    ckv = _rmsnorm(z[:, C_KV:C_POOL], kvlat_ref[...]).astype(BF16)
    zp_ref[...] = z[:, C_POOL:C_PEA]
    pe = z[:, C_PEA:C_PEB]
    pe_sw = z[:, C_PEB:D_IN_R]
    qq = _dot(cq, wuq_ref[...])
    kk = _dot(ckv, wuk_ref[...])
    vt = lax.dot_general(wuvt_ref[...], ckv, (((1,), (1,)), ((), ())), preferred_element_type=F32)
    cos = cos_ref[...]
    sin = sin_ref[...]
    scale = QK_HEAD ** -0.5 * LOG2_E
    q_main, q_swap = gains_ref[0:1, :] * cos * scale, gains_ref[1:2, :] * sin * scale
    k_main = gains_ref[2:3, :] * cos
    pe_rot = pe_sw * (gains_ref[3:4, :] * sin)
    for hd in range(MLA_HEADS):
        lo = hd * HEAD_PAD
        qh = qq[:, lo:lo + HEAD_PAD]
        qs = qq[:, HEADS_W + lo:HEADS_W + lo + HEAD_PAD]
        r = lax.rsqrt(jnp.sum(qh * qh, axis=-1, keepdims=True) * (1.0 / QK_HEAD) + EPS)
        q_ref[:, lo:lo + HEAD_PAD] = ((qh * q_main + qs * q_swap) * r).astype(BF16)
        kh = kk[:, lo:lo + HEAD_PAD] + pe
        r = lax.rsqrt(jnp.sum(kh * kh, axis=-1, keepdims=True) * (1.0 / QK_HEAD) + EPS)
        k_ref[:, lo:lo + HEAD_PAD] = ((kh * k_main + pe_rot) * r).astype(BF16)
        vt_ref[hd] = vt[hd * V_HEAD:(hd + 1) * V_HEAD, :].astype(BF16)


def _pre(x, gain, w_in_r, q_lat, kv_lat, w_uq, w_uk, w_uvt, gains, cos, sin, layer, *, tm, tk):
    seq = x.shape[0]
    per = tk // tm
    row = pl.BlockSpec((tm, D_MODEL), lambda i: (i, 0))
    heads = pl.BlockSpec((tm, HEADS_W), lambda i: (i, 0))
    tab = pl.BlockSpec((tm, LANES), lambda i: (i, 0))
    vt_spec = pl.BlockSpec((MLA_HEADS, None, V_HEAD, tm), lambda i: (0, i // per, 0, i % per))
    return pl.pallas_call(
        _pre_kernel,
        grid=(seq // tm,),
        in_specs=[row] + [_layer_spec(a, layer) for a in (gain, w_in_r, q_lat, kv_lat, w_uq, w_uk, w_uvt, gains)]
        + [tab, tab],
        out_specs=[heads, heads, vt_spec, pl.BlockSpec((tm, POOL_WIDTH), lambda i: (i, 0))],
        out_shape=[jax.ShapeDtypeStruct((seq, HEADS_W), BF16)] * 2
        + [jax.ShapeDtypeStruct((MLA_HEADS, seq // tk, V_HEAD, tk), BF16),
           jax.ShapeDtypeStruct((seq, POOL_WIDTH), F32)],
        compiler_params=_params("parallel"),
        name="pre",
    )(x, gain, w_in_r, q_lat, kv_lat, w_uq, w_uk, w_uvt, gains, cos, sin)


def _fold_rows(p):
    return jnp.sum(p.reshape(p.shape[0] // SUBLANES, SUBLANES, p.shape[1]), axis=0)


def _store_heads(o_ref, acc, den):
    out_t = acc / jnp.sum(den, axis=0, keepdims=True)
    out_t = jnp.concatenate([out_t, jnp.zeros((HEAD_PAD - V_HEAD, acc.shape[1]), F32)], axis=0)
    o_ref[...] = out_t.T.astype(BF16)


def _attn_kernel(q_ref, k_ref, vt_ref, o_ref, s_ref, *, tk):
    tq = q_ref.shape[0]
    nk = k_ref.shape[0] // tk
    nsub = tk // KSUB
    qt = q_ref[...].astype(F32).T.astype(BF16)

    def scores_sub(j, c, slot, cmax):
        kc = k_ref[pl.ds(pl.multiple_of(j * tk + c * KSUB, KSUB), KSUB), :]
        st = _dot(kc, qt)
        s_ref[slot, c * KSUB:(c + 1) * KSUB, :] = st
        return jnp.maximum(cmax, jnp.max(st.reshape(KSUB // SUBLANES, SUBLANES, tq), axis=0))

    def step(j, slot, cmax, m, den, acc, j_next):
        m_new = jnp.maximum(m, jnp.max(cmax, axis=0, keepdims=True))
        alpha = jnp.exp2(m - m_new)
        den, acc = alpha * den, alpha * acc
        cmax_next = jnp.full((SUBLANES, tq), -jnp.inf, F32)
        for c in range(nsub):
            cmax_next = scores_sub(j_next, c, 1 - slot, cmax_next)
            p = jnp.exp2(s_ref[slot, c * KSUB:(c + 1) * KSUB, :] - m_new)
            den = den + _fold_rows(p)
            acc = acc + _dot(vt_ref[j, :, c * KSUB:(c + 1) * KSUB], p.astype(BF16))
        return cmax_next, m_new, den, acc

    def pair(jj, carry):
        j = 2 * jj
        carry = step(j, 0, *carry, j + 1)
        return step(j + 1, 1, *carry, jnp.minimum(j + 2, nk - 1))

    cmax0 = jnp.full((SUBLANES, tq), -jnp.inf, F32)
    for c in range(nsub):
        cmax0 = scores_sub(0, c, 0, cmax0)
    m0 = jnp.full((1, tq), -jnp.inf, F32)
    den0 = jnp.zeros((SUBLANES, tq), F32)
    acc0 = jnp.zeros((V_HEAD, tq), F32)
    _, _, den, acc = lax.fori_loop(0, nk // 2, pair, (cmax0, m0, den0, acc0))
    _store_heads(o_ref, acc, den)


def _attn_unshifted_kernel(q_ref, k_ref, vt_ref, o_ref, s_ref, *, tk):
    tq = q_ref.shape[0]
    nk = k_ref.shape[0] // tk
    nsub = tk // KSUB
    last = k_ref.shape[0] // KSUB - 1
    qt = q_ref[...].astype(F32).T.astype(BF16)

    nq = tq // QSPLIT

    def scores_sub(g, slot, h):
        kc = k_ref[pl.ds(pl.multiple_of(g * KSUB, KSUB), KSUB), :]
        s_ref[slot, :, h * nq:(h + 1) * nq] = _dot(kc, qt[:, h * nq:(h + 1) * nq])

    per = min(CHUNKS_PER_TRIP, nk)

    def trip(jj, carry):
        dens, accs = list(carry[0]), list(carry[1])
        for u in range(per * nsub):
            j, c = jj * per + u // nsub, u % nsub
            g = j * nsub + c
            for h in range(QSPLIT):
                scores_sub(jnp.minimum(g + AHEAD, last), (u + AHEAD) % RING, h)
                p = jnp.exp2(s_ref[u % RING, :, h * nq:(h + 1) * nq])
                dens[h] = dens[h] + _fold_rows(p)
                accs[h] = accs[h] + _dot(vt_ref[j, :, c * KSUB:(c + 1) * KSUB], p.astype(BF16))
        return tuple(dens), tuple(accs)

    for g in range(AHEAD):
        for h in range(QSPLIT):
            scores_sub(g, g % RING, h)
    dens, accs = lax.fori_loop(0, nk // per, trip,
                               (tuple(jnp.zeros((SUBLANES, nq), F32) for _ in range(QSPLIT)),
                                tuple(jnp.zeros((V_HEAD, nq), F32) for _ in range(QSPLIT))))
    _store_heads(o_ref, jnp.concatenate(accs, axis=1), jnp.concatenate(dens, axis=1))


def _attn(q, k, vt, *, tq, tk, shifted):
    seq = q.shape[0]
    qspec = pl.BlockSpec((tq, HEAD_PAD), lambda h, i: (i, h))
    kspec = pl.BlockSpec((seq, HEAD_PAD), lambda h, i: (0, h))
    vspec = pl.BlockSpec((None, seq // tk, V_HEAD, tk), lambda h, i: (h, 0, 0, 0))
    body = _attn_kernel if shifted else _attn_unshifted_kernel
    return pl.pallas_call(
        functools.partial(body, tk=tk),
        grid=(MLA_HEADS, seq // tq),
        in_specs=[qspec, kspec, vspec],
        out_specs=qspec,
        out_shape=jax.ShapeDtypeStruct((seq, HEADS_W), BF16),
        scratch_shapes=[pltpu.VMEM((2, tk, tq) if shifted else (RING, KSUB, tq), F32)],
        compiler_params=_params("parallel", "parallel"),
        name="attn" if shifted else "attn_unshifted",
    )(q, k, vt)


def _score_bound(q_gain, k_gain):
    return (LOG2_E * QK_HEAD ** 0.5) * jnp.max(jnp.abs(q_gain)) * jnp.max(jnp.abs(k_gain)) * BF16_NORM_SLACK


def _post_kernel(x_ref, o_ref, zc_ref, zprev_ref, znext_ref, wpool_ref, ps_ref, woa_ref, wob_ref,
                 out_ref, ext_ref, *, tm, seq):
    i = pl.program_id(0)
    last = pl.num_programs(0) - 1
    zc = zc_ref[...]
    ext_ref[0:POOL_HALO, :] = jnp.where(i > 0, zprev_ref[...], 0.0)
    ext_ref[POOL_HALO:POOL_HALO + tm, :] = zc
    ext_ref[POOL_HALO + tm:2 * POOL_HALO + tm, :] = jnp.where(i < last, znext_ref[...], 0.0)
    pos = i * tm + lax.broadcasted_iota(jnp.int32, (tm, 1), 0)
    parts = []
    for g, w in enumerate(POOL_WINDOWS):
        left = w // 2
        right = w - 1 - left
        c0 = g * POOL_GROUP_DIM
        wsum = ext_ref[POOL_HALO - left:POOL_HALO - left + tm, c0:c0 + POOL_GROUP_DIM]
        for d in range(-left + 1, right + 1):
            wsum = wsum + ext_ref[POOL_HALO + d:POOL_HALO + d + tm, c0:c0 + POOL_GROUP_DIM]
        cnt = (jnp.minimum(pos + right + 1, seq) - jnp.maximum(pos - left, 0)).astype(F32)
        mixed = (wsum / cnt - zc[:, c0:c0 + POOL_GROUP_DIM]).astype(BF16)
        y = _dot(mixed, wpool_ref[g]) * ps_ref[:, c0:c0 + POOL_GROUP_DIM]
        parts.append(y.astype(BF16))
    b = jnp.concatenate(parts, axis=1)
    out_ref[...] = x_ref[...] + (_dot(o_ref[...], woa_ref[...]) + _dot(b, wob_ref[...]))


def _post(x, o, zp, w_pool, pool_scale, w_out_a, w_out_b, layer, *, tm):
    seq = x.shape[0]
    per = tm // POOL_HALO
    nblk = seq // POOL_HALO
    row = pl.BlockSpec((tm, D_MODEL), lambda i: (i, 0))
    return pl.pallas_call(
        functools.partial(_post_kernel, tm=tm, seq=seq),
        grid=(seq // tm,),
        in_specs=[row,
                  pl.BlockSpec((tm, HEADS_W), lambda i: (i, 0)),
                  pl.BlockSpec((tm, POOL_WIDTH), lambda i: (i, 0)),
                  pl.BlockSpec((POOL_HALO, POOL_WIDTH), lambda i: (jnp.maximum(i * per - 1, 0), 0)),
                  pl.BlockSpec((POOL_HALO, POOL_WIDTH), lambda i: (jnp.minimum((i + 1) * per, nblk - 1), 0)),
                  ] + [_layer_spec(a, layer) for a in (w_pool, pool_scale, w_out_a, w_out_b)],
        out_specs=row,
        out_shape=jax.ShapeDtypeStruct(x.shape, F32),
        scratch_shapes=[pltpu.VMEM((tm + 2 * POOL_HALO, POOL_WIDTH), F32)],
        compiler_params=_params("parallel"),
        name="post",
    )(x, o, zp, zp, zp, w_pool, pool_scale, w_out_a, w_out_b)


def _swap_halves(t):
    return jnp.concatenate([t[..., HALF_ROPE:], t[..., :HALF_ROPE]], axis=-1)


def _head_block(nope, rope):
    pad = jnp.zeros(nope.shape[:-1] + (HEAD_PAD - QK_HEAD,), nope.dtype)
    return jnp.concatenate([nope, rope, pad], axis=-1)


def _layout_weights(w_in, w_uq, w_uk, w_uv, q_norm, k_norm, w_out):
    depth = w_in.shape[0]
    o_pe = Q_LORA + KV_LORA
    o_pool = o_pe + QK_ROPE
    w_pe = w_in[..., o_pe:o_pool]
    zero_nope = jnp.zeros(w_pe.shape[:-1] + (QK_NOPE,), w_pe.dtype)
    w_in_r = jnp.concatenate(
        [w_in[..., :o_pe], w_in[..., o_pool:], _head_block(zero_nope, w_pe),
         _head_block(zero_nope, _swap_halves(w_pe))], axis=-1).astype(BF16)

    uq = w_uq.reshape(depth, Q_LORA, MLA_HEADS, QK_HEAD)
    uq_main = _head_block(uq[..., :QK_NOPE], uq[..., QK_NOPE:])
    uq_swap = _head_block(jnp.zeros_like(uq[..., :QK_NOPE]), _swap_halves(uq[..., QK_NOPE:]))
    w_uq_r = jnp.concatenate([uq_main.reshape(depth, Q_LORA, HEADS_W),
                              uq_swap.reshape(depth, Q_LORA, HEADS_W)], axis=-1).astype(BF16)

    uk = jnp.pad(w_uk.reshape(depth, KV_LORA, MLA_HEADS, QK_NOPE),
                 ((0, 0), (0, 0), (0, 0), (0, HEAD_PAD - QK_NOPE)))
    w_uk_r = uk.reshape(depth, KV_LORA, HEADS_W).astype(BF16)
    uvt = jnp.transpose(w_uv.reshape(depth, KV_LORA, MLA_HEADS, V_HEAD), (0, 2, 3, 1))
    w_uvt_r = uvt.reshape(depth, MLA_HEADS * V_HEAD, KV_LORA).astype(BF16)

    def gain_rows(g):
        main = _head_block(g[..., :QK_NOPE], g[..., QK_NOPE:])
        swap = _head_block(jnp.zeros_like(g[..., :QK_NOPE]), _swap_halves(g[..., QK_NOPE:]))
        return main, swap

    gq, gq_sw = gain_rows(q_norm)
    gk, gk_sw = gain_rows(k_norm)
    zero = jnp.zeros_like(gq)
    gains = jnp.stack([gq, gq_sw, gk, gk_sw, zero, zero, zero, zero], axis=1)

    oa = w_out[:, :MLA_WIDTH].reshape(depth, MLA_HEADS, V_HEAD, D_MODEL)
    oa = jnp.pad(oa, ((0, 0), (0, 0), (0, HEAD_PAD - V_HEAD), (0, 0))).reshape(depth, HEADS_W, D_MODEL)
    return w_in_r, w_uq_r, w_uk_r, w_uvt_r, gains, oa.astype(BF16), w_out[:, MLA_WIDTH:].astype(BF16)


def _rope_tables(seq):
    pos = jnp.arange(seq, dtype=F32)
    inv = ROPE_THETA ** (-jnp.arange(0, QK_ROPE, 2, dtype=F32) / QK_ROPE)
    ang = pos[:, None] * inv[None, :]
    cos, sin = jnp.cos(ang), jnp.sin(ang)
    ones = jnp.ones((seq, QK_NOPE), F32)
    zeros = jnp.zeros((seq, QK_NOPE), F32)
    return (_head_block(ones, jnp.concatenate([cos, cos], axis=-1)),
            _head_block(zeros, jnp.concatenate([-sin, sin], axis=-1)))


def _tiles(seq):
    tm = min(512, seq)
    return dict(tm=tm, tm_pre=min(1024, seq), tq=min(1024, seq), tk=min(2048, seq // 2), tf=D_FF)


def kernel(x, ffn1_norm, ffn1_w_gu, ffn1_w_down, mix_norm, w_in, q_lat_norm, kv_lat_norm, w_uq, w_uk, w_uv,
           q_norm, k_norm, w_pool, pool_scale, w_out, ffn2_norm, ffn2_w_gu, ffn2_w_down):
    batch, seq, _ = x.shape
    depth = w_in.shape[0]
    t = _tiles(seq)
    cos, sin = _rope_tables(seq)
    w_in_r, w_uq_r, w_uk_r, w_uvt_r, gains, w_out_a, w_out_b = _layout_weights(
        w_in, w_uq, w_uk, w_uv, q_norm, k_norm, w_out)
    f1_gu, f1_d = ffn1_w_gu.astype(BF16), ffn1_w_down.astype(BF16)
    f2_gu, f2_d = ffn2_w_gu.astype(BF16), ffn2_w_down.astype(BF16)
    w_pool_b = w_pool.astype(BF16)
    ffn1_g, ffn2_g, mix_g = ffn1_norm[:, None], ffn2_norm[:, None], mix_norm[:, None]
    q_lat_g, kv_lat_g, pool_g = q_lat_norm[:, None], kv_lat_norm[:, None], pool_scale[:, None]

    outs = []
    for b in range(batch):
        xb = x[b]
        for l in range(depth):
            xb = _ffn(xb, ffn1_g, f1_gu, f1_d, l, tm=t["tm"], tf=t["tf"])
            q, k, vt, zp = _pre(xb, mix_g, w_in_r, q_lat_g, kv_lat_g, w_uq_r, w_uk_r, w_uvt_r, gains, cos, sin, l,
                                tm=t["tm_pre"], tk=t["tk"])
            o = lax.cond(_score_bound(q_norm[l], k_norm[l]) <= MAX_UNSHIFTED_SCORE,
                         functools.partial(_attn, tq=t["tq"], tk=t["tk"], shifted=False),
                         functools.partial(_attn, tq=t["tq"], tk=t["tk"], shifted=True),
                         q, k, vt)
            xb = _post(xb, o, zp, w_pool_b, pool_g, w_out_a, w_out_b, l, tm=t["tm"])
            xb = _ffn(xb, ffn2_g, f2_gu, f2_d, l, tm=t["tm"], tf=t["tf"])
        outs.append(xb)
    return jnp.stack(outs, axis=0)
```

```python
import functools

import jax
import jax.numpy as jnp
from jax import lax
from jax.experimental import pallas as pl
from jax.experimental.pallas import tpu as pltpu

D_MODEL = 1024
MLA_HEADS = 8
QK_NOPE = 64
QK_ROPE = 32
QK_HEAD = QK_NOPE + QK_ROPE
V_HEAD = 64
Q_LORA = 384
KV_LORA = 256
MLA_WIDTH = MLA_HEADS * V_HEAD
ROPE_THETA = 10000.0
POOL_WINDOWS = (2, 4, 8, 16)
POOL_GROUP_DIM = 128
POOL_WIDTH = 512
D_FF = 2816
EPS = 1e-6
LOG2_E = 1.4426950408889634
MAX_UNSHIFTED_SCORE = 60.0
BF16_NORM_SLACK = 1.01

LANES = 128
SUBLANES = 8
HEAD_PAD = LANES
HEADS_W = MLA_HEADS * HEAD_PAD
HALF_ROPE = QK_ROPE // 2
POOL_HALO = 8
MXU_TILE = 256
KSUB = MXU_TILE
CHUNKS_PER_TRIP = 8
QSPLIT = 2
AHEAD = 1
RING = 2 * AHEAD
C_Q, C_KV, C_POOL = 0, Q_LORA, Q_LORA + KV_LORA
C_PEA = C_POOL + POOL_WIDTH
C_PEB = C_PEA + LANES
D_IN_R = C_PEB + LANES

VMEM_LIMIT = 56 * 1024 * 1024

F32 = jnp.float32
BF16 = jnp.bfloat16


def _rmsnorm(x, g):
    ms = jnp.mean(x * x, axis=-1, keepdims=True)
    return x * lax.rsqrt(ms + EPS) * g


def _dot(a, b):
    return jnp.dot(a, b, preferred_element_type=F32)


def _layer_spec(stacked, layer):
    tail = stacked.shape[1:]
    return pl.BlockSpec((None,) + tail, lambda *_: (layer,) + (0,) * len(tail), pipeline_mode=pl.Buffered(1))


def _params(*sem):
    return pltpu.CompilerParams(dimension_semantics=sem, vmem_limit_bytes=VMEM_LIMIT)


def _ffn_kernel(x_ref, g_ref, wgu_ref, wd_ref, o_ref, *, tf):
    x = x_ref[...]
    h = _rmsnorm(x, g_ref[...]).astype(BF16)
    acc = jnp.zeros(x.shape, F32)
    for lo in range(0, D_FF, tf):
        hi = min(lo + tf, D_FF)
        gate = _dot(h, wgu_ref[:, lo:hi])
        up = _dot(h, wgu_ref[:, D_FF + lo:D_FF + hi])
        act = (gate * jax.nn.sigmoid(gate) * up).astype(BF16)
        acc = acc + _dot(act, wd_ref[lo:hi, :])
    o_ref[...] = x + 0.5 * acc


def _ffn(x, gain, w_gu, w_down, layer, *, tm, tf):
    seq = x.shape[0]
    row = pl.BlockSpec((tm, D_MODEL), lambda i: (i, 0))
    return pl.pallas_call(
        functools.partial(_ffn_kernel, tf=tf),
        grid=(seq // tm,),
        in_specs=[row] + [_layer_spec(a, layer) for a in (gain, w_gu, w_down)],
        out_specs=row,
        out_shape=jax.ShapeDtypeStruct(x.shape, F32),
        compiler_params=_params("parallel"),
        name="ffn",
    )(x, gain, w_gu, w_down)


def _pre_kernel(x_ref, g_ref, win_ref, qlat_ref, kvlat_ref, wuq_ref, wuk_ref, wuvt_ref, gains_ref,
                cos_ref, sin_ref, q_ref, k_ref, vt_ref, zp_ref):
    h = _rmsnorm(x_ref[...], g_ref[...]).astype(BF16)
    z = _dot(h, win_ref[...])
    cq = _rmsnorm(z[:, C_Q:C_KV], qlat_ref[...]).astype(BF16)
    ckv = _rmsnorm(z[:, C_KV:C_POOL], kvlat_ref[...]).astype(BF16)
    zp_ref[...] = z[:, C_POOL:C_PEA]
    pe = z[:, C_PEA:C_PEB]
    pe_sw = z[:, C_PEB:D_IN_R]
    qq = _dot(cq, wuq_ref[...])
    kk = _dot(ckv, wuk_ref[...])
    vt = lax.dot_general(wuvt_ref[...], ckv, (((1,), (1,)), ((), ())), preferred_element_type=F32)
    cos = cos_ref[...]
    sin = sin_ref[...]
    scale = QK_HEAD ** -0.5 * LOG2_E
    q_main, q_swap = gains_ref[0:1, :] * cos * scale, gains_ref[1:2, :] * sin * scale
    k_main = gains_ref[2:3, :] * cos
    pe_rot = pe_sw * (gains_ref[3:4, :] * sin)
    for hd in range(MLA_HEADS):
        lo = hd * HEAD_PAD
        qh = qq[:, lo:lo + HEAD_PAD]
        qs = qq[:, HEADS_W + lo:HEADS_W + lo + HEAD_PAD]
        r = lax.rsqrt(jnp.sum(qh * qh, axis=-1, keepdims=True) * (1.0 / QK_HEAD) + EPS)
        q_ref[:, lo:lo + HEAD_PAD] = ((qh * q_main + qs * q_swap) * r).astype(BF16)
        kh = kk[:, lo:lo + HEAD_PAD] + pe
        r = lax.rsqrt(jnp.sum(kh * kh, axis=-1, keepdims=True) * (1.0 / QK_HEAD) + EPS)
        k_ref[:, lo:lo + HEAD_PAD] = ((kh * k_main + pe_rot) * r).astype(BF16)
        vt_ref[hd] = vt[hd * V_HEAD:(hd + 1) * V_HEAD, :].astype(BF16)


def _pre(x, gain, w_in_r, q_lat, kv_lat, w_uq, w_uk, w_uvt, gains, cos, sin, layer, *, tm, tk):
    seq = x.shape[0]
    per = tk // tm
    row = pl.BlockSpec((tm, D_MODEL), lambda i: (i, 0))
    heads = pl.BlockSpec((tm, HEADS_W), lambda i: (i, 0))
    tab = pl.BlockSpec((tm, LANES), lambda i: (i, 0))
    vt_spec = pl.BlockSpec((MLA_HEADS, None, V_HEAD, tm), lambda i: (0, i // per, 0, i % per))
    return pl.pallas_call(
        _pre_kernel,
        grid=(seq // tm,),
        in_specs=[row] + [_layer_spec(a, layer) for a in (gain, w_in_r, q_lat, kv_lat, w_uq, w_uk, w_uvt, gains)]
        + [tab, tab],
        out_specs=[heads, heads, vt_spec, pl.BlockSpec((tm, POOL_WIDTH), lambda i: (i, 0))],
        out_shape=[jax.ShapeDtypeStruct((seq, HEADS_W), BF16)] * 2
        + [jax.ShapeDtypeStruct((MLA_HEADS, seq // tk, V_HEAD, tk), BF16),
           jax.ShapeDtypeStruct((seq, POOL_WIDTH), F32)],
        compiler_params=_params("parallel"),
        name="pre",
    )(x, gain, w_in_r, q_lat, kv_lat, w_uq, w_uk, w_uvt, gains, cos, sin)


def _fold_rows(p):
    return jnp.sum(p.reshape(p.shape[0] // SUBLANES, SUBLANES, p.shape[1]), axis=0)


def _store_heads(o_ref, acc, den):
    out_t = acc / jnp.sum(den, axis=0, keepdims=True)
    out_t = jnp.concatenate([out_t, jnp.zeros((HEAD_PAD - V_HEAD, acc.shape[1]), F32)], axis=0)
    o_ref[...] = out_t.T.astype(BF16)


def _attn_kernel(q_ref, k_ref, vt_ref, o_ref, s_ref, *, tk):
    tq = q_ref.shape[0]
    nk = k_ref.shape[0] // tk
    nsub = tk // KSUB
    qt = q_ref[...].astype(F32).T.astype(BF16)

    def scores_sub(j, c, slot, cmax):
        kc = k_ref[pl.ds(pl.multiple_of(j * tk + c * KSUB, KSUB), KSUB), :]
        st = _dot(kc, qt)
        s_ref[slot, c * KSUB:(c + 1) * KSUB, :] = st
        return jnp.maximum(cmax, jnp.max(st.reshape(KSUB // SUBLANES, SUBLANES, tq), axis=0))

    def step(j, slot, cmax, m, den, acc, j_next):
        m_new = jnp.maximum(m, jnp.max(cmax, axis=0, keepdims=True))
        alpha = jnp.exp2(m - m_new)
        den, acc = alpha * den, alpha * acc
        cmax_next = jnp.full((SUBLANES, tq), -jnp.inf, F32)
        for c in range(nsub):
            cmax_next = scores_sub(j_next, c, 1 - slot, cmax_next)
            p = jnp.exp2(s_ref[slot, c * KSUB:(c + 1) * KSUB, :] - m_new)
            den = den + _fold_rows(p)
            acc = acc + _dot(vt_ref[j, :, c * KSUB:(c + 1) * KSUB], p.astype(BF16))
        return cmax_next, m_new, den, acc

    def pair(jj, carry):
        j = 2 * jj
        carry = step(j, 0, *carry, j + 1)
        return step(j + 1, 1, *carry, jnp.minimum(j + 2, nk - 1))

    cmax0 = jnp.full((SUBLANES, tq), -jnp.inf, F32)
    for c in range(nsub):
        cmax0 = scores_sub(0, c, 0, cmax0)
    m0 = jnp.full((1, tq), -jnp.inf, F32)
    den0 = jnp.zeros((SUBLANES, tq), F32)
    acc0 = jnp.zeros((V_HEAD, tq), F32)
    _, _, den, acc = lax.fori_loop(0, nk // 2, pair, (cmax0, m0, den0, acc0))
    _store_heads(o_ref, acc, den)


def _attn_unshifted_kernel(q_ref, k_ref, vt_ref, o_ref, s_ref, *, tk):
    tq = q_ref.shape[0]
    nk = k_ref.shape[0] // tk
    nsub = tk // KSUB
    last = k_ref.shape[0] // KSUB - 1
    qt = q_ref[...].astype(F32).T.astype(BF16)

    nq = tq // QSPLIT

    def scores_sub(g, slot, h):
        kc = k_ref[pl.ds(pl.multiple_of(g * KSUB, KSUB), KSUB), :]
        s_ref[slot, :, h * nq:(h + 1) * nq] = _dot(kc, qt[:, h * nq:(h + 1) * nq])

    per = min(CHUNKS_PER_TRIP, nk)

    def trip(jj, carry):
        dens, accs = list(carry[0]), list(carry[1])
        for u in range(per * nsub):
            j, c = jj * per + u // nsub, u % nsub
            g = j * nsub + c
            for h in range(QSPLIT):
                scores_sub(jnp.minimum(g + AHEAD, last), (u + AHEAD) % RING, h)
                p = jnp.exp2(s_ref[u % RING, :, h * nq:(h + 1) * nq])
                dens[h] = dens[h] + _fold_rows(p)
                accs[h] = accs[h] + _dot(vt_ref[j, :, c * KSUB:(c + 1) * KSUB], p.astype(BF16))
        return tuple(dens), tuple(accs)

    for g in range(AHEAD):
        for h in range(QSPLIT):
            scores_sub(g, g % RING, h)
    dens, accs = lax.fori_loop(0, nk // per, trip,
                               (tuple(jnp.zeros((SUBLANES, nq), F32) for _ in range(QSPLIT)),
                                tuple(jnp.zeros((V_HEAD, nq), F32) for _ in range(QSPLIT))))
    _store_heads(o_ref, jnp.concatenate(accs, axis=1), jnp.concatenate(dens, axis=1))


def _attn(q, k, vt, *, tq, tk, shifted):
    seq = q.shape[0]
    qspec = pl.BlockSpec((tq, HEAD_PAD), lambda h, i: (i, h))
    kspec = pl.BlockSpec((seq, HEAD_PAD), lambda h, i: (0, h))
    vspec = pl.BlockSpec((None, seq // tk, V_HEAD, tk), lambda h, i: (h, 0, 0, 0))
    body = _attn_kernel if shifted else _attn_unshifted_kernel
    return pl.pallas_call(
        functools.partial(body, tk=tk),
        grid=(MLA_HEADS, seq // tq),
        in_specs=[qspec, kspec, vspec],
        out_specs=qspec,
        out_shape=jax.ShapeDtypeStruct((seq, HEADS_W), BF16),
        scratch_shapes=[pltpu.VMEM((2, tk, tq) if shifted else (RING, KSUB, tq), F32)],
        compiler_params=_params("parallel", "parallel"),
        name="attn" if shifted else "attn_unshifted",
    )(q, k, vt)


def _score_bound(q_gain, k_gain):
    return (LOG2_E * QK_HEAD ** 0.5) * jnp.max(jnp.abs(q_gain)) * jnp.max(jnp.abs(k_gain)) * BF16_NORM_SLACK


def _post_kernel(x_ref, o_ref, zc_ref, zprev_ref, znext_ref, wpool_ref, ps_ref, woa_ref, wob_ref,
                 out_ref, ext_ref, *, tm, seq):
    i = pl.program_id(0)
    last = pl.num_programs(0) - 1
    zc = zc_ref[...]
    ext_ref[0:POOL_HALO, :] = jnp.where(i > 0, zprev_ref[...], 0.0)
    ext_ref[POOL_HALO:POOL_HALO + tm, :] = zc
    ext_ref[POOL_HALO + tm:2 * POOL_HALO + tm, :] = jnp.where(i < last, znext_ref[...], 0.0)
    pos = i * tm + lax.broadcasted_iota(jnp.int32, (tm, 1), 0)
    parts = []
    for g, w in enumerate(POOL_WINDOWS):
        left = w // 2
        right = w - 1 - left
        c0 = g * POOL_GROUP_DIM
        wsum = ext_ref[POOL_HALO - left:POOL_HALO - left + tm, c0:c0 + POOL_GROUP_DIM]
        for d in range(-left + 1, right + 1):
            wsum = wsum + ext_ref[POOL_HALO + d:POOL_HALO + d + tm, c0:c0 + POOL_GROUP_DIM]
        cnt = (jnp.minimum(pos + right + 1, seq) - jnp.maximum(pos - left, 0)).astype(F32)
        mixed = (wsum / cnt - zc[:, c0:c0 + POOL_GROUP_DIM]).astype(BF16)
        y = _dot(mixed, wpool_ref[g]) * ps_ref[:, c0:c0 + POOL_GROUP_DIM]
        parts.append(y.astype(BF16))
    b = jnp.concatenate(parts, axis=1)
    out_ref[...] = x_ref[...] + (_dot(o_ref[...], woa_ref[...]) + _dot(b, wob_ref[...]))


def _post(x, o, zp, w_pool, pool_scale, w_out_a, w_out_b, layer, *, tm):
    seq = x.shape[0]
    per = tm // POOL_HALO
    nblk = seq // POOL_HALO
    row = pl.BlockSpec((tm, D_MODEL), lambda i: (i, 0))
    return pl.pallas_call(
        functools.partial(_post_kernel, tm=tm, seq=seq),
        grid=(seq // tm,),
        in_specs=[row,
                  pl.BlockSpec((tm, HEADS_W), lambda i: (i, 0)),
                  pl.BlockSpec((tm, POOL_WIDTH), lambda i: (i, 0)),
                  pl.BlockSpec((POOL_HALO, POOL_WIDTH), lambda i: (jnp.maximum(i * per - 1, 0), 0)),
                  pl.BlockSpec((POOL_HALO, POOL_WIDTH), lambda i: (jnp.minimum((i + 1) * per, nblk - 1), 0)),
                  ] + [_layer_spec(a, layer) for a in (w_pool, pool_scale, w_out_a, w_out_b)],
        out_specs=row,
        out_shape=jax.ShapeDtypeStruct(x.shape, F32),
        scratch_shapes=[pltpu.VMEM((tm + 2 * POOL_HALO, POOL_WIDTH), F32)],
        compiler_params=_params("parallel"),
        name="post",
    )(x, o, zp, zp, zp, w_pool, pool_scale, w_out_a, w_out_b)


def _swap_halves(t):
    return jnp.concatenate([t[..., HALF_ROPE:], t[..., :HALF_ROPE]], axis=-1)


def _head_block(nope, rope):
    pad = jnp.zeros(nope.shape[:-1] + (HEAD_PAD - QK_HEAD,), nope.dtype)
    return jnp.concatenate([nope, rope, pad], axis=-1)


def _layout_weights(w_in, w_uq, w_uk, w_uv, q_norm, k_norm, w_out):
    depth = w_in.shape[0]
    o_pe = Q_LORA + KV_LORA
    o_pool = o_pe + QK_ROPE
    w_pe = w_in[..., o_pe:o_pool]
    zero_nope = jnp.zeros(w_pe.shape[:-1] + (QK_NOPE,), w_pe.dtype)
    w_in_r = jnp.concatenate(
        [w_in[..., :o_pe], w_in[..., o_pool:], _head_block(zero_nope, w_pe),
         _head_block(zero_nope, _swap_halves(w_pe))], axis=-1).astype(BF16)

    uq = w_uq.reshape(depth, Q_LORA, MLA_HEADS, QK_HEAD)
    uq_main = _head_block(uq[..., :QK_NOPE], uq[..., QK_NOPE:])
    uq_swap = _head_block(jnp.zeros_like(uq[..., :QK_NOPE]), _swap_halves(uq[..., QK_NOPE:]))
    w_uq_r = jnp.concatenate([uq_main.reshape(depth, Q_LORA, HEADS_W),
                              uq_swap.reshape(depth, Q_LORA, HEADS_W)], axis=-1).astype(BF16)

    uk = jnp.pad(w_uk.reshape(depth, KV_LORA, MLA_HEADS, QK_NOPE),
                 ((0, 0), (0, 0), (0, 0), (0, HEAD_PAD - QK_NOPE)))
    w_uk_r = uk.reshape(depth, KV_LORA, HEADS_W).astype(BF16)
    uvt = jnp.transpose(w_uv.reshape(depth, KV_LORA, MLA_HEADS, V_HEAD), (0, 2, 3, 1))
    w_uvt_r = uvt.reshape(depth, MLA_HEADS * V_HEAD, KV_LORA).astype(BF16)

    def gain_rows(g):
        main = _head_block(g[..., :QK_NOPE], g[..., QK_NOPE:])
        swap = _head_block(jnp.zeros_like(g[..., :QK_NOPE]), _swap_halves(g[..., QK_NOPE:]))
        return main, swap

    gq, gq_sw = gain_rows(q_norm)
    gk, gk_sw = gain_rows(k_norm)
    zero = jnp.zeros_like(gq)
    gains = jnp.stack([gq, gq_sw, gk, gk_sw, zero, zero, zero, zero], axis=1)

    oa = w_out[:, :MLA_WIDTH].reshape(depth, MLA_HEADS, V_HEAD, D_MODEL)
    oa = jnp.pad(oa, ((0, 0), (0, 0), (0, HEAD_PAD - V_HEAD), (0, 0))).reshape(depth, HEADS_W, D_MODEL)
    return w_in_r, w_uq_r, w_uk_r, w_uvt_r, gains, oa.astype(BF16), w_out[:, MLA_WIDTH:].astype(BF16)


def _rope_tables(seq):
    pos = jnp.arange(seq, dtype=F32)
    inv = ROPE_THETA ** (-jnp.arange(0, QK_ROPE, 2, dtype=F32) / QK_ROPE)
    ang = pos[:, None] * inv[None, :]
    cos, sin = jnp.cos(ang), jnp.sin(ang)
    ones = jnp.ones((seq, QK_NOPE), F32)
    zeros = jnp.zeros((seq, QK_NOPE), F32)
    return (_head_block(ones, jnp.concatenate([cos, cos], axis=-1)),
            _head_block(zeros, jnp.concatenate([-sin, sin], axis=-1)))


def _tiles(seq):
    tm = min(1024, seq)
    return dict(tm=tm, tm_pre=tm, tq=min(1024, seq), tk=min(2048, seq // 2), tf=4 * MXU_TILE)


def kernel(x, ffn1_norm, ffn1_w_gu, ffn1_w_down, mix_norm, w_in, q_lat_norm, kv_lat_norm, w_uq, w_uk, w_uv,
           q_norm, k_norm, w_pool, pool_scale, w_out, ffn2_norm, ffn2_w_gu, ffn2_w_down):
    batch, seq, _ = x.shape
    depth = w_in.shape[0]
    t = _tiles(seq)
    cos, sin = _rope_tables(seq)
    w_in_r, w_uq_r, w_uk_r, w_uvt_r, gains, w_out_a, w_out_b = _layout_weights(
        w_in, w_uq, w_uk, w_uv, q_norm, k_norm, w_out)
    f1_gu, f1_d = ffn1_w_gu.astype(BF16), ffn1_w_down.astype(BF16)
    f2_gu, f2_d = ffn2_w_gu.astype(BF16), ffn2_w_down.astype(BF16)
    w_pool_b = w_pool.astype(BF16)
    ffn1_g, ffn2_g, mix_g = ffn1_norm[:, None], ffn2_norm[:, None], mix_norm[:, None]
    q_lat_g, kv_lat_g, pool_g = q_lat_norm[:, None], kv_lat_norm[:, None], pool_scale[:, None]

    outs = []
    for b in range(batch):
        xb = x[b]
        for l in range(depth):
            xb = _ffn(xb, ffn1_g, f1_gu, f1_d, l, tm=t["tm"], tf=t["tf"])
            q, k, vt, zp = _pre(xb, mix_g, w_in_r, q_lat_g, kv_lat_g, w_uq_r, w_uk_r, w_uvt_r, gains, cos, sin, l,
                                tm=t["tm_pre"], tk=t["tk"])
            o = lax.cond(_score_bound(q_norm[l], k_norm[l]) <= MAX_UNSHIFTED_SCORE,
                         functools.partial(_attn, tq=t["tq"], tk=t["tk"], shifted=False),
                         functools.partial(_attn, tq=t["tq"], tk=t["tk"], shifted=True),
                         q, k, vt)
            xb = _post(xb, o, zp, w_pool_b, pool_g, w_out_a, w_out_b, l, tm=t["tm"])
            xb = _ffn(xb, ffn2_g, f2_gu, f2_d, l, tm=t["tm"], tf=t["tf"])
        outs.append(xb)
    return jnp.stack(outs, axis=0)
```

```python
import functools

import jax
import jax.numpy as jnp
from jax import lax
from jax.experimental import pallas as pl
from jax.experimental.pallas import tpu as pltpu

D_MODEL = 1024
MLA_HEADS = 8
QK_NOPE = 64
QK_ROPE = 32
QK_HEAD = QK_NOPE + QK_ROPE
V_HEAD = 64
Q_LORA = 384
KV_LORA = 256
MLA_WIDTH = MLA_HEADS * V_HEAD
ROPE_THETA = 10000.0
POOL_WINDOWS = (2, 4, 8, 16)
POOL_GROUP_DIM = 128
POOL_WIDTH = 512
D_FF = 2816
EPS = 1e-6
LOG2_E = 1.4426950408889634
MAX_UNSHIFTED_SCORE = 60.0
BF16_NORM_SLACK = 1.01

LANES = 128
SUBLANES = 8
HEAD_PAD = LANES
HEADS_W = MLA_HEADS * HEAD_PAD
HALF_ROPE = QK_ROPE // 2
POOL_HALO = 8
MXU_TILE = 256
KSUB = MXU_TILE
CHUNKS_PER_TRIP = 8
QSPLIT = 2
AHEAD = 1
RING = 2 * AHEAD
C_Q, C_KV, C_POOL = 0, Q_LORA, Q_LORA + KV_LORA
C_PEA = C_POOL + POOL_WIDTH
C_PEB = C_PEA + LANES
D_IN_R = C_PEB + LANES

VMEM_LIMIT = 56 * 1024 * 1024

F32 = jnp.float32
BF16 = jnp.bfloat16


def _rmsnorm(x, g):
    ms = jnp.mean(x * x, axis=-1, keepdims=True)
    return x * lax.rsqrt(ms + EPS) * g


def _dot(a, b):
    return jnp.dot(a, b, preferred_element_type=F32)


def _layer_spec(stacked, layer):
    tail = stacked.shape[1:]
    return pl.BlockSpec((None,) + tail, lambda *_: (layer,) + (0,) * len(tail), pipeline_mode=pl.Buffered(1))


def _params(*sem):
    return pltpu.CompilerParams(dimension_semantics=sem, vmem_limit_bytes=VMEM_LIMIT)


def _ffn_kernel(x_ref, g_ref, wgu_ref, wd_ref, o_ref, *, tf):
    x = x_ref[...]
    h = _rmsnorm(x, g_ref[...]).astype(BF16)
    acc = jnp.zeros(x.shape, F32)
    for lo in range(0, D_FF, tf):
        hi = min(lo + tf, D_FF)
        gate = _dot(h, wgu_ref[:, lo:hi])
        up = _dot(h, wgu_ref[:, D_FF + lo:D_FF + hi])
        act = (gate * jax.nn.sigmoid(gate) * up).astype(BF16)
        acc = acc + _dot(act, wd_ref[lo:hi, :])
    o_ref[...] = x + 0.5 * acc


def _ffn(x, gain, w_gu, w_down, layer, *, tm, tf):
    seq = x.shape[0]
    row = pl.BlockSpec((tm, D_MODEL), lambda i: (i, 0))
    return pl.pallas_call(
        functools.partial(_ffn_kernel, tf=tf),
        grid=(seq // tm,),
        in_specs=[row] + [_layer_spec(a, layer) for a in (gain, w_gu, w_down)],
        out_specs=row,
        out_shape=jax.ShapeDtypeStruct(x.shape, F32),
        compiler_params=_params("parallel"),
        name="ffn",
    )(x, gain, w_gu, w_down)


def _pre_kernel(x_ref, g_ref, win_ref, qlat_ref, kvlat_ref, wuq_ref, wuk_ref, wuvt_ref, gains_ref,
                cos_ref, sin_ref, q_ref, k_ref, vt_ref, zp_ref):
    h = _rmsnorm(x_ref[...], g_ref[...]).astype(BF16)
    z = _dot(h, win_ref[...])
    cq = _rmsnorm(z[:, C_Q:C_KV], qlat_ref[...]).astype(BF16)
    ckv = _rmsnorm(z[:, C_KV:C_POOL], kvlat_ref[...]).astype(BF16)
    zp_ref[...] = z[:, C_POOL:C_PEA]
    pe = z[:, C_PEA:C_PEB]
    pe_sw = z[:, C_PEB:D_IN_R]
    qq = _dot(cq, wuq_ref[...])
    kk = _dot(ckv, wuk_ref[...])
    vt = lax.dot_general(wuvt_ref[...], ckv, (((1,), (1,)), ((), ())), preferred_element_type=F32)
    cos = cos_ref[...]
    sin = sin_ref[...]
    scale = QK_HEAD ** -0.5 * LOG2_E
    q_main, q_swap = gains_ref[0:1, :] * cos * scale, gains_ref[1:2, :] * sin * scale
    k_main = gains_ref[2:3, :] * cos
    pe_rot = pe_sw * (gains_ref[3:4, :] * sin)
    for hd in range(MLA_HEADS):
        lo = hd * HEAD_PAD
        qh = qq[:, lo:lo + HEAD_PAD]
        qs = qq[:, HEADS_W + lo:HEADS_W + lo + HEAD_PAD]
        r = lax.rsqrt(jnp.sum(qh * qh, axis=-1, keepdims=True) * (1.0 / QK_HEAD) + EPS)
        q_ref[:, lo:lo + HEAD_PAD] = ((qh * q_main + qs * q_swap) * r).astype(BF16)
        kh = kk[:, lo:lo + HEAD_PAD] + pe
        r = lax.rsqrt(jnp.sum(kh * kh, axis=-1, keepdims=True) * (1.0 / QK_HEAD) + EPS)
        k_ref[:, lo:lo + HEAD_PAD] = ((kh * k_main + pe_rot) * r).astype(BF16)
        vt_ref[hd] = vt[hd * V_HEAD:(hd + 1) * V_HEAD, :].astype(BF16)


def _pre(x, gain, w_in_r, q_lat, kv_lat, w_uq, w_uk, w_uvt, gains, cos, sin, layer, *, tm, tk):
    seq = x.shape[0]
    per = tk // tm
    row = pl.BlockSpec((tm, D_MODEL), lambda i: (i, 0))
    heads = pl.BlockSpec((tm, HEADS_W), lambda i: (i, 0))
    tab = pl.BlockSpec((tm, LANES), lambda i: (i, 0))
    vt_spec = pl.BlockSpec((MLA_HEADS, None, V_HEAD, tm), lambda i: (0, i // per, 0, i % per))
    return pl.pallas_call(
        _pre_kernel,
        grid=(seq // tm,),
        in_specs=[row] + [_layer_spec(a, layer) for a in (gain, w_in_r, q_lat, kv_lat, w_uq, w_uk, w_uvt, gains)]
        + [tab, tab],
        out_specs=[heads, heads, vt_spec, pl.BlockSpec((tm, POOL_WIDTH), lambda i: (i, 0))],
        out_shape=[jax.ShapeDtypeStruct((seq, HEADS_W), BF16)] * 2
        + [jax.ShapeDtypeStruct((MLA_HEADS, seq // tk, V_HEAD, tk), BF16),
           jax.ShapeDtypeStruct((seq, POOL_WIDTH), F32)],
        compiler_params=_params("parallel"),
        name="pre",
    )(x, gain, w_in_r, q_lat, kv_lat, w_uq, w_uk, w_uvt, gains, cos, sin)


def _fold_rows(p):
    return jnp.sum(p.reshape(p.shape[0] // SUBLANES, SUBLANES, p.shape[1]), axis=0)


def _store_heads(o_ref, acc, den):
    o_ref[...] = (acc / jnp.sum(den, axis=0, keepdims=True)).astype(BF16)


def _attn_kernel(q_ref, k_ref, vt_ref, o_ref, s_ref, *, tk):
    tq = q_ref.shape[0]
    nk = k_ref.shape[0] // tk
    nsub = tk // KSUB
    qt = q_ref[...].astype(F32).T.astype(BF16)

    def scores_sub(j, c, slot, cmax):
        kc = k_ref[pl.ds(pl.multiple_of(j * tk + c * KSUB, KSUB), KSUB), :]
        st = _dot(kc, qt)
        s_ref[slot, c * KSUB:(c + 1) * KSUB, :] = st
        return jnp.maximum(cmax, jnp.max(st.reshape(KSUB // SUBLANES, SUBLANES, tq), axis=0))

    def step(j, slot, cmax, m, den, acc, j_next):
        m_new = jnp.maximum(m, jnp.max(cmax, axis=0, keepdims=True))
        alpha = jnp.exp2(m - m_new)
        den, acc = alpha * den, alpha * acc
        cmax_next = jnp.full((SUBLANES, tq), -jnp.inf, F32)
        for c in range(nsub):
            cmax_next = scores_sub(j_next, c, 1 - slot, cmax_next)
            p = jnp.exp2(s_ref[slot, c * KSUB:(c + 1) * KSUB, :] - m_new)
            den = den + _fold_rows(p)
            acc = acc + _dot(vt_ref[j, :, c * KSUB:(c + 1) * KSUB], p.astype(BF16))
        return cmax_next, m_new, den, acc

    def pair(jj, carry):
        j = 2 * jj
        carry = step(j, 0, *carry, j + 1)
        return step(j + 1, 1, *carry, jnp.minimum(j + 2, nk - 1))

    cmax0 = jnp.full((SUBLANES, tq), -jnp.inf, F32)
    for c in range(nsub):
        cmax0 = scores_sub(0, c, 0, cmax0)
    m0 = jnp.full((1, tq), -jnp.inf, F32)
    den0 = jnp.zeros((SUBLANES, tq), F32)
    acc0 = jnp.zeros((V_HEAD, tq), F32)
    _, _, den, acc = lax.fori_loop(0, nk // 2, pair, (cmax0, m0, den0, acc0))
    _store_heads(o_ref, acc, den)


def _attn_unshifted_kernel(q_ref, k_ref, vt_ref, o_ref, s_ref, *, tk):
    tq = q_ref.shape[0]
    nk = k_ref.shape[0] // tk
    nsub = tk // KSUB
    last = k_ref.shape[0] // KSUB - 1
    qt = q_ref[...].astype(F32).T.astype(BF16)

    nq = tq // QSPLIT

    def scores_sub(g, slot, h):
        kc = k_ref[pl.ds(pl.multiple_of(g * KSUB, KSUB), KSUB), :]
        s_ref[slot, :, h * nq:(h + 1) * nq] = _dot(kc, qt[:, h * nq:(h + 1) * nq])

    per = min(CHUNKS_PER_TRIP, nk)

    def trip(jj, carry):
        dens, accs = list(carry[0]), list(carry[1])
        for u in range(per * nsub):
            j, c = jj * per + u // nsub, u % nsub
            g = j * nsub + c
            for h in range(QSPLIT):
                scores_sub(jnp.minimum(g + AHEAD, last), (u + AHEAD) % RING, h)
                p = jnp.exp2(s_ref[u % RING, :, h * nq:(h + 1) * nq])
                dens[h] = dens[h] + _fold_rows(p)
                accs[h] = accs[h] + _dot(vt_ref[j, :, c * KSUB:(c + 1) * KSUB], p.astype(BF16))
        return tuple(dens), tuple(accs)

    for g in range(AHEAD):
        for h in range(QSPLIT):
            scores_sub(g, g % RING, h)
    dens, accs = lax.fori_loop(0, nk // per, trip,
                               (tuple(jnp.zeros((SUBLANES, nq), F32) for _ in range(QSPLIT)),
                                tuple(jnp.zeros((V_HEAD, nq), F32) for _ in range(QSPLIT))))
    _store_heads(o_ref, jnp.concatenate(accs, axis=1), jnp.concatenate(dens, axis=1))


def _attn(q, k, vt, *, tq, tk, shifted):
    seq = q.shape[0]
    qspec = pl.BlockSpec((tq, HEAD_PAD), lambda h, i: (i, h))
    kspec = pl.BlockSpec((seq, HEAD_PAD), lambda h, i: (0, h))
    vspec = pl.BlockSpec((None, seq // tk, V_HEAD, tk), lambda h, i: (h, 0, 0, 0))
    body = _attn_kernel if shifted else _attn_unshifted_kernel
    return pl.pallas_call(
        functools.partial(body, tk=tk),
        grid=(MLA_HEADS, seq // tq),
        in_specs=[qspec, kspec, vspec],
        out_specs=pl.BlockSpec((V_HEAD, tq), lambda h, i: (h, i)),
        out_shape=jax.ShapeDtypeStruct((MLA_WIDTH, seq), BF16),
        scratch_shapes=[pltpu.VMEM((2, tk, tq) if shifted else (RING, KSUB, tq), F32)],
        compiler_params=_params("parallel", "parallel"),
        name="attn" if shifted else "attn_unshifted",
    )(q, k, vt)


def _score_bound(q_gain, k_gain):
    return (LOG2_E * QK_HEAD ** 0.5) * jnp.max(jnp.abs(q_gain)) * jnp.max(jnp.abs(k_gain)) * BF16_NORM_SLACK


def _post_kernel(x_ref, o_ref, zc_ref, zprev_ref, znext_ref, wpool_ref, ps_ref, woa_ref, wob_ref,
                 out_ref, ext_ref, *, tm, seq):
    i = pl.program_id(0)
    last = pl.num_programs(0) - 1
    zc = zc_ref[...]
    ext_ref[0:POOL_HALO, :] = jnp.where(i > 0, zprev_ref[...], 0.0)
    ext_ref[POOL_HALO:POOL_HALO + tm, :] = zc
    ext_ref[POOL_HALO + tm:2 * POOL_HALO + tm, :] = jnp.where(i < last, znext_ref[...], 0.0)
    pos = i * tm + lax.broadcasted_iota(jnp.int32, (tm, 1), 0)
    parts = []
    for g, w in enumerate(POOL_WINDOWS):
        left = w // 2
        right = w - 1 - left
        c0 = g * POOL_GROUP_DIM
        wsum = ext_ref[POOL_HALO - left:POOL_HALO - left + tm, c0:c0 + POOL_GROUP_DIM]
        for d in range(-left + 1, right + 1):
            wsum = wsum + ext_ref[POOL_HALO + d:POOL_HALO + d + tm, c0:c0 + POOL_GROUP_DIM]
        cnt = (jnp.minimum(pos + right + 1, seq) - jnp.maximum(pos - left, 0)).astype(F32)
        mixed = (wsum / cnt - zc[:, c0:c0 + POOL_GROUP_DIM]).astype(BF16)
        y = _dot(mixed, wpool_ref[g]) * ps_ref[:, c0:c0 + POOL_GROUP_DIM]
        parts.append(y.astype(BF16))
    b = jnp.concatenate(parts, axis=1)
    attn_part = lax.dot_general(o_ref[...], woa_ref[...], (((0,), (0,)), ((), ())), preferred_element_type=F32)
    out_ref[...] = x_ref[...] + (attn_part + _dot(b, wob_ref[...]))


def _post(x, o, zp, w_pool, pool_scale, w_out_a, w_out_b, layer, *, tm):
    seq = x.shape[0]
    per = tm // POOL_HALO
    nblk = seq // POOL_HALO
    row = pl.BlockSpec((tm, D_MODEL), lambda i: (i, 0))
    return pl.pallas_call(
        functools.partial(_post_kernel, tm=tm, seq=seq),
        grid=(seq // tm,),
        in_specs=[row,
                  pl.BlockSpec((MLA_WIDTH, tm), lambda i: (0, i)),
                  pl.BlockSpec((tm, POOL_WIDTH), lambda i: (i, 0)),
                  pl.BlockSpec((POOL_HALO, POOL_WIDTH), lambda i: (jnp.maximum(i * per - 1, 0), 0)),
                  pl.BlockSpec((POOL_HALO, POOL_WIDTH), lambda i: (jnp.minimum((i + 1) * per, nblk - 1), 0)),
                  ] + [_layer_spec(a, layer) for a in (w_pool, pool_scale, w_out_a, w_out_b)],
        out_specs=row,
        out_shape=jax.ShapeDtypeStruct(x.shape, F32),
        scratch_shapes=[pltpu.VMEM((tm + 2 * POOL_HALO, POOL_WIDTH), F32)],
        compiler_params=_params("parallel"),
        name="post",
    )(x, o, zp, zp, zp, w_pool, pool_scale, w_out_a, w_out_b)


def _swap_halves(t):
    return jnp.concatenate([t[..., HALF_ROPE:], t[..., :HALF_ROPE]], axis=-1)


def _head_block(nope, rope):
    pad = jnp.zeros(nope.shape[:-1] + (HEAD_PAD - QK_HEAD,), nope.dtype)
    return jnp.concatenate([nope, rope, pad], axis=-1)


def _layout_weights(w_in, w_uq, w_uk, w_uv, q_norm, k_norm, w_out):
    depth = w_in.shape[0]
    w_in, w_uq, w_uk, w_uv, w_out = (w.astype(BF16) for w in (w_in, w_uq, w_uk, w_uv, w_out))
    o_pe = Q_LORA + KV_LORA
    o_pool = o_pe + QK_ROPE
    w_pe = w_in[..., o_pe:o_pool]
    zero_nope = jnp.zeros(w_pe.shape[:-1] + (QK_NOPE,), w_pe.dtype)
    w_in_r = jnp.concatenate(
        [w_in[..., :o_pe], w_in[..., o_pool:], _head_block(zero_nope, w_pe),
         _head_block(zero_nope, _swap_halves(w_pe))], axis=-1).astype(BF16)

    uq = w_uq.reshape(depth, Q_LORA, MLA_HEADS, QK_HEAD)
    uq_main = _head_block(uq[..., :QK_NOPE], uq[..., QK_NOPE:])
    uq_swap = _head_block(jnp.zeros_like(uq[..., :QK_NOPE]), _swap_halves(uq[..., QK_NOPE:]))
    w_uq_r = jnp.concatenate([uq_main.reshape(depth, Q_LORA, HEADS_W),
                              uq_swap.reshape(depth, Q_LORA, HEADS_W)], axis=-1).astype(BF16)

    uk = jnp.pad(w_uk.reshape(depth, KV_LORA, MLA_HEADS, QK_NOPE),
                 ((0, 0), (0, 0), (0, 0), (0, HEAD_PAD - QK_NOPE)))
    w_uk_r = uk.reshape(depth, KV_LORA, HEADS_W).astype(BF16)
    uvt = jnp.transpose(w_uv.reshape(depth, KV_LORA, MLA_HEADS, V_HEAD), (0, 2, 3, 1))
    w_uvt_r = uvt.reshape(depth, MLA_HEADS * V_HEAD, KV_LORA).astype(BF16)

    def gain_rows(g):
        main = _head_block(g[..., :QK_NOPE], g[..., QK_NOPE:])
        swap = _head_block(jnp.zeros_like(g[..., :QK_NOPE]), _swap_halves(g[..., QK_NOPE:]))
        return main, swap

    gq, gq_sw = gain_rows(q_norm)
    gk, gk_sw = gain_rows(k_norm)
    zero = jnp.zeros_like(gq)
    gains = jnp.stack([gq, gq_sw, gk, gk_sw, zero, zero, zero, zero], axis=1)

    return w_in_r, w_uq_r, w_uk_r, w_uvt_r, gains, w_out[:, :MLA_WIDTH], w_out[:, MLA_WIDTH:]


def _rope_tables(seq):
    inv = ROPE_THETA ** (-jnp.arange(0, QK_ROPE, 2, dtype=F32) / QK_ROPE)
    off = jnp.zeros((QK_NOPE,), F32)
    inv_lane = _head_block(off, jnp.concatenate([inv, inv]))
    sign = _head_block(off, jnp.concatenate([-jnp.ones_like(inv), jnp.ones_like(inv)]))
    ang = jnp.arange(seq, dtype=F32)[:, None] * inv_lane[None, :]
    return jnp.cos(ang), jnp.sin(ang) * sign


def _tiles(seq):
    tm = min(1024, seq)
    return dict(tm=tm, tm_pre=tm, tq=min(1024, seq), tk=min(2048, seq // 2), tf=4 * MXU_TILE)


def kernel(x, ffn1_norm, ffn1_w_gu, ffn1_w_down, mix_norm, w_in, q_lat_norm, kv_lat_norm, w_uq, w_uk, w_uv,
           q_norm, k_norm, w_pool, pool_scale, w_out, ffn2_norm, ffn2_w_gu, ffn2_w_down):
    batch, seq, _ = x.shape
    depth = w_in.shape[0]
    t = _tiles(seq)
    cos, sin = _rope_tables(seq)
    w_in_r, w_uq_r, w_uk_r, w_uvt_r, gains, w_out_a, w_out_b = _layout_weights(
        w_in, w_uq, w_uk, w_uv, q_norm, k_norm, w_out)
    f1_gu, f1_d = ffn1_w_gu.astype(BF16), ffn1_w_down.astype(BF16)
    f2_gu, f2_d = ffn2_w_gu.astype(BF16), ffn2_w_down.astype(BF16)
    w_pool_b = w_pool.astype(BF16)
    ffn1_g, ffn2_g, mix_g = ffn1_norm[:, None], ffn2_norm[:, None], mix_norm[:, None]
    q_lat_g, kv_lat_g, pool_g = q_lat_norm[:, None], kv_lat_norm[:, None], pool_scale[:, None]

    outs = []
    for b in range(batch):
        xb = x[b]
        for l in range(depth):
            xb = _ffn(xb, ffn1_g, f1_gu, f1_d, l, tm=t["tm"], tf=t["tf"])
            q, k, vt, zp = _pre(xb, mix_g, w_in_r, q_lat_g, kv_lat_g, w_uq_r, w_uk_r, w_uvt_r, gains, cos, sin, l,
                                tm=t["tm_pre"], tk=t["tk"])
            o = lax.cond(_score_bound(q_norm[l], k_norm[l]) <= MAX_UNSHIFTED_SCORE,
                         functools.partial(_attn, tq=t["tq"], tk=t["tk"], shifted=False),
                         functools.partial(_attn, tq=t["tq"], tk=t["tk"], shifted=True),
                         q, k, vt)
            xb = _post(xb, o, zp, w_pool_b, pool_g, w_out_a, w_out_b, l, tm=t["tm"])
            xb = _ffn(xb, ffn2_g, f2_gu, f2_d, l, tm=t["tm"], tf=t["tf"])
        outs.append(xb)
    return jnp.stack(outs, axis=0)
```

```python
import functools

import jax
import jax.numpy as jnp
from jax import lax
from jax.experimental import pallas as pl
from jax.experimental.pallas import tpu as pltpu

D_MODEL = 1024
MLA_HEADS = 8
QK_NOPE = 64
QK_ROPE = 32
QK_HEAD = QK_NOPE + QK_ROPE
V_HEAD = 64
Q_LORA = 384
KV_LORA = 256
MLA_WIDTH = MLA_HEADS * V_HEAD
ROPE_THETA = 10000.0
POOL_WINDOWS = (2, 4, 8, 16)
POOL_GROUP_DIM = 128
POOL_WIDTH = 512
D_FF = 2816
EPS = 1e-6
LOG2_E = 1.4426950408889634
MAX_UNSHIFTED_SCORE = 60.0
BF16_NORM_SLACK = 1.01

LANES = 128
SUBLANES = 8
HEAD_PAD = LANES
HEADS_W = MLA_HEADS * HEAD_PAD
HALF_ROPE = QK_ROPE // 2
POOL_HALO = 8
MXU_TILE = 256
KSUB = MXU_TILE
CHUNKS_PER_TRIP = 8
QSPLIT = 2
AHEAD = 1
RING = 2 * AHEAD
C_Q, C_KV, C_POOL = 0, Q_LORA, Q_LORA + KV_LORA
C_PEA = C_POOL + POOL_WIDTH
C_PEB = C_PEA + LANES
D_IN_R = C_PEB + LANES

VMEM_LIMIT = 56 * 1024 * 1024

F32 = jnp.float32
BF16 = jnp.bfloat16


def _rmsnorm(x, g):
    ms = jnp.mean(x * x, axis=-1, keepdims=True)
    return x * lax.rsqrt(ms + EPS) * g


def _dot(a, b):
    return jnp.dot(a, b, preferred_element_type=F32)


def _layer_spec(stacked, layer):
    tail = stacked.shape[1:]
    return pl.BlockSpec((None,) + tail, lambda *_: (layer,) + (0,) * len(tail), pipeline_mode=pl.Buffered(1))


def _params(*sem):
    return pltpu.CompilerParams(dimension_semantics=sem, vmem_limit_bytes=VMEM_LIMIT)


def _ffn_kernel(x_ref, g_ref, wgu_ref, wd_ref, o_ref, *, tf):
    x = x_ref[...]
    h = _rmsnorm(x, g_ref[...]).astype(BF16)
    acc = jnp.zeros(x.shape, F32)
    for lo in range(0, D_FF, tf):
        hi = min(lo + tf, D_FF)
        gate = _dot(h, wgu_ref[:, lo:hi])
        up = _dot(h, wgu_ref[:, D_FF + lo:D_FF + hi])
        act = (gate * jax.nn.sigmoid(gate) * up).astype(BF16)
        acc = acc + _dot(act, wd_ref[lo:hi, :])
    o_ref[...] = x + 0.5 * acc


def _ffn(x, gain, w_gu, w_down, layer, *, tm, tf):
    seq = x.shape[0]
    assert seq % tm == 0 and tf % MXU_TILE == 0
    row = pl.BlockSpec((tm, D_MODEL), lambda i: (i, 0))
    return pl.pallas_call(
        functools.partial(_ffn_kernel, tf=tf),
        grid=(seq // tm,),
        in_specs=[row] + [_layer_spec(a, layer) for a in (gain, w_gu, w_down)],
        out_specs=row,
        out_shape=jax.ShapeDtypeStruct(x.shape, F32),
        compiler_params=_params("parallel"),
        name="ffn",
    )(x, gain, w_gu, w_down)


def _pre_kernel(x_ref, g_ref, win_ref, qlat_ref, kvlat_ref, wuq_ref, wuk_ref, wuvt_ref, gains_ref,
                cos_ref, sin_ref, q_ref, k_ref, vt_ref, zp_ref):
    h = _rmsnorm(x_ref[...], g_ref[...]).astype(BF16)
    z = _dot(h, win_ref[...])
    cq = _rmsnorm(z[:, C_Q:C_KV], qlat_ref[...]).astype(BF16)
    ckv = _rmsnorm(z[:, C_KV:C_POOL], kvlat_ref[...]).astype(BF16)
    zp_ref[...] = z[:, C_POOL:C_PEA]
    pe = z[:, C_PEA:C_PEB]
    pe_sw = z[:, C_PEB:D_IN_R]
    qq = _dot(cq, wuq_ref[...])
    kk = _dot(ckv, wuk_ref[...])
    vt = lax.dot_general(wuvt_ref[...], ckv, (((1,), (1,)), ((), ())), preferred_element_type=F32)
    cos = cos_ref[...]
    sin = sin_ref[...]
    scale = QK_HEAD ** -0.5 * LOG2_E
    q_main, q_swap = gains_ref[0:1, :] * cos * scale, gains_ref[1:2, :] * sin * scale
    k_main = gains_ref[2:3, :] * cos
    pe_rot = pe_sw * (gains_ref[3:4, :] * sin)
    for hd in range(MLA_HEADS):
        lo = hd * HEAD_PAD
        qh = qq[:, lo:lo + HEAD_PAD]
        qs = qq[:, HEADS_W + lo:HEADS_W + lo + HEAD_PAD]
        r = lax.rsqrt(jnp.sum(qh * qh, axis=-1, keepdims=True) * (1.0 / QK_HEAD) + EPS)
        q_ref[:, lo:lo + HEAD_PAD] = ((qh * q_main + qs * q_swap) * r).astype(BF16)
        kh = kk[:, lo:lo + HEAD_PAD] + pe
        r = lax.rsqrt(jnp.sum(kh * kh, axis=-1, keepdims=True) * (1.0 / QK_HEAD) + EPS)
        k_ref[:, lo:lo + HEAD_PAD] = ((kh * k_main + pe_rot) * r).astype(BF16)
        vt_ref[hd] = vt[hd * V_HEAD:(hd + 1) * V_HEAD, :].astype(BF16)


def _pre(x, gain, w_in_r, q_lat, kv_lat, w_uq, w_uk, w_uvt, gains, cos, sin, layer, *, tm, tk):
    seq = x.shape[0]
    assert seq % tk == 0 and tk % tm == 0
    per = tk // tm
    row = pl.BlockSpec((tm, D_MODEL), lambda i: (i, 0))
    heads = pl.BlockSpec((tm, HEADS_W), lambda i: (i, 0))
    tab = pl.BlockSpec((tm, LANES), lambda i: (i, 0))
    vt_spec = pl.BlockSpec((MLA_HEADS, None, V_HEAD, tm), lambda i: (0, i // per, 0, i % per))
    return pl.pallas_call(
        _pre_kernel,
        grid=(seq // tm,),
        in_specs=[row] + [_layer_spec(a, layer) for a in (gain, w_in_r, q_lat, kv_lat, w_uq, w_uk, w_uvt, gains)]
        + [tab, tab],
        out_specs=[heads, heads, vt_spec, pl.BlockSpec((tm, POOL_WIDTH), lambda i: (i, 0))],
        out_shape=[jax.ShapeDtypeStruct((seq, HEADS_W), BF16)] * 2
        + [jax.ShapeDtypeStruct((MLA_HEADS, seq // tk, V_HEAD, tk), BF16),
           jax.ShapeDtypeStruct((seq, POOL_WIDTH), F32)],
        compiler_params=_params("parallel"),
        name="pre",
    )(x, gain, w_in_r, q_lat, kv_lat, w_uq, w_uk, w_uvt, gains, cos, sin)


def _fold_rows(p):
    return jnp.sum(p.reshape(p.shape[0] // SUBLANES, SUBLANES, p.shape[1]), axis=0)


def _store_heads(o_ref, acc, den):
    o_ref[...] = (acc / jnp.sum(den, axis=0, keepdims=True)).astype(BF16)


def _attn_kernel(q_ref, k_ref, vt_ref, o_ref, s_ref, *, tk):
    tq = q_ref.shape[0]
    nk = k_ref.shape[0] // tk
    nsub = tk // KSUB
    qt = q_ref[...].astype(F32).T.astype(BF16)

    def scores_sub(j, c, slot, cmax):
        kc = k_ref[pl.ds(pl.multiple_of(j * tk + c * KSUB, KSUB), KSUB), :]
        st = _dot(kc, qt)
        s_ref[slot, c * KSUB:(c + 1) * KSUB, :] = st
        return jnp.maximum(cmax, jnp.max(st.reshape(KSUB // SUBLANES, SUBLANES, tq), axis=0))

    def step(j, slot, cmax, m, den, acc, j_next):
        m_new = jnp.maximum(m, jnp.max(cmax, axis=0, keepdims=True))
        alpha = jnp.exp2(m - m_new)
        den, acc = alpha * den, alpha * acc
        cmax_next = jnp.full((SUBLANES, tq), -jnp.inf, F32)
        for c in range(nsub):
            cmax_next = scores_sub(j_next, c, 1 - slot, cmax_next)
            p = jnp.exp2(s_ref[slot, c * KSUB:(c + 1) * KSUB, :] - m_new)
            den = den + _fold_rows(p)
            acc = acc + _dot(vt_ref[j, :, c * KSUB:(c + 1) * KSUB], p.astype(BF16))
        return cmax_next, m_new, den, acc

    def pair(jj, carry):
        j = 2 * jj
        carry = step(j, 0, *carry, j + 1)
        return step(j + 1, 1, *carry, jnp.minimum(j + 2, nk - 1))

    cmax0 = jnp.full((SUBLANES, tq), -jnp.inf, F32)
    for c in range(nsub):
        cmax0 = scores_sub(0, c, 0, cmax0)
    m0 = jnp.full((1, tq), -jnp.inf, F32)
    den0 = jnp.zeros((SUBLANES, tq), F32)
    acc0 = jnp.zeros((V_HEAD, tq), F32)
    _, _, den, acc = lax.fori_loop(0, nk // 2, pair, (cmax0, m0, den0, acc0))
    _store_heads(o_ref, acc, den)


def _attn_unshifted_kernel(q_ref, k_ref, vt_ref, o_ref, s_ref, *, tk):
    tq = q_ref.shape[0]
    nk = k_ref.shape[0] // tk
    nsub = tk // KSUB
    last = k_ref.shape[0] // KSUB - 1
    qt = q_ref[...].astype(F32).T.astype(BF16)

    nq = tq // QSPLIT

    def scores_sub(g, slot, h):
        kc = k_ref[pl.ds(pl.multiple_of(g * KSUB, KSUB), KSUB), :]
        s_ref[slot, :, h * nq:(h + 1) * nq] = _dot(kc, qt[:, h * nq:(h + 1) * nq])

    per = min(CHUNKS_PER_TRIP, nk)

    def trip(jj, carry):
        dens, accs = list(carry[0]), list(carry[1])
        for u in range(per * nsub):
            j, c = jj * per + u // nsub, u % nsub
            g = j * nsub + c
            for h in range(QSPLIT):
                scores_sub(jnp.minimum(g + AHEAD, last), (u + AHEAD) % RING, h)
                p = jnp.exp2(s_ref[u % RING, :, h * nq:(h + 1) * nq])
                dens[h] = dens[h] + _fold_rows(p)
                accs[h] = accs[h] + _dot(vt_ref[j, :, c * KSUB:(c + 1) * KSUB], p.astype(BF16))
        return tuple(dens), tuple(accs)

    for g in range(AHEAD):
        for h in range(QSPLIT):
            scores_sub(g, g % RING, h)
    dens, accs = lax.fori_loop(0, nk // per, trip,
                               (tuple(jnp.zeros((SUBLANES, nq), F32) for _ in range(QSPLIT)),
                                tuple(jnp.zeros((V_HEAD, nq), F32) for _ in range(QSPLIT))))
    _store_heads(o_ref, jnp.concatenate(accs, axis=1), jnp.concatenate(dens, axis=1))


def _attn(q, k, vt, *, tq, tk, shifted):
    seq = q.shape[0]
    nk = seq // tk
    assert seq % tq == 0 and seq % tk == 0 and nk % 2 == 0 and nk % min(CHUNKS_PER_TRIP, nk) == 0
    assert tk % (KSUB * RING) == 0 and tq % (QSPLIT * LANES) == 0
    qspec = pl.BlockSpec((tq, HEAD_PAD), lambda h, i: (i, h))
    kspec = pl.BlockSpec((seq, HEAD_PAD), lambda h, i: (0, h))
    vspec = pl.BlockSpec((None, seq // tk, V_HEAD, tk), lambda h, i: (h, 0, 0, 0))
    body = _attn_kernel if shifted else _attn_unshifted_kernel
    return pl.pallas_call(
        functools.partial(body, tk=tk),
        grid=(MLA_HEADS, seq // tq),
        in_specs=[qspec, kspec, vspec],
        out_specs=pl.BlockSpec((V_HEAD, tq), lambda h, i: (h, i)),
        out_shape=jax.ShapeDtypeStruct((MLA_WIDTH, seq), BF16),
        scratch_shapes=[pltpu.VMEM((2, tk, tq) if shifted else (RING, KSUB, tq), F32)],
        compiler_params=_params("parallel", "parallel"),
        name="attn" if shifted else "attn_unshifted",
    )(q, k, vt)


def _score_bound(q_gain, k_gain):
    return (LOG2_E * QK_HEAD ** 0.5) * jnp.max(jnp.abs(q_gain)) * jnp.max(jnp.abs(k_gain)) * BF16_NORM_SLACK


def _post_kernel(x_ref, o_ref, zc_ref, zprev_ref, znext_ref, wpool_ref, ps_ref, woa_ref, wob_ref,
                 out_ref, ext_ref, *, tm, seq):
    i = pl.program_id(0)
    last = pl.num_programs(0) - 1
    zc = zc_ref[...]
    ext_ref[0:POOL_HALO, :] = jnp.where(i > 0, zprev_ref[...], 0.0)
    ext_ref[POOL_HALO:POOL_HALO + tm, :] = zc
    ext_ref[POOL_HALO + tm:2 * POOL_HALO + tm, :] = jnp.where(i < last, znext_ref[...], 0.0)
    pos = i * tm + lax.broadcasted_iota(jnp.int32, (tm, 1), 0)
    parts = []
    for g, w in enumerate(POOL_WINDOWS):
        left = w // 2
        right = w - 1 - left
        c0 = g * POOL_GROUP_DIM
        wsum = ext_ref[POOL_HALO - left:POOL_HALO - left + tm, c0:c0 + POOL_GROUP_DIM]
        for d in range(-left + 1, right + 1):
            wsum = wsum + ext_ref[POOL_HALO + d:POOL_HALO + d + tm, c0:c0 + POOL_GROUP_DIM]
        cnt = (jnp.minimum(pos + right + 1, seq) - jnp.maximum(pos - left, 0)).astype(F32)
        mixed = (wsum / cnt - zc[:, c0:c0 + POOL_GROUP_DIM]).astype(BF16)
        y = _dot(mixed, wpool_ref[g]) * ps_ref[:, c0:c0 + POOL_GROUP_DIM]
        parts.append(y.astype(BF16))
    b = jnp.concatenate(parts, axis=1)
    attn_part = lax.dot_general(o_ref[...], woa_ref[...], (((0,), (0,)), ((), ())), preferred_element_type=F32)
    out_ref[...] = x_ref[...] + (attn_part + _dot(b, wob_ref[...]))


def _post(x, o, zp, w_pool, pool_scale, w_out_a, w_out_b, layer, *, tm):
    seq = x.shape[0]
    assert seq % tm == 0 and tm % POOL_HALO == 0
    per = tm // POOL_HALO
    nblk = seq // POOL_HALO
    row = pl.BlockSpec((tm, D_MODEL), lambda i: (i, 0))
    return pl.pallas_call(
        functools.partial(_post_kernel, tm=tm, seq=seq),
        grid=(seq // tm,),
        in_specs=[row,
                  pl.BlockSpec((MLA_WIDTH, tm), lambda i: (0, i)),
                  pl.BlockSpec((tm, POOL_WIDTH), lambda i: (i, 0)),
                  pl.BlockSpec((POOL_HALO, POOL_WIDTH), lambda i: (jnp.maximum(i * per - 1, 0), 0)),
                  pl.BlockSpec((POOL_HALO, POOL_WIDTH), lambda i: (jnp.minimum((i + 1) * per, nblk - 1), 0)),
                  ] + [_layer_spec(a, layer) for a in (w_pool, pool_scale, w_out_a, w_out_b)],
        out_specs=row,
        out_shape=jax.ShapeDtypeStruct(x.shape, F32),
        scratch_shapes=[pltpu.VMEM((tm + 2 * POOL_HALO, POOL_WIDTH), F32)],
        compiler_params=_params("parallel"),
        name="post",
    )(x, o, zp, zp, zp, w_pool, pool_scale, w_out_a, w_out_b)


def _swap_halves(t):
    return jnp.concatenate([t[..., HALF_ROPE:], t[..., :HALF_ROPE]], axis=-1)


def _head_block(nope, rope):
    pad = jnp.zeros(nope.shape[:-1] + (HEAD_PAD - QK_HEAD,), nope.dtype)
    return jnp.concatenate([nope, rope, pad], axis=-1)


def _layout_weights(w_in, w_uq, w_uk, w_uv, q_norm, k_norm, w_out):
    depth = w_in.shape[0]
    w_in, w_uq, w_uk, w_uv, w_out = (w.astype(BF16) for w in (w_in, w_uq, w_uk, w_uv, w_out))
    o_pe = Q_LORA + KV_LORA
    o_pool = o_pe + QK_ROPE
    w_pe = w_in[..., o_pe:o_pool]
    zero_nope = jnp.zeros(w_pe.shape[:-1] + (QK_NOPE,), w_pe.dtype)
    w_in_r = jnp.concatenate(
        [w_in[..., :o_pe], w_in[..., o_pool:], _head_block(zero_nope, w_pe),
         _head_block(zero_nope, _swap_halves(w_pe))], axis=-1).astype(BF16)

    uq = w_uq.reshape(depth, Q_LORA, MLA_HEADS, QK_HEAD)
    uq_main = _head_block(uq[..., :QK_NOPE], uq[..., QK_NOPE:])
    uq_swap = _head_block(jnp.zeros_like(uq[..., :QK_NOPE]), _swap_halves(uq[..., QK_NOPE:]))
    w_uq_r = jnp.concatenate([uq_main.reshape(depth, Q_LORA, HEADS_W),
                              uq_swap.reshape(depth, Q_LORA, HEADS_W)], axis=-1).astype(BF16)

    uk = jnp.pad(w_uk.reshape(depth, KV_LORA, MLA_HEADS, QK_NOPE),
                 ((0, 0), (0, 0), (0, 0), (0, HEAD_PAD - QK_NOPE)))
    w_uk_r = uk.reshape(depth, KV_LORA, HEADS_W).astype(BF16)
    uvt = jnp.transpose(w_uv.reshape(depth, KV_LORA, MLA_HEADS, V_HEAD), (0, 2, 3, 1))
    w_uvt_r = uvt.reshape(depth, MLA_HEADS * V_HEAD, KV_LORA).astype(BF16)

    def gain_rows(g):
        main = _head_block(g[..., :QK_NOPE], g[..., QK_NOPE:])
        swap = _head_block(jnp.zeros_like(g[..., :QK_NOPE]), _swap_halves(g[..., QK_NOPE:]))
        return main, swap

    gq, gq_sw = gain_rows(q_norm)
    gk, gk_sw = gain_rows(k_norm)
    zero = jnp.zeros_like(gq)
    gains = jnp.stack([gq, gq_sw, gk, gk_sw, zero, zero, zero, zero], axis=1)

    return w_in_r, w_uq_r, w_uk_r, w_uvt_r, gains, w_out[:, :MLA_WIDTH], w_out[:, MLA_WIDTH:]


def _rope_tables(seq):
    inv = ROPE_THETA ** (-jnp.arange(0, QK_ROPE, 2, dtype=F32) / QK_ROPE)
    off = jnp.zeros((QK_NOPE,), F32)
    inv_lane = _head_block(off, jnp.concatenate([inv, inv]))
    sign = _head_block(off, jnp.concatenate([-jnp.ones_like(inv), jnp.ones_like(inv)]))
    ang = jnp.arange(seq, dtype=F32)[:, None] * inv_lane[None, :]
    return jnp.cos(ang), jnp.sin(ang) * sign


def _tiles(seq):
    tk = min(2048, seq // 2)
    return dict(tm=min(1024, tk), tq=min(1024, seq), tk=tk, tf=4 * MXU_TILE)


def kernel(x, ffn1_norm, ffn1_w_gu, ffn1_w_down, mix_norm, w_in, q_lat_norm, kv_lat_norm, w_uq, w_uk, w_uv,
           q_norm, k_norm, w_pool, pool_scale, w_out, ffn2_norm, ffn2_w_gu, ffn2_w_down):
    batch, seq, _ = x.shape
    depth = w_in.shape[0]
    t = _tiles(seq)
    cos, sin = _rope_tables(seq)
    w_in_r, w_uq_r, w_uk_r, w_uvt_r, gains, w_out_a, w_out_b = _layout_weights(
        w_in, w_uq, w_uk, w_uv, q_norm, k_norm, w_out)
    f1_gu, f1_d = ffn1_w_gu.astype(BF16), ffn1_w_down.astype(BF16)
    f2_gu, f2_d = ffn2_w_gu.astype(BF16), ffn2_w_down.astype(BF16)
    w_pool_b = w_pool.astype(BF16)
    ffn1_g, ffn2_g, mix_g = ffn1_norm[:, None], ffn2_norm[:, None], mix_norm[:, None]
    q_lat_g, kv_lat_g, pool_g = q_lat_norm[:, None], kv_lat_norm[:, None], pool_scale[:, None]

    outs = []
    for b in range(batch):
        xb = x[b]
        for l in range(depth):
            xb = _ffn(xb, ffn1_g, f1_gu, f1_d, l, tm=t["tm"], tf=t["tf"])
            q, k, vt, zp = _pre(xb, mix_g, w_in_r, q_lat_g, kv_lat_g, w_uq_r, w_uk_r, w_uvt_r, gains, cos, sin, l,
                                tm=t["tm"], tk=t["tk"])
            o = lax.cond(_score_bound(q_norm[l], k_norm[l]) <= MAX_UNSHIFTED_SCORE,
                         functools.partial(_attn, tq=t["tq"], tk=t["tk"], shifted=False),
                         functools.partial(_attn, tq=t["tq"], tk=t["tk"], shifted=True),
                         q, k, vt)
            xb = _post(xb, o, zp, w_pool_b, pool_g, w_out_a, w_out_b, l, tm=t["tm"])
            xb = _ffn(xb, ffn2_g, f2_gu, f2_d, l, tm=t["tm"], tf=t["tf"])
        outs.append(xb)
    return outs[0][None] if batch == 1 else jnp.stack(outs, axis=0)
```

```python
import functools

import jax
import jax.numpy as jnp
from jax import lax
from jax.experimental import pallas as pl
from jax.experimental.pallas import tpu as pltpu

D_MODEL = 1024
MLA_HEADS = 8
QK_NOPE = 64
QK_ROPE = 32
QK_HEAD = QK_NOPE + QK_ROPE
V_HEAD = 64
Q_LORA = 384
KV_LORA = 256
MLA_WIDTH = MLA_HEADS * V_HEAD
ROPE_THETA = 10000.0
POOL_WINDOWS = (2, 4, 8, 16)
POOL_GROUP_DIM = 128
POOL_WIDTH = 512
D_FF = 2816
EPS = 1e-6
LOG2_E = 1.4426950408889634
MAX_UNSHIFTED_SCORE = 60.0
BF16_NORM_SLACK = 1.01

LANES = 128
SUBLANES = 8
BF16_ROWS = 16
HEAD_PAD = LANES
HEADS_W = MLA_HEADS * HEAD_PAD
HALF_ROPE = QK_ROPE // 2
POOL_HALO = 8
MXU_TILE = 256
KSUB = MXU_TILE
CHUNKS_PER_TRIP = 8
QSPLIT = 2
AHEAD = 1
RING = 2 * AHEAD
C_Q, C_KV, C_POOL = 0, Q_LORA, Q_LORA + KV_LORA
C_PEA = C_POOL + POOL_WIDTH
C_PEB = C_PEA + LANES
D_IN_R = C_PEB + LANES

VMEM_LIMIT = 56 * 1024 * 1024

F32 = jnp.float32
BF16 = jnp.bfloat16


def _rmsnorm(x, g):
    ms = jnp.mean(x * x, axis=-1, keepdims=True)
    return x * lax.rsqrt(ms + EPS) * g


def _dot(a, b):
    return jnp.dot(a, b, preferred_element_type=F32)


def _layer_spec(stacked, layer):
    tail = stacked.shape[1:]
    return pl.BlockSpec((None,) + tail, lambda *_: (layer,) + (0,) * len(tail), pipeline_mode=pl.Buffered(1))


def _params(*sem):
    return pltpu.CompilerParams(dimension_semantics=sem, vmem_limit_bytes=VMEM_LIMIT)


def _ffn_kernel(x_ref, g_ref, wgu_ref, wd_ref, *rest, tf, casts_next):
    if casts_next:
        ngu_ref, nd_ref, o_ref, ngu_out, nd_out = rest
        ngu_out[...] = ngu_ref[...].astype(BF16)
        nd_out[...] = nd_ref[...].astype(BF16)
    else:
        (o_ref,) = rest
    x = x_ref[...]
    h = _rmsnorm(x, g_ref[...]).astype(BF16)
    acc = jnp.zeros(x.shape, F32)
    for lo in range(0, D_FF, tf):
        hi = min(lo + tf, D_FF)
        gate = _dot(h, wgu_ref[:, lo:hi])
        up = _dot(h, wgu_ref[:, D_FF + lo:D_FF + hi])
        act = (gate * jax.nn.sigmoid(gate) * up).astype(BF16)
        acc = acc + _dot(act, wd_ref[lo:hi, :])
    o_ref[...] = x + 0.5 * acc


def _ffn(x, gain, layer, w_gu, w_down, nxt, *, tm, tf):
    seq = x.shape[0]
    steps = seq // tm
    assert seq % tm == 0 and tf % MXU_TILE == 0
    row = pl.BlockSpec((tm, D_MODEL), lambda i: (i, 0))
    in_specs = [row, _layer_spec(gain, layer), _layer_spec(w_gu, 0), _layer_spec(w_down, 0)]
    out_specs, out_shape, args = [row], [jax.ShapeDtypeStruct(x.shape, F32)], [x, gain, w_gu, w_down]
    if nxt is not None:
        n_gu, n_down, n_layer = nxt
        for w in (n_gu, n_down):
            rows, cols = w.shape[1] // steps, w.shape[2]
            assert w.shape[1] % steps == 0 and rows % BF16_ROWS == 0
            in_specs.append(pl.BlockSpec((None, rows, cols), lambda i: (n_layer, i, 0)))
            out_specs.append(pl.BlockSpec((None, rows, cols), lambda i: (0, i, 0)))
            out_shape.append(jax.ShapeDtypeStruct((1,) + w.shape[1:], BF16))
            args.append(w)
    out = pl.pallas_call(
        functools.partial(_ffn_kernel, tf=tf, casts_next=nxt is not None),
        grid=(steps,),
        in_specs=in_specs,
        out_specs=out_specs,
        out_shape=out_shape,
        compiler_params=_params("parallel"),
        name="ffn",
    )(*args)
    return out if nxt is not None else out[0]


def _pre_kernel(x_ref, g_ref, win_ref, qlat_ref, kvlat_ref, wuq_ref, wuk_ref, wuvt_ref, gains_ref,
                cos_ref, sin_ref, q_ref, k_ref, vt_ref, zp_ref):
    h = _rmsnorm(x_ref[...], g_ref[...]).astype(BF16)
    z = _dot(h, win_ref[...])
    cq = _rmsnorm(z[:, C_Q:C_KV], qlat_ref[...]).astype(BF16)
    ckv = _rmsnorm(z[:, C_KV:C_POOL], kvlat_ref[...]).astype(BF16)
    zp_ref[...] = z[:, C_POOL:C_PEA]
    pe = z[:, C_PEA:C_PEB]
    pe_sw = z[:, C_PEB:D_IN_R]
    qq = _dot(cq, wuq_ref[...])
    kk = _dot(ckv, wuk_ref[...])
    vt = lax.dot_general(wuvt_ref[...], ckv, (((1,), (1,)), ((), ())), preferred_element_type=F32)
    cos = cos_ref[...]
    sin = sin_ref[...]
    scale = QK_HEAD ** -0.5 * LOG2_E
    q_main, q_swap = gains_ref[0:1, :] * cos * scale, gains_ref[1:2, :] * sin * scale
    k_main = gains_ref[2:3, :] * cos
    pe_rot = pe_sw * (gains_ref[3:4, :] * sin)
    for hd in range(MLA_HEADS):
        lo = hd * HEAD_PAD
        qh = qq[:, lo:lo + HEAD_PAD]
        qs = qq[:, HEADS_W + lo:HEADS_W + lo + HEAD_PAD]
        r = lax.rsqrt(jnp.sum(qh * qh, axis=-1, keepdims=True) * (1.0 / QK_HEAD) + EPS)
        q_ref[:, lo:lo + HEAD_PAD] = ((qh * q_main + qs * q_swap) * r).astype(BF16)
        kh = kk[:, lo:lo + HEAD_PAD] + pe
        r = lax.rsqrt(jnp.sum(kh * kh, axis=-1, keepdims=True) * (1.0 / QK_HEAD) + EPS)
        k_ref[:, lo:lo + HEAD_PAD] = ((kh * k_main + pe_rot) * r).astype(BF16)
        vt_ref[hd] = vt[hd * V_HEAD:(hd + 1) * V_HEAD, :].astype(BF16)


def _pre(x, gain, w_in_r, q_lat, kv_lat, w_uq, w_uk, w_uvt, gains, cos, sin, layer, *, tm, tk):
    seq = x.shape[0]
    assert seq % tk == 0 and tk % tm == 0
    per = tk // tm
    row = pl.BlockSpec((tm, D_MODEL), lambda i: (i, 0))
    heads = pl.BlockSpec((tm, HEADS_W), lambda i: (i, 0))
    tab = pl.BlockSpec((tm, LANES), lambda i: (i, 0))
    vt_spec = pl.BlockSpec((MLA_HEADS, None, V_HEAD, tm), lambda i: (0, i // per, 0, i % per))
    return pl.pallas_call(
        _pre_kernel,
        grid=(seq // tm,),
        in_specs=[row] + [_layer_spec(a, layer) for a in (gain, w_in_r, q_lat, kv_lat, w_uq, w_uk, w_uvt, gains)]
        + [tab, tab],
        out_specs=[heads, heads, vt_spec, pl.BlockSpec((tm, POOL_WIDTH), lambda i: (i, 0))],
        out_shape=[jax.ShapeDtypeStruct((seq, HEADS_W), BF16)] * 2
        + [jax.ShapeDtypeStruct((MLA_HEADS, seq // tk, V_HEAD, tk), BF16),
           jax.ShapeDtypeStruct((seq, POOL_WIDTH), F32)],
        compiler_params=_params("parallel"),
        name="pre",
    )(x, gain, w_in_r, q_lat, kv_lat, w_uq, w_uk, w_uvt, gains, cos, sin)


def _fold_rows(p):
    return jnp.sum(p.reshape(p.shape[0] // SUBLANES, SUBLANES, p.shape[1]), axis=0)


def _store_heads(o_ref, acc, den):
    o_ref[...] = (acc / jnp.sum(den, axis=0, keepdims=True)).astype(BF16)


def _attn_kernel(q_ref, k_ref, vt_ref, o_ref, s_ref, *, tk):
    tq = q_ref.shape[0]
    nk = k_ref.shape[0] // tk
    nsub = tk // KSUB
    qt = q_ref[...].astype(F32).T.astype(BF16)

    def scores_sub(j, c, slot, cmax):
        kc = k_ref[pl.ds(pl.multiple_of(j * tk + c * KSUB, KSUB), KSUB), :]
        st = _dot(kc, qt)
        s_ref[slot, c * KSUB:(c + 1) * KSUB, :] = st
        return jnp.maximum(cmax, jnp.max(st.reshape(KSUB // SUBLANES, SUBLANES, tq), axis=0))

    def step(j, slot, cmax, m, den, acc, j_next):
        m_new = jnp.maximum(m, jnp.max(cmax, axis=0, keepdims=True))
        alpha = jnp.exp2(m - m_new)
        den, acc = alpha * den, alpha * acc
        cmax_next = jnp.full((SUBLANES, tq), -jnp.inf, F32)
        for c in range(nsub):
            cmax_next = scores_sub(j_next, c, 1 - slot, cmax_next)
            p = jnp.exp2(s_ref[slot, c * KSUB:(c + 1) * KSUB, :] - m_new)
            den = den + _fold_rows(p)
            acc = acc + _dot(vt_ref[j, :, c * KSUB:(c + 1) * KSUB], p.astype(BF16))
        return cmax_next, m_new, den, acc

    def pair(jj, carry):
        j = 2 * jj
        carry = step(j, 0, *carry, j + 1)
        return step(j + 1, 1, *carry, jnp.minimum(j + 2, nk - 1))

    cmax0 = jnp.full((SUBLANES, tq), -jnp.inf, F32)
    for c in range(nsub):
        cmax0 = scores_sub(0, c, 0, cmax0)
    m0 = jnp.full((1, tq), -jnp.inf, F32)
    den0 = jnp.zeros((SUBLANES, tq), F32)
    acc0 = jnp.zeros((V_HEAD, tq), F32)
    _, _, den, acc = lax.fori_loop(0, nk // 2, pair, (cmax0, m0, den0, acc0))
    _store_heads(o_ref, acc, den)


def _attn_unshifted_kernel(q_ref, k_ref, vt_ref, o_ref, s_ref, *, tk):
    tq = q_ref.shape[0]
    nk = k_ref.shape[0] // tk
    nsub = tk // KSUB
    last = k_ref.shape[0] // KSUB - 1
    qt = q_ref[...].astype(F32).T.astype(BF16)

    nq = tq // QSPLIT

    def scores_sub(g, slot, h):
        kc = k_ref[pl.ds(pl.multiple_of(g * KSUB, KSUB), KSUB), :]
        s_ref[slot, :, h * nq:(h + 1) * nq] = _dot(kc, qt[:, h * nq:(h + 1) * nq])

    per = min(CHUNKS_PER_TRIP, nk)

    def trip(jj, carry):
        dens, accs = list(carry[0]), list(carry[1])
        for u in range(per * nsub):
            j, c = jj * per + u // nsub, u % nsub
            g = j * nsub + c
            for h in range(QSPLIT):
                scores_sub(jnp.minimum(g + AHEAD, last), (u + AHEAD) % RING, h)
                p = jnp.exp2(s_ref[u % RING, :, h * nq:(h + 1) * nq])
                dens[h] = dens[h] + _fold_rows(p)
                accs[h] = accs[h] + _dot(vt_ref[j, :, c * KSUB:(c + 1) * KSUB], p.astype(BF16))
        return tuple(dens), tuple(accs)

    for g in range(AHEAD):
        for h in range(QSPLIT):
            scores_sub(g, g % RING, h)
    dens, accs = lax.fori_loop(0, nk // per, trip,
                               (tuple(jnp.zeros((SUBLANES, nq), F32) for _ in range(QSPLIT)),
                                tuple(jnp.zeros((V_HEAD, nq), F32) for _ in range(QSPLIT))))
    _store_heads(o_ref, jnp.concatenate(accs, axis=1), jnp.concatenate(dens, axis=1))


def _attn(q, k, vt, *, tq, tk, shifted):
    seq = q.shape[0]
    nk = seq // tk
    assert seq % tq == 0 and seq % tk == 0 and nk % 2 == 0 and nk % min(CHUNKS_PER_TRIP, nk) == 0
    assert tk % (KSUB * RING) == 0 and tq % (QSPLIT * LANES) == 0
    qspec = pl.BlockSpec((tq, HEAD_PAD), lambda h, i: (i, h))
    kspec = pl.BlockSpec((seq, HEAD_PAD), lambda h, i: (0, h))
    vspec = pl.BlockSpec((None, seq // tk, V_HEAD, tk), lambda h, i: (h, 0, 0, 0))
    body = _attn_kernel if shifted else _attn_unshifted_kernel
    return pl.pallas_call(
        functools.partial(body, tk=tk),
        grid=(MLA_HEADS, seq // tq),
        in_specs=[qspec, kspec, vspec],
        out_specs=pl.BlockSpec((V_HEAD, tq), lambda h, i: (h, i)),
        out_shape=jax.ShapeDtypeStruct((MLA_WIDTH, seq), BF16),
        scratch_shapes=[pltpu.VMEM((2, tk, tq) if shifted else (RING, KSUB, tq), F32)],
        compiler_params=_params("parallel", "parallel"),
        name="attn" if shifted else "attn_unshifted",
    )(q, k, vt)


def _score_bound(q_gain, k_gain):
    return (LOG2_E * QK_HEAD ** 0.5) * jnp.max(jnp.abs(q_gain)) * jnp.max(jnp.abs(k_gain)) * BF16_NORM_SLACK


def _post_kernel(x_ref, o_ref, zc_ref, zprev_ref, znext_ref, wpool_ref, ps_ref, woa_ref, wob_ref,
                 out_ref, ext_ref, *, tm, seq):
    i = pl.program_id(0)
    last = pl.num_programs(0) - 1
    zc = zc_ref[...]
    ext_ref[0:POOL_HALO, :] = jnp.where(i > 0, zprev_ref[...], 0.0)
    ext_ref[POOL_HALO:POOL_HALO + tm, :] = zc
    ext_ref[POOL_HALO + tm:2 * POOL_HALO + tm, :] = jnp.where(i < last, znext_ref[...], 0.0)
    pos = i * tm + lax.broadcasted_iota(jnp.int32, (tm, 1), 0)
    parts = []
    for g, w in enumerate(POOL_WINDOWS):
        left = w // 2
        right = w - 1 - left
        c0 = g * POOL_GROUP_DIM
        wsum = ext_ref[POOL_HALO - left:POOL_HALO - left + tm, c0:c0 + POOL_GROUP_DIM]
        for d in range(-left + 1, right + 1):
            wsum = wsum + ext_ref[POOL_HALO + d:POOL_HALO + d + tm, c0:c0 + POOL_GROUP_DIM]
        cnt = (jnp.minimum(pos + right + 1, seq) - jnp.maximum(pos - left, 0)).astype(F32)
        mixed = (wsum / cnt - zc[:, c0:c0 + POOL_GROUP_DIM]).astype(BF16)
        y = _dot(mixed, wpool_ref[g]) * ps_ref[:, c0:c0 + POOL_GROUP_DIM]
        parts.append(y.astype(BF16))
    b = jnp.concatenate(parts, axis=1)
    attn_part = lax.dot_general(o_ref[...], woa_ref[...], (((0,), (0,)), ((), ())), preferred_element_type=F32)
    out_ref[...] = x_ref[...] + (attn_part + _dot(b, wob_ref[...]))


def _post(x, o, zp, w_pool, pool_scale, w_out_a, w_out_b, layer, *, tm):
    seq = x.shape[0]
    assert seq % tm == 0 and tm % POOL_HALO == 0
    per = tm // POOL_HALO
    nblk = seq // POOL_HALO
    row = pl.BlockSpec((tm, D_MODEL), lambda i: (i, 0))
    return pl.pallas_call(
        functools.partial(_post_kernel, tm=tm, seq=seq),
        grid=(seq // tm,),
        in_specs=[row,
                  pl.BlockSpec((MLA_WIDTH, tm), lambda i: (0, i)),
                  pl.BlockSpec((tm, POOL_WIDTH), lambda i: (i, 0)),
                  pl.BlockSpec((POOL_HALO, POOL_WIDTH), lambda i: (jnp.maximum(i * per - 1, 0), 0)),
                  pl.BlockSpec((POOL_HALO, POOL_WIDTH), lambda i: (jnp.minimum((i + 1) * per, nblk - 1), 0)),
                  ] + [_layer_spec(a, layer) for a in (w_pool, pool_scale, w_out_a, w_out_b)],
        out_specs=row,
        out_shape=jax.ShapeDtypeStruct(x.shape, F32),
        scratch_shapes=[pltpu.VMEM((tm + 2 * POOL_HALO, POOL_WIDTH), F32)],
        compiler_params=_params("parallel"),
        name="post",
    )(x, o, zp, zp, zp, w_pool, pool_scale, w_out_a, w_out_b)


def _swap_halves(t):
    return jnp.concatenate([t[..., HALF_ROPE:], t[..., :HALF_ROPE]], axis=-1)


def _head_block(nope, rope):
    pad = jnp.zeros(nope.shape[:-1] + (HEAD_PAD - QK_HEAD,), nope.dtype)
    return jnp.concatenate([nope, rope, pad], axis=-1)


def _layout_weights(w_in, w_uq, w_uk, w_uv, q_norm, k_norm, w_out):
    depth = w_in.shape[0]
    w_in, w_uq, w_uk, w_uv, w_out = (w.astype(BF16) for w in (w_in, w_uq, w_uk, w_uv, w_out))
    o_pe = Q_LORA + KV_LORA
    o_pool = o_pe + QK_ROPE
    w_pe = w_in[..., o_pe:o_pool]
    zero_nope = jnp.zeros(w_pe.shape[:-1] + (QK_NOPE,), w_pe.dtype)
    w_in_r = jnp.concatenate(
        [w_in[..., :o_pe], w_in[..., o_pool:], _head_block(zero_nope, w_pe),
         _head_block(zero_nope, _swap_halves(w_pe))], axis=-1).astype(BF16)

    uq = w_uq.reshape(depth, Q_LORA, MLA_HEADS, QK_HEAD)
    uq_main = _head_block(uq[..., :QK_NOPE], uq[..., QK_NOPE:])
    uq_swap = _head_block(jnp.zeros_like(uq[..., :QK_NOPE]), _swap_halves(uq[..., QK_NOPE:]))
    w_uq_r = jnp.concatenate([uq_main.reshape(depth, Q_LORA, HEADS_W),
                              uq_swap.reshape(depth, Q_LORA, HEADS_W)], axis=-1).astype(BF16)

    uk = jnp.pad(w_uk.reshape(depth, KV_LORA, MLA_HEADS, QK_NOPE),
                 ((0, 0), (0, 0), (0, 0), (0, HEAD_PAD - QK_NOPE)))
    w_uk_r = uk.reshape(depth, KV_LORA, HEADS_W).astype(BF16)
    uvt = jnp.transpose(w_uv.reshape(depth, KV_LORA, MLA_HEADS, V_HEAD), (0, 2, 3, 1))
    w_uvt_r = uvt.reshape(depth, MLA_HEADS * V_HEAD, KV_LORA).astype(BF16)

    def gain_rows(g):
        main = _head_block(g[..., :QK_NOPE], g[..., QK_NOPE:])
        swap = _head_block(jnp.zeros_like(g[..., :QK_NOPE]), _swap_halves(g[..., QK_NOPE:]))
        return main, swap

    gq, gq_sw = gain_rows(q_norm)
    gk, gk_sw = gain_rows(k_norm)
    zero = jnp.zeros_like(gq)
    gains = jnp.stack([gq, gq_sw, gk, gk_sw, zero, zero, zero, zero], axis=1)

    return w_in_r, w_uq_r, w_uk_r, w_uvt_r, gains, w_out[:, :MLA_WIDTH], w_out[:, MLA_WIDTH:]


def _rope_tables(seq):
    inv = ROPE_THETA ** (-jnp.arange(0, QK_ROPE, 2, dtype=F32) / QK_ROPE)
    off = jnp.zeros((QK_NOPE,), F32)
    inv_lane = _head_block(off, jnp.concatenate([inv, inv]))
    sign = _head_block(off, jnp.concatenate([-jnp.ones_like(inv), jnp.ones_like(inv)]))
    ang = jnp.arange(seq, dtype=F32)[:, None] * inv_lane[None, :]
    return jnp.cos(ang), jnp.sin(ang) * sign


def _tiles(seq):
    tk = min(2048, seq // 2)
    return dict(tm=min(1024, tk), tq=min(1024, seq), tk=tk, tf=4 * MXU_TILE)


def kernel(x, ffn1_norm, ffn1_w_gu, ffn1_w_down, mix_norm, w_in, q_lat_norm, kv_lat_norm, w_uq, w_uk, w_uv,
           q_norm, k_norm, w_pool, pool_scale, w_out, ffn2_norm, ffn2_w_gu, ffn2_w_down):
    batch, seq, _ = x.shape
    depth = w_in.shape[0]
    t = _tiles(seq)
    cos, sin = _rope_tables(seq)
    w_in_r, w_uq_r, w_uk_r, w_uvt_r, gains, w_out_a, w_out_b = _layout_weights(
        w_in, w_uq, w_uk, w_uv, q_norm, k_norm, w_out)
    gu, down = ffn1_w_gu[:1].astype(BF16), ffn1_w_down[:1].astype(BF16)
    w_pool_b = w_pool.astype(BF16)
    ffn1_g, ffn2_g, mix_g = ffn1_norm[:, None], ffn2_norm[:, None], mix_norm[:, None]
    q_lat_g, kv_lat_g, pool_g = q_lat_norm[:, None], kv_lat_norm[:, None], pool_scale[:, None]

    outs = []
    for b in range(batch):
        xb = x[b]
        for l in range(depth):
            xb, gu, down = _ffn(xb, ffn1_g, l, gu, down, (ffn2_w_gu, ffn2_w_down, l), tm=t["tm"], tf=t["tf"])
            q, k, vt, zp = _pre(xb, mix_g, w_in_r, q_lat_g, kv_lat_g, w_uq_r, w_uk_r, w_uvt_r, gains, cos, sin, l,
                                tm=t["tm"], tk=t["tk"])
            o = lax.cond(_score_bound(q_norm[l], k_norm[l]) <= MAX_UNSHIFTED_SCORE,
                         functools.partial(_attn, tq=t["tq"], tk=t["tk"], shifted=False),
                         functools.partial(_attn, tq=t["tq"], tk=t["tk"], shifted=True),
                         q, k, vt)
            xb = _post(xb, o, zp, w_pool_b, pool_g, w_out_a, w_out_b, l, tm=t["tm"])
            if l + 1 < depth or b + 1 < batch:
                nxt = (ffn1_w_gu, ffn1_w_down, (l + 1) % depth)
                xb, gu, down = _ffn(xb, ffn2_g, l, gu, down, nxt, tm=t["tm"], tf=t["tf"])
            else:
                xb = _ffn(xb, ffn2_g, l, gu, down, None, tm=t["tm"], tf=t["tf"])
        outs.append(xb)
    return outs[0][None] if batch == 1 else jnp.stack(outs, axis=0)
```

```python
import functools

import jax
import jax.numpy as jnp
from jax import lax
from jax.experimental import pallas as pl
from jax.experimental.pallas import tpu as pltpu

D_MODEL = 1024
MLA_HEADS = 8
QK_NOPE = 64
QK_ROPE = 32
QK_HEAD = QK_NOPE + QK_ROPE
V_HEAD = 64
Q_LORA = 384
KV_LORA = 256
MLA_WIDTH = MLA_HEADS * V_HEAD
ROPE_THETA = 10000.0
POOL_WINDOWS = (2, 4, 8, 16)
POOL_GROUP_DIM = 128
POOL_WIDTH = 512
D_FF = 2816
EPS = 1e-6
LOG2_E = 1.4426950408889634
MAX_UNSHIFTED_SCORE = 60.0
BF16_NORM_SLACK = 1.01

LANES = 128
SUBLANES = 8
BF16_ROWS = 16
HEAD_PAD = LANES
HEADS_W = MLA_HEADS * HEAD_PAD
HALF_ROPE = QK_ROPE // 2
POOL_HALO = 8
MXU_TILE = 256
KSUB = MXU_TILE
CHUNKS_PER_TRIP = 8
QSPLIT = 2
AHEAD = 1
RING = 2 * AHEAD
C_Q, C_KV, C_POOL = 0, Q_LORA, Q_LORA + KV_LORA
D_IN_R = C_POOL + POOL_WIDTH

VMEM_LIMIT = 56 * 1024 * 1024

F32 = jnp.float32
BF16 = jnp.bfloat16


def _rmsnorm(x, g):
    ms = jnp.mean(x * x, axis=-1, keepdims=True)
    return x * lax.rsqrt(ms + EPS) * g


def _dot(a, b):
    return jnp.dot(a, b, preferred_element_type=F32)


def _layer_spec(stacked, layer):
    tail = stacked.shape[1:]
    return pl.BlockSpec((None,) + tail, lambda *_: (layer,) + (0,) * len(tail), pipeline_mode=pl.Buffered(1))


def _params(*sem):
    return pltpu.CompilerParams(dimension_semantics=sem, vmem_limit_bytes=VMEM_LIMIT)


def _ffn_kernel(x_ref, g_ref, wgu_ref, wd_ref, *rest, tf, casts_next):
    if casts_next:
        ngu_ref, nd_ref, o_ref, ngu_out, nd_out = rest
        ngu_out[...] = ngu_ref[...].astype(BF16)
        nd_out[...] = nd_ref[...].astype(BF16)
    else:
        (o_ref,) = rest
    x = x_ref[...]
    h = _rmsnorm(x, g_ref[...]).astype(BF16)
    acc = jnp.zeros(x.shape, F32)
    for lo in range(0, D_FF, tf):
        hi = min(lo + tf, D_FF)
        gate = _dot(h, wgu_ref[:, lo:hi])
        up = _dot(h, wgu_ref[:, D_FF + lo:D_FF + hi])
        act = (gate * jax.nn.sigmoid(gate) * up).astype(BF16)
        acc = acc + _dot(act, wd_ref[lo:hi, :])
    o_ref[...] = x + 0.5 * acc


def _ffn(x, gain, layer, w_gu, w_down, nxt, *, tm, tf):
    seq = x.shape[0]
    steps = seq // tm
    assert seq % tm == 0 and tf % MXU_TILE == 0
    row = pl.BlockSpec((tm, D_MODEL), lambda i: (i, 0))
    in_specs = [row, _layer_spec(gain, layer), _layer_spec(w_gu, 0), _layer_spec(w_down, 0)]
    out_specs, out_shape, args = [row], [jax.ShapeDtypeStruct(x.shape, F32)], [x, gain, w_gu, w_down]
    if nxt is not None:
        n_gu, n_down, n_layer = nxt
        for w in (n_gu, n_down):
            rows, cols = w.shape[1] // steps, w.shape[2]
            assert w.shape[1] % steps == 0 and rows % BF16_ROWS == 0
            in_specs.append(pl.BlockSpec((None, rows, cols), lambda i: (n_layer, i, 0)))
            out_specs.append(pl.BlockSpec((None, rows, cols), lambda i: (0, i, 0)))
            out_shape.append(jax.ShapeDtypeStruct((1,) + w.shape[1:], BF16))
            args.append(w)
    out = pl.pallas_call(
        functools.partial(_ffn_kernel, tf=tf, casts_next=nxt is not None),
        grid=(steps,),
        in_specs=in_specs,
        out_specs=out_specs,
        out_shape=out_shape,
        compiler_params=_params("parallel"),
        name="ffn",
    )(*args)
    return out if nxt is not None else out[0]


def _swap_rope_rows(t):
    return jnp.concatenate([t[HALF_ROPE:], t[:HALF_ROPE]], axis=0)


def _norm_rope_t(t, main, swap):
    r = lax.rsqrt(jnp.sum(t * t, axis=0, keepdims=True) * (1.0 / QK_HEAD) + EPS)
    nope = t[:QK_NOPE] * main[:QK_NOPE]
    rope = t[QK_NOPE:QK_HEAD]
    rope = rope * main[QK_NOPE:QK_HEAD] + _swap_rope_rows(rope) * swap[QK_NOPE:QK_HEAD]
    pad = jnp.zeros((HEAD_PAD - QK_HEAD, t.shape[1]), F32)
    return jnp.concatenate([nope * r, rope * r, pad], axis=0)


def _pre_kernel(x_ref, g_ref, win_ref, wpe_ref, qlat_ref, kvlat_ref, wuq_ref, wuk_ref, wuvt_ref, gains_ref,
                cos_ref, sin_ref, qt_ref, k_ref, vt_ref, zp_ref):
    nt = (((1,), (1,)), ((), ()))
    h = _rmsnorm(x_ref[...], g_ref[...]).astype(BF16)
    z = _dot(h, win_ref[...])
    cq = _rmsnorm(z[:, C_Q:C_KV], qlat_ref[...]).astype(BF16)
    ckv = _rmsnorm(z[:, C_KV:C_POOL], kvlat_ref[...]).astype(BF16)
    zp_ref[...] = z[:, C_POOL:D_IN_R]
    pe_t = lax.dot_general(wpe_ref[...], h, nt, preferred_element_type=F32)
    qq_t = lax.dot_general(wuq_ref[...], cq, nt, preferred_element_type=F32)
    kk_t = lax.dot_general(wuk_ref[...], ckv, nt, preferred_element_type=F32)
    vt = lax.dot_general(wuvt_ref[...], ckv, nt, preferred_element_type=F32)
    cos = cos_ref[...]
    sin = sin_ref[...]
    scale = QK_HEAD ** -0.5 * LOG2_E
    q_main, q_swap = gains_ref[:, 0:1] * cos * scale, gains_ref[:, 1:2] * sin * scale
    k_main, k_swap = gains_ref[:, 2:3] * cos, gains_ref[:, 3:4] * sin
    for hd in range(MLA_HEADS):
        rows = slice(hd * HEAD_PAD, (hd + 1) * HEAD_PAD)
        qt_ref[rows, :] = _norm_rope_t(qq_t[rows], q_main, q_swap).astype(BF16)
        k_t = _norm_rope_t(kk_t[rows] + pe_t, k_main, k_swap)
        k_ref[:, rows] = k_t.T.astype(BF16)
        vt_ref[hd] = vt[hd * V_HEAD:(hd + 1) * V_HEAD, :].astype(BF16)


def _pre(x, gain, w_in_r, w_pe_t, q_lat, kv_lat, w_uq_t, w_uk_t, w_uvt, gains, cos, sin, layer, *, tm, tk):
    seq = x.shape[0]
    assert seq % tk == 0 and tk % tm == 0
    per = tk // tm
    row = pl.BlockSpec((tm, D_MODEL), lambda i: (i, 0))
    tab = pl.BlockSpec((HEAD_PAD, tm), lambda i: (0, i))
    vt_spec = pl.BlockSpec((MLA_HEADS, None, V_HEAD, tm), lambda i: (0, i // per, 0, i % per))
    params = (gain, w_in_r, w_pe_t, q_lat, kv_lat, w_uq_t, w_uk_t, w_uvt, gains)
    return pl.pallas_call(
        _pre_kernel,
        grid=(seq // tm,),
        in_specs=[row] + [_layer_spec(a, layer) for a in params] + [tab, tab],
        out_specs=[pl.BlockSpec((HEADS_W, tm), lambda i: (0, i)), pl.BlockSpec((tm, HEADS_W), lambda i: (i, 0)),
                   vt_spec, pl.BlockSpec((tm, POOL_WIDTH), lambda i: (i, 0))],
        out_shape=[jax.ShapeDtypeStruct((HEADS_W, seq), BF16), jax.ShapeDtypeStruct((seq, HEADS_W), BF16),
                   jax.ShapeDtypeStruct((MLA_HEADS, seq // tk, V_HEAD, tk), BF16),
                   jax.ShapeDtypeStruct((seq, POOL_WIDTH), F32)],
        compiler_params=_params("parallel"),
        name="pre",
    )(x, *params, cos, sin)


def _fold_rows(p):
    return jnp.sum(p.reshape(p.shape[0] // SUBLANES, SUBLANES, p.shape[1]), axis=0)


def _store_heads(o_ref, acc, den):
    o_ref[...] = (acc / jnp.sum(den, axis=0, keepdims=True)).astype(BF16)


def _attn_kernel(q_ref, k_ref, vt_ref, o_ref, s_ref, *, tk):
    tq = q_ref.shape[1]
    nk = k_ref.shape[0] // tk
    nsub = tk // KSUB
    qt = q_ref[...]

    def scores_sub(j, c, slot, cmax):
        kc = k_ref[pl.ds(pl.multiple_of(j * tk + c * KSUB, KSUB), KSUB), :]
        st = _dot(kc, qt)
        s_ref[slot, c * KSUB:(c + 1) * KSUB, :] = st
        return jnp.maximum(cmax, jnp.max(st.reshape(KSUB // SUBLANES, SUBLANES, tq), axis=0))

    def step(j, slot, cmax, m, den, acc, j_next):
        m_new = jnp.maximum(m, jnp.max(cmax, axis=0, keepdims=True))
        alpha = jnp.exp2(m - m_new)
        den, acc = alpha * den, alpha * acc
        cmax_next = jnp.full((SUBLANES, tq), -jnp.inf, F32)
        for c in range(nsub):
            cmax_next = scores_sub(j_next, c, 1 - slot, cmax_next)
            p = jnp.exp2(s_ref[slot, c * KSUB:(c + 1) * KSUB, :] - m_new)
            den = den + _fold_rows(p)
            acc = acc + _dot(vt_ref[j, :, c * KSUB:(c + 1) * KSUB], p.astype(BF16))
        return cmax_next, m_new, den, acc

    def pair(jj, carry):
        j = 2 * jj
        carry = step(j, 0, *carry, j + 1)
        return step(j + 1, 1, *carry, jnp.minimum(j + 2, nk - 1))

    cmax0 = jnp.full((SUBLANES, tq), -jnp.inf, F32)
    for c in range(nsub):
        cmax0 = scores_sub(0, c, 0, cmax0)
    m0 = jnp.full((1, tq), -jnp.inf, F32)
    den0 = jnp.zeros((SUBLANES, tq), F32)
    acc0 = jnp.zeros((V_HEAD, tq), F32)
    _, _, den, acc = lax.fori_loop(0, nk // 2, pair, (cmax0, m0, den0, acc0))
    _store_heads(o_ref, acc, den)


def _attn_unshifted_kernel(q_ref, k_ref, vt_ref, o_ref, s_ref, *, tk):
    tq = q_ref.shape[1]
    nk = k_ref.shape[0] // tk
    nsub = tk // KSUB
    last = k_ref.shape[0] // KSUB - 1
    qt = q_ref[...]

    nq = tq // QSPLIT

    def scores_sub(g, slot, h):
        kc = k_ref[pl.ds(pl.multiple_of(g * KSUB, KSUB), KSUB), :]
        s_ref[slot, :, h * nq:(h + 1) * nq] = _dot(kc, qt[:, h * nq:(h + 1) * nq])

    per = min(CHUNKS_PER_TRIP, nk)

    def trip(jj, carry):
        dens, accs = list(carry[0]), list(carry[1])
        for u in range(per * nsub):
            j, c = jj * per + u // nsub, u % nsub
            g = j * nsub + c
            for h in range(QSPLIT):
                scores_sub(jnp.minimum(g + AHEAD, last), (u + AHEAD) % RING, h)
                p = jnp.exp2(s_ref[u % RING, :, h * nq:(h + 1) * nq])
                dens[h] = dens[h] + _fold_rows(p)
                accs[h] = accs[h] + _dot(vt_ref[j, :, c * KSUB:(c + 1) * KSUB], p.astype(BF16))
        return tuple(dens), tuple(accs)

    for g in range(AHEAD):
        for h in range(QSPLIT):
            scores_sub(g, g % RING, h)
    dens, accs = lax.fori_loop(0, nk // per, trip,
                               (tuple(jnp.zeros((SUBLANES, nq), F32) for _ in range(QSPLIT)),
                                tuple(jnp.zeros((V_HEAD, nq), F32) for _ in range(QSPLIT))))
    _store_heads(o_ref, jnp.concatenate(accs, axis=1), jnp.concatenate(dens, axis=1))


def _attn(q, k, vt, *, tq, tk, shifted):
    seq = q.shape[1]
    nk = seq // tk
    assert seq % tq == 0 and seq % tk == 0 and nk % 2 == 0 and nk % min(CHUNKS_PER_TRIP, nk) == 0
    assert tk % (KSUB * RING) == 0 and tq % (QSPLIT * LANES) == 0
    qspec = pl.BlockSpec((HEAD_PAD, tq), lambda h, i: (h, i))
    kspec = pl.BlockSpec((seq, HEAD_PAD), lambda h, i: (0, h))
    vspec = pl.BlockSpec((None, seq // tk, V_HEAD, tk), lambda h, i: (h, 0, 0, 0))
    body = _attn_kernel if shifted else _attn_unshifted_kernel
    return pl.pallas_call(
        functools.partial(body, tk=tk),
        grid=(MLA_HEADS, seq // tq),
        in_specs=[qspec, kspec, vspec],
        out_specs=pl.BlockSpec((V_HEAD, tq), lambda h, i: (h, i)),
        out_shape=jax.ShapeDtypeStruct((MLA_WIDTH, seq), BF16),
        scratch_shapes=[pltpu.VMEM((2, tk, tq) if shifted else (RING, KSUB, tq), F32)],
        compiler_params=_params("parallel", "parallel"),
        name="attn" if shifted else "attn_unshifted",
    )(q, k, vt)


def _score_bound(q_gain, k_gain):
    return (LOG2_E * QK_HEAD ** 0.5) * jnp.max(jnp.abs(q_gain)) * jnp.max(jnp.abs(k_gain)) * BF16_NORM_SLACK


def _post_kernel(x_ref, o_ref, zc_ref, zprev_ref, znext_ref, wpool_ref, ps_ref, woa_ref, wob_ref,
                 out_ref, ext_ref, *, tm, seq):
    i = pl.program_id(0)
    last = pl.num_programs(0) - 1
    zc = zc_ref[...]
    ext_ref[0:POOL_HALO, :] = jnp.where(i > 0, zprev_ref[...], 0.0)
    ext_ref[POOL_HALO:POOL_HALO + tm, :] = zc
    ext_ref[POOL_HALO + tm:2 * POOL_HALO + tm, :] = jnp.where(i < last, znext_ref[...], 0.0)
    pos = i * tm + lax.broadcasted_iota(jnp.int32, (tm, 1), 0)
    parts = []
    for g, w in enumerate(POOL_WINDOWS):
        left = w // 2
        right = w - 1 - left
        c0 = g * POOL_GROUP_DIM
        wsum = ext_ref[POOL_HALO - left:POOL_HALO - left + tm, c0:c0 + POOL_GROUP_DIM]
        for d in range(-left + 1, right + 1):
            wsum = wsum + ext_ref[POOL_HALO + d:POOL_HALO + d + tm, c0:c0 + POOL_GROUP_DIM]
        cnt = (jnp.minimum(pos + right + 1, seq) - jnp.maximum(pos - left, 0)).astype(F32)
        mixed = (wsum / cnt - zc[:, c0:c0 + POOL_GROUP_DIM]).astype(BF16)
        y = _dot(mixed, wpool_ref[g]) * ps_ref[:, c0:c0 + POOL_GROUP_DIM]
        parts.append(y.astype(BF16))
    b = jnp.concatenate(parts, axis=1)
    attn_part = lax.dot_general(o_ref[...], woa_ref[...], (((0,), (0,)), ((), ())), preferred_element_type=F32)
    out_ref[...] = x_ref[...] + (attn_part + _dot(b, wob_ref[...]))


def _post(x, o, zp, w_pool, pool_scale, w_out_a, w_out_b, layer, *, tm):
    seq = x.shape[0]
    assert seq % tm == 0 and tm % POOL_HALO == 0
    per = tm // POOL_HALO
    nblk = seq // POOL_HALO
    row = pl.BlockSpec((tm, D_MODEL), lambda i: (i, 0))
    return pl.pallas_call(
        functools.partial(_post_kernel, tm=tm, seq=seq),
        grid=(seq // tm,),
        in_specs=[row,
                  pl.BlockSpec((MLA_WIDTH, tm), lambda i: (0, i)),
                  pl.BlockSpec((tm, POOL_WIDTH), lambda i: (i, 0)),
                  pl.BlockSpec((POOL_HALO, POOL_WIDTH), lambda i: (jnp.maximum(i * per - 1, 0), 0)),
                  pl.BlockSpec((POOL_HALO, POOL_WIDTH), lambda i: (jnp.minimum((i + 1) * per, nblk - 1), 0)),
                  ] + [_layer_spec(a, layer) for a in (w_pool, pool_scale, w_out_a, w_out_b)],
        out_specs=row,
        out_shape=jax.ShapeDtypeStruct(x.shape, F32),
        scratch_shapes=[pltpu.VMEM((tm + 2 * POOL_HALO, POOL_WIDTH), F32)],
        compiler_params=_params("parallel"),
        name="post",
    )(x, o, zp, zp, zp, w_pool, pool_scale, w_out_a, w_out_b)


def _head_rows(nope, rope):
    pad = jnp.zeros(nope.shape[:-2] + (HEAD_PAD - QK_HEAD, nope.shape[-1]), nope.dtype)
    return jnp.concatenate([nope, rope, pad], axis=-2)


def _layout_weights(w_in, w_uq, w_uk, w_uv, q_norm, k_norm, w_out):
    depth = w_in.shape[0]
    w_in, w_uq, w_uk, w_uv, w_out = (w.astype(BF16) for w in (w_in, w_uq, w_uk, w_uv, w_out))
    o_pe = Q_LORA + KV_LORA
    o_pool = o_pe + QK_ROPE
    w_in_r = jnp.concatenate([w_in[..., :o_pe], w_in[..., o_pool:]], axis=-1)
    w_pe = jnp.swapaxes(w_in[..., o_pe:o_pool], 1, 2)
    w_pe_t = _head_rows(jnp.zeros((depth, QK_NOPE, D_MODEL), BF16), w_pe)

    uq = jnp.transpose(w_uq.reshape(depth, Q_LORA, MLA_HEADS, QK_HEAD), (0, 2, 3, 1))
    w_uq_t = _head_rows(uq[..., :QK_NOPE, :], uq[..., QK_NOPE:, :]).reshape(depth, HEADS_W, Q_LORA)
    uk = jnp.transpose(w_uk.reshape(depth, KV_LORA, MLA_HEADS, QK_NOPE), (0, 2, 3, 1))
    w_uk_t = jnp.pad(uk, ((0, 0), (0, 0), (0, HEAD_PAD - QK_NOPE), (0, 0))).reshape(depth, HEADS_W, KV_LORA)
    uvt = jnp.transpose(w_uv.reshape(depth, KV_LORA, MLA_HEADS, V_HEAD), (0, 2, 3, 1))
    w_uvt_r = uvt.reshape(depth, MLA_HEADS * V_HEAD, KV_LORA)

    def gain_cols(g):
        nope, rope = g[:, :QK_NOPE], g[:, QK_NOPE:]
        swapped = jnp.concatenate([rope[:, HALF_ROPE:], rope[:, :HALF_ROPE]], axis=-1)
        pad = jnp.zeros((depth, HEAD_PAD - QK_HEAD), g.dtype)
        return (jnp.concatenate([nope, rope, pad], axis=-1),
                jnp.concatenate([jnp.zeros_like(nope), swapped, pad], axis=-1))

    cols = gain_cols(q_norm) + gain_cols(k_norm)
    gains = jnp.stack(cols + (jnp.zeros_like(cols[0]),) * 4, axis=-1)
    return w_in_r, w_pe_t, w_uq_t, w_uk_t, w_uvt_r, gains, w_out[:, :MLA_WIDTH], w_out[:, MLA_WIDTH:]


def _rope_tables(seq):
    inv = ROPE_THETA ** (-jnp.arange(0, QK_ROPE, 2, dtype=F32) / QK_ROPE)
    off = jnp.zeros((QK_NOPE,), F32)
    pad = jnp.zeros((HEAD_PAD - QK_HEAD,), F32)
    inv_row = jnp.concatenate([off, inv, inv, pad])
    sign = jnp.concatenate([off, -jnp.ones_like(inv), jnp.ones_like(inv), pad])
    ang = inv_row[:, None] * jnp.arange(seq, dtype=F32)[None, :]
    return jnp.cos(ang), jnp.sin(ang) * sign[:, None]


def _tiles(seq):
    tk = min(2048, seq // 2)
    return dict(tm=min(1024, tk), tq=min(1024, seq), tk=tk, tf=4 * MXU_TILE)


def kernel(x, ffn1_norm, ffn1_w_gu, ffn1_w_down, mix_norm, w_in, q_lat_norm, kv_lat_norm, w_uq, w_uk, w_uv,
           q_norm, k_norm, w_pool, pool_scale, w_out, ffn2_norm, ffn2_w_gu, ffn2_w_down):
    batch, seq, _ = x.shape
    depth = w_in.shape[0]
    t = _tiles(seq)
    cos, sin = _rope_tables(seq)
    w_in_r, w_pe_t, w_uq_t, w_uk_t, w_uvt_r, gains, w_out_a, w_out_b = _layout_weights(
        w_in, w_uq, w_uk, w_uv, q_norm, k_norm, w_out)
    gu, down = ffn1_w_gu[:1].astype(BF16), ffn1_w_down[:1].astype(BF16)
    w_pool_b = w_pool.astype(BF16)
    ffn1_g, ffn2_g, mix_g = ffn1_norm[:, None], ffn2_norm[:, None], mix_norm[:, None]
    q_lat_g, kv_lat_g, pool_g = q_lat_norm[:, None], kv_lat_norm[:, None], pool_scale[:, None]

    outs = []
    for b in range(batch):
        xb = x[b]
        for l in range(depth):
            xb, gu, down = _ffn(xb, ffn1_g, l, gu, down, (ffn2_w_gu, ffn2_w_down, l), tm=t["tm"], tf=t["tf"])
            q, k, vt, zp = _pre(xb, mix_g, w_in_r, w_pe_t, q_lat_g, kv_lat_g, w_uq_t, w_uk_t, w_uvt_r, gains,
                                cos, sin, l, tm=t["tm"], tk=t["tk"])
            o = lax.cond(_score_bound(q_norm[l], k_norm[l]) <= MAX_UNSHIFTED_SCORE,
                         functools.partial(_attn, tq=t["tq"], tk=t["tk"], shifted=False),
                         functools.partial(_attn, tq=t["tq"], tk=t["tk"], shifted=True),
                         q, k, vt)
            xb = _post(xb, o, zp, w_pool_b, pool_g, w_out_a, w_out_b, l, tm=t["tm"])
            if l + 1 < depth or b + 1 < batch:
                nxt = (ffn1_w_gu, ffn1_w_down, (l + 1) % depth)
                xb, gu, down = _ffn(xb, ffn2_g, l, gu, down, nxt, tm=t["tm"], tf=t["tf"])
            else:
                xb = _ffn(xb, ffn2_g, l, gu, down, None, tm=t["tm"], tf=t["tf"])
        outs.append(xb)
    return outs[0][None] if batch == 1 else jnp.stack(outs, axis=0)
```

```python
import functools

import jax
import jax.numpy as jnp
from jax import lax
from jax.experimental import pallas as pl
from jax.experimental.pallas import tpu as pltpu

D_MODEL = 1024
MLA_HEADS = 8
QK_NOPE = 64
QK_ROPE = 32
QK_HEAD = QK_NOPE + QK_ROPE
V_HEAD = 64
Q_LORA = 384
KV_LORA = 256
MLA_WIDTH = MLA_HEADS * V_HEAD
ROPE_THETA = 10000.0
POOL_WINDOWS = (2, 4, 8, 16)
POOL_GROUP_DIM = 128
POOL_WIDTH = 512
D_FF = 2816
EPS = 1e-6
LOG2_E = 1.4426950408889634
MAX_UNSHIFTED_SCORE = 60.0
BF16_NORM_SLACK = 1.01

LANES = 128
SUBLANES = 8
BF16_ROWS = 16
HEAD_PAD = LANES
HEADS_W = MLA_HEADS * HEAD_PAD
HALF_ROPE = QK_ROPE // 2
POOL_HALO = 8
MXU_TILE = 256
KSUB = MXU_TILE
CHUNKS_PER_TRIP = 8
QSPLIT = 2
AHEAD = 1
RING = 2 * AHEAD
C_Q, C_KV, C_POOL = 0, Q_LORA, Q_LORA + KV_LORA
D_IN_R = C_POOL + POOL_WIDTH

VMEM_LIMIT = 56 * 1024 * 1024

F32 = jnp.float32
BF16 = jnp.bfloat16


def _rmsnorm(x, g):
    ms = jnp.mean(x * x, axis=-1, keepdims=True)
    return x * lax.rsqrt(ms + EPS) * g


def _dot(a, b):
    return jnp.dot(a, b, preferred_element_type=F32)


def _layer_spec(stacked, layer):
    tail = stacked.shape[1:]
    return pl.BlockSpec((None,) + tail, lambda *_: (layer,) + (0,) * len(tail), pipeline_mode=pl.Buffered(1))


def _params(*sem):
    return pltpu.CompilerParams(dimension_semantics=sem, vmem_limit_bytes=VMEM_LIMIT)


def _ffn_kernel(x_ref, g_ref, wgu_ref, wd_ref, *rest, tf, casts_next):
    if casts_next:
        ngu_ref, nd_ref, o_ref, ngu_out, nd_out = rest
        ngu_out[...] = ngu_ref[...].astype(BF16)
        nd_out[...] = nd_ref[...].astype(BF16)
    else:
        (o_ref,) = rest
    x = x_ref[...]
    h = _rmsnorm(x, g_ref[...]).astype(BF16)
    acc = jnp.zeros(x.shape, F32)
    for lo in range(0, D_FF, tf):
        hi = min(lo + tf, D_FF)
        gate = _dot(h, wgu_ref[:, lo:hi])
        up = _dot(h, wgu_ref[:, D_FF + lo:D_FF + hi])
        act = (gate * jax.nn.sigmoid(gate) * up).astype(BF16)
        acc = acc + _dot(act, wd_ref[lo:hi, :])
    o_ref[...] = x + 0.5 * acc


def _ffn(x, gain, layer, w_gu, w_down, nxt, *, tm, tf):
    seq = x.shape[0]
    steps = seq // tm
    assert seq % tm == 0 and tf % MXU_TILE == 0
    row = pl.BlockSpec((tm, D_MODEL), lambda i: (i, 0))
    in_specs = [row, _layer_spec(gain, layer), _layer_spec(w_gu, 0), _layer_spec(w_down, 0)]
    out_specs, out_shape, args = [row], [jax.ShapeDtypeStruct(x.shape, F32)], [x, gain, w_gu, w_down]
    if nxt is not None:
        n_gu, n_down, n_layer = nxt
        for w in (n_gu, n_down):
            rows, cols = w.shape[1] // steps, w.shape[2]
            assert w.shape[1] % steps == 0 and rows % BF16_ROWS == 0
            in_specs.append(pl.BlockSpec((None, rows, cols), lambda i: (n_layer, i, 0)))
            out_specs.append(pl.BlockSpec((None, rows, cols), lambda i: (0, i, 0)))
            out_shape.append(jax.ShapeDtypeStruct((1,) + w.shape[1:], BF16))
            args.append(w)
    out = pl.pallas_call(
        functools.partial(_ffn_kernel, tf=tf, casts_next=nxt is not None),
        grid=(steps,),
        in_specs=in_specs,
        out_specs=out_specs,
        out_shape=out_shape,
        compiler_params=_params("parallel"),
        name="ffn",
    )(*args)
    return out if nxt is not None else out[0]


def _swap_rope_rows(t):
    return jnp.concatenate([t[HALF_ROPE:], t[:HALF_ROPE]], axis=0)


def _norm_rope_t(t, main, swap):
    r = lax.rsqrt(jnp.sum(t * t, axis=0, keepdims=True) * (1.0 / QK_HEAD) + EPS)
    nope = t[:QK_NOPE] * main[:QK_NOPE]
    rope = t[QK_NOPE:QK_HEAD]
    rope = rope * main[QK_NOPE:QK_HEAD] + _swap_rope_rows(rope) * swap[QK_NOPE:QK_HEAD]
    pad = jnp.zeros((HEAD_PAD - QK_HEAD, t.shape[1]), F32)
    return jnp.concatenate([nope * r, rope * r, pad], axis=0)


def _pre_kernel(x_ref, g_ref, win_ref, wpe_ref, qlat_ref, kvlat_ref, wuq_ref, wuk_ref, wuvt_ref, gains_ref,
                cos_ref, sin_ref, qt_ref, k_ref, vt_ref, zp_ref):
    nt = (((1,), (1,)), ((), ()))
    h = _rmsnorm(x_ref[...], g_ref[...]).astype(BF16)
    z = _dot(h, win_ref[...])
    cq = _rmsnorm(z[:, C_Q:C_KV], qlat_ref[...]).astype(BF16)
    ckv = _rmsnorm(z[:, C_KV:C_POOL], kvlat_ref[...]).astype(BF16)
    zp_ref[...] = z[:, C_POOL:D_IN_R]
    pe_t = lax.dot_general(wpe_ref[...], h, nt, preferred_element_type=F32)
    qq_t = lax.dot_general(wuq_ref[...], cq, nt, preferred_element_type=F32)
    kk_t = lax.dot_general(wuk_ref[...], ckv, nt, preferred_element_type=F32)
    vt = lax.dot_general(wuvt_ref[...], ckv, nt, preferred_element_type=F32)
    cos = cos_ref[...]
    sin = sin_ref[...]
    scale = QK_HEAD ** -0.5 * LOG2_E
    q_main, q_swap = gains_ref[:, 0:1] * cos * scale, gains_ref[:, 1:2] * sin * scale
    k_main, k_swap = gains_ref[:, 2:3] * cos, gains_ref[:, 3:4] * sin
    for hd in range(MLA_HEADS):
        rows = slice(hd * HEAD_PAD, (hd + 1) * HEAD_PAD)
        qt_ref[rows, :] = _norm_rope_t(qq_t[rows], q_main, q_swap).astype(BF16)
        k_t = _norm_rope_t(kk_t[rows] + pe_t, k_main, k_swap)
        k_ref[:, rows] = k_t.T.astype(BF16)
        vt_ref[hd] = vt[hd * V_HEAD:(hd + 1) * V_HEAD, :].astype(BF16)


def _pre(x, gain, w_in_r, w_pe_t, q_lat, kv_lat, w_uq_t, w_uk_t, w_uvt, gains, cos, sin, layer, *, tm, tk):
    seq = x.shape[0]
    assert seq % tk == 0 and tk % tm == 0
    per = tk // tm
    row = pl.BlockSpec((tm, D_MODEL), lambda i: (i, 0))
    tab = pl.BlockSpec((HEAD_PAD, tm), lambda i: (0, i))
    vt_spec = pl.BlockSpec((MLA_HEADS, None, V_HEAD, tm), lambda i: (0, i // per, 0, i % per))
    params = (gain, w_in_r, w_pe_t, q_lat, kv_lat, w_uq_t, w_uk_t, w_uvt, gains)
    return pl.pallas_call(
        _pre_kernel,
        grid=(seq // tm,),
        in_specs=[row] + [_layer_spec(a, layer) for a in params] + [tab, tab],
        out_specs=[pl.BlockSpec((HEADS_W, tm), lambda i: (0, i)), pl.BlockSpec((tm, HEADS_W), lambda i: (i, 0)),
                   vt_spec, pl.BlockSpec((tm, POOL_WIDTH), lambda i: (i, 0))],
        out_shape=[jax.ShapeDtypeStruct((HEADS_W, seq), BF16), jax.ShapeDtypeStruct((seq, HEADS_W), BF16),
                   jax.ShapeDtypeStruct((MLA_HEADS, seq // tk, V_HEAD, tk), BF16),
                   jax.ShapeDtypeStruct((seq, POOL_WIDTH), F32)],
        compiler_params=_params("parallel"),
        name="pre",
    )(x, *params, cos, sin)


def _fold_rows(p):
    return jnp.sum(p.reshape(p.shape[0] // SUBLANES, SUBLANES, p.shape[1]), axis=0)


def _store_heads(o_ref, acc, den):
    o_ref[...] = (acc / jnp.sum(den, axis=0, keepdims=True)).astype(BF16)


def _attn_kernel(q_ref, k_ref, vt_ref, o_ref, s_ref, *, tk):
    tq = q_ref.shape[1]
    nk = k_ref.shape[0] // tk
    nsub = tk // KSUB
    qt = q_ref[...]

    def scores_sub(j, c, slot, cmax):
        kc = k_ref[pl.ds(pl.multiple_of(j * tk + c * KSUB, KSUB), KSUB), :]
        st = _dot(kc, qt)
        s_ref[slot, c * KSUB:(c + 1) * KSUB, :] = st
        return jnp.maximum(cmax, jnp.max(st.reshape(KSUB // SUBLANES, SUBLANES, tq), axis=0))

    def step(j, slot, cmax, m, den, acc, j_next):
        m_new = jnp.maximum(m, jnp.max(cmax, axis=0, keepdims=True))
        alpha = jnp.exp2(m - m_new)
        den, acc = alpha * den, alpha * acc
        cmax_next = jnp.full((SUBLANES, tq), -jnp.inf, F32)
        for c in range(nsub):
            cmax_next = scores_sub(j_next, c, 1 - slot, cmax_next)
            p = jnp.exp2(s_ref[slot, c * KSUB:(c + 1) * KSUB, :] - m_new)
            den = den + _fold_rows(p)
            acc = acc + _dot(vt_ref[j, :, c * KSUB:(c + 1) * KSUB], p.astype(BF16))
        return cmax_next, m_new, den, acc

    def pair(jj, carry):
        j = 2 * jj
        carry = step(j, 0, *carry, j + 1)
        return step(j + 1, 1, *carry, jnp.minimum(j + 2, nk - 1))

    cmax0 = jnp.full((SUBLANES, tq), -jnp.inf, F32)
    for c in range(nsub):
        cmax0 = scores_sub(0, c, 0, cmax0)
    m0 = jnp.full((1, tq), -jnp.inf, F32)
    den0 = jnp.zeros((SUBLANES, tq), F32)
    acc0 = jnp.zeros((V_HEAD, tq), F32)
    _, _, den, acc = lax.fori_loop(0, nk // 2, pair, (cmax0, m0, den0, acc0))
    _store_heads(o_ref, acc, den)


def _attn_unshifted_kernel(q_ref, k_ref, vt_ref, o_ref, s_ref, *, tk):
    tq = q_ref.shape[1]
    nk = k_ref.shape[0] // tk
    nsub = tk // KSUB
    last = k_ref.shape[0] // KSUB - 1
    qt = q_ref[...]

    nq = tq // QSPLIT

    def scores_sub(g, slot, h):
        kc = k_ref[pl.ds(pl.multiple_of(g * KSUB, KSUB), KSUB), :]
        s_ref[slot, :, h * nq:(h + 1) * nq] = _dot(kc, qt[:, h * nq:(h + 1) * nq])

    per = min(CHUNKS_PER_TRIP, nk)

    def trip(jj, carry):
        dens, accs = list(carry[0]), list(carry[1])
        for u in range(per * nsub):
            j, c = jj * per + u // nsub, u % nsub
            g = j * nsub + c
            for h in range(QSPLIT):
                scores_sub(jnp.minimum(g + AHEAD, last), (u + AHEAD) % RING, h)
                p = jnp.exp2(s_ref[u % RING, :, h * nq:(h + 1) * nq])
                dens[h] = dens[h] + _fold_rows(p)
                accs[h] = accs[h] + _dot(vt_ref[j, :, c * KSUB:(c + 1) * KSUB], p.astype(BF16))
        return tuple(dens), tuple(accs)

    for g in range(AHEAD):
        for h in range(QSPLIT):
            scores_sub(g, g % RING, h)
    dens, accs = lax.fori_loop(0, nk // per, trip,
                               (tuple(jnp.zeros((SUBLANES, nq), F32) for _ in range(QSPLIT)),
                                tuple(jnp.zeros((V_HEAD, nq), F32) for _ in range(QSPLIT))))
    _store_heads(o_ref, jnp.concatenate(accs, axis=1), jnp.concatenate(dens, axis=1))


def _attn(q, k, vt, *, tq, tk, shifted):
    seq = q.shape[1]
    nk = seq // tk
    assert seq % tq == 0 and seq % tk == 0 and nk % 2 == 0 and nk % min(CHUNKS_PER_TRIP, nk) == 0
    assert tk % (KSUB * RING) == 0 and tq % (QSPLIT * LANES) == 0
    qspec = pl.BlockSpec((HEAD_PAD, tq), lambda h, i: (h, i))
    kspec = pl.BlockSpec((seq, HEAD_PAD), lambda h, i: (0, h))
    vspec = pl.BlockSpec((None, seq // tk, V_HEAD, tk), lambda h, i: (h, 0, 0, 0))
    body = _attn_kernel if shifted else _attn_unshifted_kernel
    return pl.pallas_call(
        functools.partial(body, tk=tk),
        grid=(MLA_HEADS, seq // tq),
        in_specs=[qspec, kspec, vspec],
        out_specs=pl.BlockSpec((V_HEAD, tq), lambda h, i: (h, i)),
        out_shape=jax.ShapeDtypeStruct((MLA_WIDTH, seq), BF16),
        scratch_shapes=[pltpu.VMEM((2, tk, tq) if shifted else (RING, KSUB, tq), F32)],
        compiler_params=_params("parallel", "parallel"),
        name="attn" if shifted else "attn_unshifted",
    )(q, k, vt)


def _score_bound(q_gain, k_gain):
    return (LOG2_E * QK_HEAD ** 0.5) * jnp.max(jnp.abs(q_gain)) * jnp.max(jnp.abs(k_gain)) * BF16_NORM_SLACK


def _post_kernel(x_ref, o_ref, zc_ref, zprev_ref, znext_ref, wpool_ref, ps_ref, woa_ref, wob_ref,
                 out_ref, ext_ref, p2_ref, p4_ref, *, tm, seq):
    i = pl.program_id(0)
    last = pl.num_programs(0) - 1
    zc = zc_ref[...]
    gd, h = POOL_GROUP_DIM, POOL_HALO
    ext_ref[0:h, :] = jnp.where(i > 0, zprev_ref[...], 0.0)
    ext_ref[h:h + tm, :] = zc
    ext_ref[h + tm:2 * h + tm, :] = jnp.where(i < last, znext_ref[...], 0.0)
    ext_ref[2 * h + tm:, :] = jnp.zeros((2 * h, POOL_WIDTH), F32)
    n2, n4, n8 = tm + 3 * h, tm + 2 * h, tm + h
    p2_ref[...] = ext_ref[0:n2, gd:] + ext_ref[1:n2 + 1, gd:]
    p4_ref[...] = p2_ref[0:n4, gd:] + p2_ref[2:n4 + 2, gd:]
    p8 = p4_ref[0:n8, gd:] + p4_ref[4:n8 + 4, gd:]
    wsums = (ext_ref[h - 1:h - 1 + tm, 0:gd] + ext_ref[h:h + tm, 0:gd],
             p2_ref[h - 2:h - 2 + tm, 0:gd] + p2_ref[h:h + tm, 0:gd],
             p4_ref[h - 4:h - 4 + tm, 0:gd] + p4_ref[h:h + tm, 0:gd],
             p8[0:tm] + p8[h:h + tm])
    pos = i * tm + lax.broadcasted_iota(jnp.int32, (tm, 1), 0)
    parts = []
    for g, w in enumerate(POOL_WINDOWS):
        left = w // 2
        right = w - 1 - left
        c0 = g * gd
        cnt = (jnp.minimum(pos + right + 1, seq) - jnp.maximum(pos - left, 0)).astype(F32)
        mixed = (wsums[g] / cnt - zc[:, c0:c0 + gd]).astype(BF16)
        y = _dot(mixed, wpool_ref[g]) * ps_ref[:, c0:c0 + gd]
        parts.append(y.astype(BF16))
    b = jnp.concatenate(parts, axis=1)
    attn_part = lax.dot_general(o_ref[...], woa_ref[...], (((0,), (0,)), ((), ())), preferred_element_type=F32)
    out_ref[...] = x_ref[...] + (attn_part + _dot(b, wob_ref[...]))


def _post(x, o, zp, w_pool, pool_scale, w_out_a, w_out_b, layer, *, tm):
    seq = x.shape[0]
    assert seq % tm == 0 and tm % POOL_HALO == 0 and POOL_WINDOWS == (2, 4, 8, 16)
    per = tm // POOL_HALO
    nblk = seq // POOL_HALO
    row = pl.BlockSpec((tm, D_MODEL), lambda i: (i, 0))
    return pl.pallas_call(
        functools.partial(_post_kernel, tm=tm, seq=seq),
        grid=(seq // tm,),
        in_specs=[row,
                  pl.BlockSpec((MLA_WIDTH, tm), lambda i: (0, i)),
                  pl.BlockSpec((tm, POOL_WIDTH), lambda i: (i, 0)),
                  pl.BlockSpec((POOL_HALO, POOL_WIDTH), lambda i: (jnp.maximum(i * per - 1, 0), 0)),
                  pl.BlockSpec((POOL_HALO, POOL_WIDTH), lambda i: (jnp.minimum((i + 1) * per, nblk - 1), 0)),
                  ] + [_layer_spec(a, layer) for a in (w_pool, pool_scale, w_out_a, w_out_b)],
        out_specs=row,
        out_shape=jax.ShapeDtypeStruct(x.shape, F32),
        scratch_shapes=[pltpu.VMEM((tm + 4 * POOL_HALO, POOL_WIDTH), F32),
                        pltpu.VMEM((tm + 3 * POOL_HALO, POOL_WIDTH - POOL_GROUP_DIM), F32),
                        pltpu.VMEM((tm + 2 * POOL_HALO, POOL_WIDTH - 2 * POOL_GROUP_DIM), F32)],
        compiler_params=_params("parallel"),
        name="post",
    )(x, o, zp, zp, zp, w_pool, pool_scale, w_out_a, w_out_b)


def _head_rows(nope, rope):
    pad = jnp.zeros(nope.shape[:-2] + (HEAD_PAD - QK_HEAD, nope.shape[-1]), nope.dtype)
    return jnp.concatenate([nope, rope, pad], axis=-2)


def _layout_weights(w_in, w_uq, w_uk, w_uv, q_norm, k_norm, w_out):
    depth = w_in.shape[0]
    w_in, w_uq, w_uk, w_uv, w_out = (w.astype(BF16) for w in (w_in, w_uq, w_uk, w_uv, w_out))
    o_pe = Q_LORA + KV_LORA
    o_pool = o_pe + QK_ROPE
    w_in_r = jnp.concatenate([w_in[..., :o_pe], w_in[..., o_pool:]], axis=-1)
    w_pe = jnp.swapaxes(w_in[..., o_pe:o_pool], 1, 2)
    w_pe_t = _head_rows(jnp.zeros((depth, QK_NOPE, D_MODEL), BF16), w_pe)

    uq = jnp.transpose(w_uq.reshape(depth, Q_LORA, MLA_HEADS, QK_HEAD), (0, 2, 3, 1))
    w_uq_t = _head_rows(uq[..., :QK_NOPE, :], uq[..., QK_NOPE:, :]).reshape(depth, HEADS_W, Q_LORA)
    uk = jnp.transpose(w_uk.reshape(depth, KV_LORA, MLA_HEADS, QK_NOPE), (0, 2, 3, 1))
    w_uk_t = jnp.pad(uk, ((0, 0), (0, 0), (0, HEAD_PAD - QK_NOPE), (0, 0))).reshape(depth, HEADS_W, KV_LORA)
    uvt = jnp.transpose(w_uv.reshape(depth, KV_LORA, MLA_HEADS, V_HEAD), (0, 2, 3, 1))
    w_uvt_r = uvt.reshape(depth, MLA_HEADS * V_HEAD, KV_LORA)

    def gain_cols(g):
        nope, rope = g[:, :QK_NOPE], g[:, QK_NOPE:]
        swapped = jnp.concatenate([rope[:, HALF_ROPE:], rope[:, :HALF_ROPE]], axis=-1)
        pad = jnp.zeros((depth, HEAD_PAD - QK_HEAD), g.dtype)
        return (jnp.concatenate([nope, rope, pad], axis=-1),
                jnp.concatenate([jnp.zeros_like(nope), swapped, pad], axis=-1))

    cols = gain_cols(q_norm) + gain_cols(k_norm)
    gains = jnp.stack(cols + (jnp.zeros_like(cols[0]),) * 4, axis=-1)
    return w_in_r, w_pe_t, w_uq_t, w_uk_t, w_uvt_r, gains, w_out[:, :MLA_WIDTH], w_out[:, MLA_WIDTH:]


def _rope_tables(seq):
    inv = ROPE_THETA ** (-jnp.arange(0, QK_ROPE, 2, dtype=F32) / QK_ROPE)
    off = jnp.zeros((QK_NOPE,), F32)
    pad = jnp.zeros((HEAD_PAD - QK_HEAD,), F32)
    inv_row = jnp.concatenate([off, inv, inv, pad])
    sign = jnp.concatenate([off, -jnp.ones_like(inv), jnp.ones_like(inv), pad])
    ang = inv_row[:, None] * jnp.arange(seq, dtype=F32)[None, :]
    return jnp.cos(ang), jnp.sin(ang) * sign[:, None]


def _tiles(seq):
    tk = min(2048, seq // 2)
    return dict(tm=min(1024, tk), tq=min(1024, seq), tk=tk, tf=4 * MXU_TILE)


def kernel(x, ffn1_norm, ffn1_w_gu, ffn1_w_down, mix_norm, w_in, q_lat_norm, kv_lat_norm, w_uq, w_uk, w_uv,
           q_norm, k_norm, w_pool, pool_scale, w_out, ffn2_norm, ffn2_w_gu, ffn2_w_down):
    batch, seq, _ = x.shape
    depth = w_in.shape[0]
    t = _tiles(seq)
    cos, sin = _rope_tables(seq)
    w_in_r, w_pe_t, w_uq_t, w_uk_t, w_uvt_r, gains, w_out_a, w_out_b = _layout_weights(
        w_in, w_uq, w_uk, w_uv, q_norm, k_norm, w_out)
    gu, down = ffn1_w_gu[:1].astype(BF16), ffn1_w_down[:1].astype(BF16)
    w_pool_b = w_pool.astype(BF16)
    ffn1_g, ffn2_g, mix_g = ffn1_norm[:, None], ffn2_norm[:, None], mix_norm[:, None]
    q_lat_g, kv_lat_g, pool_g = q_lat_norm[:, None], kv_lat_norm[:, None], pool_scale[:, None]

    outs = []
    for b in range(batch):
        xb = x[b]
        for l in range(depth):
            xb, gu, down = _ffn(xb, ffn1_g, l, gu, down, (ffn2_w_gu, ffn2_w_down, l), tm=t["tm"], tf=t["tf"])
            q, k, vt, zp = _pre(xb, mix_g, w_in_r, w_pe_t, q_lat_g, kv_lat_g, w_uq_t, w_uk_t, w_uvt_r, gains,
                                cos, sin, l, tm=t["tm"], tk=t["tk"])
            o = lax.cond(_score_bound(q_norm[l], k_norm[l]) <= MAX_UNSHIFTED_SCORE,
                         functools.partial(_attn, tq=t["tq"], tk=t["tk"], shifted=False),
                         functools.partial(_attn, tq=t["tq"], tk=t["tk"], shifted=True),
                         q, k, vt)
            xb = _post(xb, o, zp, w_pool_b, pool_g, w_out_a, w_out_b, l, tm=t["tm"])
            if l + 1 < depth or b + 1 < batch:
                nxt = (ffn1_w_gu, ffn1_w_down, (l + 1) % depth)
                xb, gu, down = _ffn(xb, ffn2_g, l, gu, down, nxt, tm=t["tm"], tf=t["tf"])
            else:
                xb = _ffn(xb, ffn2_g, l, gu, down, None, tm=t["tm"], tf=t["tf"])
        outs.append(xb)
    return outs[0][None] if batch == 1 else jnp.stack(outs, axis=0)
```

```python
import functools

import jax
import jax.numpy as jnp
from jax import lax
from jax.experimental import pallas as pl
from jax.experimental.pallas import tpu as pltpu

D_MODEL = 1024
MLA_HEADS = 8
QK_NOPE = 64
QK_ROPE = 32
QK_HEAD = QK_NOPE + QK_ROPE
V_HEAD = 64
Q_LORA = 384
KV_LORA = 256
MLA_WIDTH = MLA_HEADS * V_HEAD
ROPE_THETA = 10000.0
POOL_WINDOWS = (2, 4, 8, 16)
POOL_GROUP_DIM = 128
POOL_WIDTH = 512
D_FF = 2816
EPS = 1e-6
LOG2_E = 1.4426950408889634
MAX_UNSHIFTED_SCORE = 60.0
BF16_NORM_SLACK = 1.01

LANES = 128
SUBLANES = 8
BF16_ROWS = 16
HEAD_PAD = LANES
HEADS_W = MLA_HEADS * HEAD_PAD
HALF_ROPE = QK_ROPE // 2
POOL_HALO = 8
MXU_TILE = 256
KSUB = MXU_TILE
CHUNKS_PER_TRIP = 8
QSPLIT = 2
AHEAD = 1
RING = 2 * AHEAD
C_Q, C_KV, C_POOL = 0, Q_LORA, Q_LORA + KV_LORA
D_IN_R = C_POOL + POOL_WIDTH

VMEM_LIMIT = 56 * 1024 * 1024

F32 = jnp.float32
BF16 = jnp.bfloat16


def _rmsnorm(x, g):
    ms = jnp.mean(x * x, axis=-1, keepdims=True)
    return x * lax.rsqrt(ms + EPS) * g


def _dot(a, b):
    return jnp.dot(a, b, preferred_element_type=F32)


def _layer_spec(stacked, layer):
    tail = stacked.shape[1:]
    return pl.BlockSpec((None,) + tail, lambda *_: (layer,) + (0,) * len(tail), pipeline_mode=pl.Buffered(1))


def _params(*sem):
    return pltpu.CompilerParams(dimension_semantics=sem, vmem_limit_bytes=VMEM_LIMIT)


def _ffn_kernel(x_ref, g_ref, wgu_ref, wd_ref, *rest, tf, casts_next):
    if casts_next:
        ngu_ref, nd_ref, o_ref, ngu_out, nd_out = rest
        ngu_out[...] = ngu_ref[...].astype(BF16)
        nd_out[...] = nd_ref[...].astype(BF16)
    else:
        (o_ref,) = rest
    x = x_ref[...]
    h = _rmsnorm(x, g_ref[...]).astype(BF16)
    acc = jnp.zeros(x.shape, F32)
    for lo in range(0, D_FF, tf):
        hi = min(lo + tf, D_FF)
        gate = _dot(h, wgu_ref[:, lo:hi])
        up = _dot(h, wgu_ref[:, D_FF + lo:D_FF + hi])
        act = (gate * jax.nn.sigmoid(gate) * up).astype(BF16)
        acc = acc + _dot(act, wd_ref[lo:hi, :])
    o_ref[...] = x + 0.5 * acc


def _ffn(x, gain, layer, w_gu, w_down, nxt, *, tm, tf):
    seq = x.shape[0]
    steps = seq // tm
    assert seq % tm == 0 and tf % MXU_TILE == 0
    row = pl.BlockSpec((tm, D_MODEL), lambda i: (i, 0))
    in_specs = [row, _layer_spec(gain, layer), _layer_spec(w_gu, 0), _layer_spec(w_down, 0)]
    out_specs, out_shape, args = [row], [jax.ShapeDtypeStruct(x.shape, F32)], [x, gain, w_gu, w_down]
    if nxt is not None:
        n_gu, n_down, n_layer = nxt
        for w in (n_gu, n_down):
            rows, cols = w.shape[1] // steps, w.shape[2]
            assert w.shape[1] % steps == 0 and rows % BF16_ROWS == 0
            in_specs.append(pl.BlockSpec((None, rows, cols), lambda i: (n_layer, i, 0)))
            out_specs.append(pl.BlockSpec((None, rows, cols), lambda i: (0, i, 0)))
            out_shape.append(jax.ShapeDtypeStruct((1,) + w.shape[1:], BF16))
            args.append(w)
    out = pl.pallas_call(
        functools.partial(_ffn_kernel, tf=tf, casts_next=nxt is not None),
        grid=(steps,),
        in_specs=in_specs,
        out_specs=out_specs,
        out_shape=out_shape,
        compiler_params=_params("parallel"),
        name="ffn",
    )(*args)
    return out if nxt is not None else out[0]


def _swap_rope_rows(t):
    return jnp.concatenate([t[HALF_ROPE:], t[:HALF_ROPE]], axis=0)


def _norm_rope_t(t, main, swap):
    r = lax.rsqrt(jnp.sum(t * t, axis=0, keepdims=True) * (1.0 / QK_HEAD) + EPS)
    nope = t[:QK_NOPE] * main[:QK_NOPE]
    rope = t[QK_NOPE:QK_HEAD]
    rope = rope * main[QK_NOPE:QK_HEAD] + _swap_rope_rows(rope) * swap[QK_NOPE:QK_HEAD]
    pad = jnp.zeros((HEAD_PAD - QK_HEAD, t.shape[1]), F32)
    return jnp.concatenate([nope * r, rope * r, pad], axis=0)


def _pre_kernel(x_ref, g_ref, win_ref, wpe_ref, qlat_ref, kvlat_ref, wuq_ref, wuk_ref, wuvt_ref, gains_ref,
                cos_ref, sin_ref, qt_ref, k_ref, vt_ref, zp_ref):
    nt = (((1,), (1,)), ((), ()))
    h = _rmsnorm(x_ref[...], g_ref[...]).astype(BF16)
    z = _dot(h, win_ref[...])
    cq = _rmsnorm(z[:, C_Q:C_KV], qlat_ref[...]).astype(BF16)
    ckv = _rmsnorm(z[:, C_KV:C_POOL], kvlat_ref[...]).astype(BF16)
    zp_ref[...] = z[:, C_POOL:D_IN_R]
    pe_t = lax.dot_general(wpe_ref[...], h, nt, preferred_element_type=F32)
    qq_t = lax.dot_general(wuq_ref[...], cq, nt, preferred_element_type=F32)
    kk_t = lax.dot_general(wuk_ref[...], ckv, nt, preferred_element_type=F32)
    vt = lax.dot_general(wuvt_ref[...], ckv, nt, preferred_element_type=F32)
    cos = cos_ref[...]
    sin = sin_ref[...]
    scale = QK_HEAD ** -0.5 * LOG2_E
    q_main, q_swap = gains_ref[:, 0:1] * cos * scale, gains_ref[:, 1:2] * sin * scale
    k_main, k_swap = gains_ref[:, 2:3] * cos, gains_ref[:, 3:4] * sin
    for hd in range(MLA_HEADS):
        rows = slice(hd * HEAD_PAD, (hd + 1) * HEAD_PAD)
        qt_ref[rows, :] = _norm_rope_t(qq_t[rows], q_main, q_swap).astype(BF16)
        k_t = _norm_rope_t(kk_t[rows] + pe_t, k_main, k_swap)
        k_ref[:, rows] = k_t.T.astype(BF16)
        vt_ref[hd] = vt[hd * V_HEAD:(hd + 1) * V_HEAD, :].astype(BF16)


def _pre(x, gain, w_in_r, w_pe_t, q_lat, kv_lat, w_uq_t, w_uk_t, w_uvt, gains, cos, sin, layer, *, tm, tk):
    seq = x.shape[0]
    assert seq % tk == 0 and tk % tm == 0
    per = tk // tm
    row = pl.BlockSpec((tm, D_MODEL), lambda i: (i, 0))
    tab = pl.BlockSpec((HEAD_PAD, tm), lambda i: (0, i))
    vt_spec = pl.BlockSpec((MLA_HEADS, None, V_HEAD, tm), lambda i: (0, i // per, 0, i % per))
    params = (gain, w_in_r, w_pe_t, q_lat, kv_lat, w_uq_t, w_uk_t, w_uvt, gains)
    return pl.pallas_call(
        _pre_kernel,
        grid=(seq // tm,),
        in_specs=[row] + [_layer_spec(a, layer) for a in params] + [tab, tab],
        out_specs=[pl.BlockSpec((HEADS_W, tm), lambda i: (0, i)), pl.BlockSpec((tm, HEADS_W), lambda i: (i, 0)),
                   vt_spec, pl.BlockSpec((tm, POOL_WIDTH), lambda i: (i, 0))],
        out_shape=[jax.ShapeDtypeStruct((HEADS_W, seq), BF16), jax.ShapeDtypeStruct((seq, HEADS_W), BF16),
                   jax.ShapeDtypeStruct((MLA_HEADS, seq // tk, V_HEAD, tk), BF16),
                   jax.ShapeDtypeStruct((seq, POOL_WIDTH), F32)],
        compiler_params=_params("parallel"),
        name="pre",
    )(x, *params, cos, sin)


def _fold_rows(p):
    return jnp.sum(p.reshape(p.shape[0] // SUBLANES, SUBLANES, p.shape[1]), axis=0)


def _store_heads(o_ref, acc, den):
    o_ref[...] = (acc / jnp.sum(den, axis=0, keepdims=True)).astype(BF16)


def _attn_kernel(q_ref, k_ref, vt_ref, o_ref, s_ref, *, tk):
    tq = q_ref.shape[1]
    nk = k_ref.shape[0] // tk
    nsub = tk // KSUB
    qt = q_ref[...]

    def scores_sub(j, c, slot, cmax):
        kc = k_ref[pl.ds(pl.multiple_of(j * tk + c * KSUB, KSUB), KSUB), :]
        st = _dot(kc, qt)
        s_ref[slot, c * KSUB:(c + 1) * KSUB, :] = st
        return jnp.maximum(cmax, jnp.max(st.reshape(KSUB // SUBLANES, SUBLANES, tq), axis=0))

    def step(j, slot, cmax, m, den, acc, j_next):
        m_new = jnp.maximum(m, jnp.max(cmax, axis=0, keepdims=True))
        alpha = jnp.exp2(m - m_new)
        den, acc = alpha * den, alpha * acc
        cmax_next = jnp.full((SUBLANES, tq), -jnp.inf, F32)
        for c in range(nsub):
            cmax_next = scores_sub(j_next, c, 1 - slot, cmax_next)
            p = jnp.exp2(s_ref[slot, c * KSUB:(c + 1) * KSUB, :] - m_new)
            den = den + _fold_rows(p)
            acc = acc + _dot(vt_ref[j, :, c * KSUB:(c + 1) * KSUB], p.astype(BF16))
        return cmax_next, m_new, den, acc

    def pair(jj, carry):
        j = 2 * jj
        carry = step(j, 0, *carry, j + 1)
        return step(j + 1, 1, *carry, jnp.minimum(j + 2, nk - 1))

    cmax0 = jnp.full((SUBLANES, tq), -jnp.inf, F32)
    for c in range(nsub):
        cmax0 = scores_sub(0, c, 0, cmax0)
    m0 = jnp.full((1, tq), -jnp.inf, F32)
    den0 = jnp.zeros((SUBLANES, tq), F32)
    acc0 = jnp.zeros((V_HEAD, tq), F32)
    _, _, den, acc = lax.fori_loop(0, nk // 2, pair, (cmax0, m0, den0, acc0))
    _store_heads(o_ref, acc, den)


def _attn_unshifted_kernel(q_ref, k_ref, vt_ref, o_ref, s_ref, *, tk):
    tq = q_ref.shape[1]
    nk = k_ref.shape[0] // tk
    nsub = tk // KSUB
    last = k_ref.shape[0] // KSUB - 1
    qt = q_ref[...]

    nq = tq // QSPLIT

    def scores_sub(g, slot, h):
        start = g * KSUB if isinstance(g, int) else pl.multiple_of(g * KSUB, KSUB)
        s_ref[slot, :, h * nq:(h + 1) * nq] = _dot(k_ref[pl.ds(start, KSUB), :], qt[:, h * nq:(h + 1) * nq])

    per = min(CHUNKS_PER_TRIP, nk)
    trips = nk // per

    def trip(jj, carry):
        dens, accs = list(carry[0]), list(carry[1])
        for u in range(per * nsub):
            j, c = jj * per + u // nsub, u % nsub
            ahead = j * nsub + c + AHEAD
            for h in range(QSPLIT):
                if trips > 1:
                    scores_sub(jnp.minimum(ahead, last), (u + AHEAD) % RING, h)
                elif ahead <= last:
                    scores_sub(ahead, (u + AHEAD) % RING, h)
                p = jnp.exp2(s_ref[u % RING, :, h * nq:(h + 1) * nq])
                dens[h] = dens[h] + _fold_rows(p)
                accs[h] = accs[h] + _dot(vt_ref[j, :, c * KSUB:(c + 1) * KSUB], p.astype(BF16))
        return tuple(dens), tuple(accs)

    for g in range(AHEAD):
        for h in range(QSPLIT):
            scores_sub(g, g % RING, h)
    init = (tuple(jnp.zeros((SUBLANES, nq), F32) for _ in range(QSPLIT)),
            tuple(jnp.zeros((V_HEAD, nq), F32) for _ in range(QSPLIT)))
    dens, accs = trip(0, init) if trips == 1 else lax.fori_loop(0, trips, trip, init)
    _store_heads(o_ref, jnp.concatenate(accs, axis=1), jnp.concatenate(dens, axis=1))


def _attn(q, k, vt, *, tq, tk, shifted):
    seq = q.shape[1]
    nk = seq // tk
    assert seq % tq == 0 and seq % tk == 0 and nk % 2 == 0 and nk % min(CHUNKS_PER_TRIP, nk) == 0
    assert tk % (KSUB * RING) == 0 and tq % (QSPLIT * LANES) == 0
    qspec = pl.BlockSpec((HEAD_PAD, tq), lambda h, i: (h, i))
    kspec = pl.BlockSpec((seq, HEAD_PAD), lambda h, i: (0, h))
    vspec = pl.BlockSpec((None, seq // tk, V_HEAD, tk), lambda h, i: (h, 0, 0, 0))
    body = _attn_kernel if shifted else _attn_unshifted_kernel
    return pl.pallas_call(
        functools.partial(body, tk=tk),
        grid=(MLA_HEADS, seq // tq),
        in_specs=[qspec, kspec, vspec],
        out_specs=pl.BlockSpec((V_HEAD, tq), lambda h, i: (h, i)),
        out_shape=jax.ShapeDtypeStruct((MLA_WIDTH, seq), BF16),
        scratch_shapes=[pltpu.VMEM((2, tk, tq) if shifted else (RING, KSUB, tq), F32)],
        compiler_params=_params("parallel", "parallel"),
        name="attn" if shifted else "attn_unshifted",
    )(q, k, vt)


def _score_bound(q_gain, k_gain):
    return (LOG2_E * QK_HEAD ** 0.5) * jnp.max(jnp.abs(q_gain)) * jnp.max(jnp.abs(k_gain)) * BF16_NORM_SLACK


def _post_kernel(x_ref, o_ref, zc_ref, zprev_ref, znext_ref, wpool_ref, ps_ref, woa_ref, wob_ref,
                 out_ref, ext_ref, p2_ref, p4_ref, *, tm, seq):
    i = pl.program_id(0)
    last = pl.num_programs(0) - 1
    zc = zc_ref[...]
    gd, h = POOL_GROUP_DIM, POOL_HALO
    ext_ref[0:h, :] = jnp.where(i > 0, zprev_ref[...], 0.0)
    ext_ref[h:h + tm, :] = zc
    ext_ref[h + tm:2 * h + tm, :] = jnp.where(i < last, znext_ref[...], 0.0)
    ext_ref[2 * h + tm:, :] = jnp.zeros((2 * h, POOL_WIDTH), F32)
    n2, n4, n8 = tm + 3 * h, tm + 2 * h, tm + h
    p2_ref[...] = ext_ref[0:n2, gd:] + ext_ref[1:n2 + 1, gd:]
    p4_ref[...] = p2_ref[0:n4, gd:] + p2_ref[2:n4 + 2, gd:]
    p8 = p4_ref[0:n8, gd:] + p4_ref[4:n8 + 4, gd:]
    wsums = (ext_ref[h - 1:h - 1 + tm, 0:gd] + ext_ref[h:h + tm, 0:gd],
             p2_ref[h - 2:h - 2 + tm, 0:gd] + p2_ref[h:h + tm, 0:gd],
             p4_ref[h - 4:h - 4 + tm, 0:gd] + p4_ref[h:h + tm, 0:gd],
             p8[0:tm] + p8[h:h + tm])
    pos = i * tm + lax.broadcasted_iota(jnp.int32, (tm, 1), 0)
    parts = []
    for g, w in enumerate(POOL_WINDOWS):
        left = w // 2
        right = w - 1 - left
        c0 = g * gd
        cnt = (jnp.minimum(pos + right + 1, seq) - jnp.maximum(pos - left, 0)).astype(F32)
        mixed = (wsums[g] / cnt - zc[:, c0:c0 + gd]).astype(BF16)
        y = _dot(mixed, wpool_ref[g]) * ps_ref[:, c0:c0 + gd]
        parts.append(y.astype(BF16))
    b = jnp.concatenate(parts, axis=1)
    attn_part = lax.dot_general(o_ref[...], woa_ref[...], (((0,), (0,)), ((), ())), preferred_element_type=F32)
    out_ref[...] = x_ref[...] + (attn_part + _dot(b, wob_ref[...]))


def _post(x, o, zp, w_pool, pool_scale, w_out_a, w_out_b, layer, *, tm):
    seq = x.shape[0]
    assert seq % tm == 0 and tm % POOL_HALO == 0 and POOL_WINDOWS == (2, 4, 8, 16)
    per = tm // POOL_HALO
    nblk = seq // POOL_HALO
    row = pl.BlockSpec((tm, D_MODEL), lambda i: (i, 0))
    return pl.pallas_call(
        functools.partial(_post_kernel, tm=tm, seq=seq),
        grid=(seq // tm,),
        in_specs=[row,
                  pl.BlockSpec((MLA_WIDTH, tm), lambda i: (0, i)),
                  pl.BlockSpec((tm, POOL_WIDTH), lambda i: (i, 0)),
                  pl.BlockSpec((POOL_HALO, POOL_WIDTH), lambda i: (jnp.maximum(i * per - 1, 0), 0)),
                  pl.BlockSpec((POOL_HALO, POOL_WIDTH), lambda i: (jnp.minimum((i + 1) * per, nblk - 1), 0)),
                  ] + [_layer_spec(a, layer) for a in (w_pool, pool_scale, w_out_a, w_out_b)],
        out_specs=row,
        out_shape=jax.ShapeDtypeStruct(x.shape, F32),
        scratch_shapes=[pltpu.VMEM((tm + 4 * POOL_HALO, POOL_WIDTH), F32),
                        pltpu.VMEM((tm + 3 * POOL_HALO, POOL_WIDTH - POOL_GROUP_DIM), F32),
                        pltpu.VMEM((tm + 2 * POOL_HALO, POOL_WIDTH - 2 * POOL_GROUP_DIM), F32)],
        compiler_params=_params("parallel"),
        name="post",
    )(x, o, zp, zp, zp, w_pool, pool_scale, w_out_a, w_out_b)


def _head_rows(nope, rope):
    pad = jnp.zeros(nope.shape[:-2] + (HEAD_PAD - QK_HEAD, nope.shape[-1]), nope.dtype)
    return jnp.concatenate([nope, rope, pad], axis=-2)


def _layout_weights(w_in, w_uq, w_uk, w_uv, q_norm, k_norm, w_out):
    depth = w_in.shape[0]
    w_in, w_uq, w_uk, w_uv, w_out = (w.astype(BF16) for w in (w_in, w_uq, w_uk, w_uv, w_out))
    o_pe = Q_LORA + KV_LORA
    o_pool = o_pe + QK_ROPE
    w_in_r = jnp.concatenate([w_in[..., :o_pe], w_in[..., o_pool:]], axis=-1)
    w_pe = jnp.swapaxes(w_in[..., o_pe:o_pool], 1, 2)
    w_pe_t = _head_rows(jnp.zeros((depth, QK_NOPE, D_MODEL), BF16), w_pe)

    uq = jnp.transpose(w_uq.reshape(depth, Q_LORA, MLA_HEADS, QK_HEAD), (0, 2, 3, 1))
    w_uq_t = _head_rows(uq[..., :QK_NOPE, :], uq[..., QK_NOPE:, :]).reshape(depth, HEADS_W, Q_LORA)
    uk = jnp.transpose(w_uk.reshape(depth, KV_LORA, MLA_HEADS, QK_NOPE), (0, 2, 3, 1))
    w_uk_t = jnp.pad(uk, ((0, 0), (0, 0), (0, HEAD_PAD - QK_NOPE), (0, 0))).reshape(depth, HEADS_W, KV_LORA)
    uvt = jnp.transpose(w_uv.reshape(depth, KV_LORA, MLA_HEADS, V_HEAD), (0, 2, 3, 1))
    w_uvt_r = uvt.reshape(depth, MLA_HEADS * V_HEAD, KV_LORA)

    def gain_cols(g):
        nope, rope = g[:, :QK_NOPE], g[:, QK_NOPE:]
        swapped = jnp.concatenate([rope[:, HALF_ROPE:], rope[:, :HALF_ROPE]], axis=-1)
        pad = jnp.zeros((depth, HEAD_PAD - QK_HEAD), g.dtype)
        return (jnp.concatenate([nope, rope, pad], axis=-1),
                jnp.concatenate([jnp.zeros_like(nope), swapped, pad], axis=-1))

    cols = gain_cols(q_norm) + gain_cols(k_norm)
    gains = jnp.stack(cols + (jnp.zeros_like(cols[0]),) * 4, axis=-1)
    return w_in_r, w_pe_t, w_uq_t, w_uk_t, w_uvt_r, gains, w_out[:, :MLA_WIDTH], w_out[:, MLA_WIDTH:]


def _rope_tables(seq):
    inv = ROPE_THETA ** (-jnp.arange(0, QK_ROPE, 2, dtype=F32) / QK_ROPE)
    off = jnp.zeros((QK_NOPE,), F32)
    pad = jnp.zeros((HEAD_PAD - QK_HEAD,), F32)
    inv_row = jnp.concatenate([off, inv, inv, pad])
    sign = jnp.concatenate([off, -jnp.ones_like(inv), jnp.ones_like(inv), pad])
    ang = inv_row[:, None] * jnp.arange(seq, dtype=F32)[None, :]
    return jnp.cos(ang), jnp.sin(ang) * sign[:, None]


def _tiles(seq):
    tk = min(2048, seq // 2)
    return dict(tm=min(1024, tk), tq=min(1024, seq), tk=tk, tf=4 * MXU_TILE)


def kernel(x, ffn1_norm, ffn1_w_gu, ffn1_w_down, mix_norm, w_in, q_lat_norm, kv_lat_norm, w_uq, w_uk, w_uv,
           q_norm, k_norm, w_pool, pool_scale, w_out, ffn2_norm, ffn2_w_gu, ffn2_w_down):
    batch, seq, _ = x.shape
    depth = w_in.shape[0]
    t = _tiles(seq)
    cos, sin = _rope_tables(seq)
    w_in_r, w_pe_t, w_uq_t, w_uk_t, w_uvt_r, gains, w_out_a, w_out_b = _layout_weights(
        w_in, w_uq, w_uk, w_uv, q_norm, k_norm, w_out)
    gu, down = ffn1_w_gu[:1].astype(BF16), ffn1_w_down[:1].astype(BF16)
    w_pool_b = w_pool.astype(BF16)
    ffn1_g, ffn2_g, mix_g = ffn1_norm[:, None], ffn2_norm[:, None], mix_norm[:, None]
    q_lat_g, kv_lat_g, pool_g = q_lat_norm[:, None], kv_lat_norm[:, None], pool_scale[:, None]

    outs = []
    for b in range(batch):
        xb = x[b]
        for l in range(depth):
            xb, gu, down = _ffn(xb, ffn1_g, l, gu, down, (ffn2_w_gu, ffn2_w_down, l), tm=t["tm"], tf=t["tf"])
            q, k, vt, zp = _pre(xb, mix_g, w_in_r, w_pe_t, q_lat_g, kv_lat_g, w_uq_t, w_uk_t, w_uvt_r, gains,
                                cos, sin, l, tm=t["tm"], tk=t["tk"])
            o = lax.cond(_score_bound(q_norm[l], k_norm[l]) <= MAX_UNSHIFTED_SCORE,
                         functools.partial(_attn, tq=t["tq"], tk=t["tk"], shifted=False),
                         functools.partial(_attn, tq=t["tq"], tk=t["tk"], shifted=True),
                         q, k, vt)
            xb = _post(xb, o, zp, w_pool_b, pool_g, w_out_a, w_out_b, l, tm=t["tm"])
            if l + 1 < depth or b + 1 < batch:
                nxt = (ffn1_w_gu, ffn1_w_down, (l + 1) % depth)
                xb, gu, down = _ffn(xb, ffn2_g, l, gu, down, nxt, tm=t["tm"], tf=t["tf"])
            else:
                xb = _ffn(xb, ffn2_g, l, gu, down, None, tm=t["tm"], tf=t["tf"])
        outs.append(xb)
    return outs[0][None] if batch == 1 else jnp.stack(outs, axis=0)
```

```python
import functools

import jax
import jax.numpy as jnp
from jax import lax
from jax.experimental import pallas as pl
from jax.experimental.pallas import tpu as pltpu

D_MODEL = 1024
MLA_HEADS = 8
QK_NOPE = 64
QK_ROPE = 32
QK_HEAD = QK_NOPE + QK_ROPE
V_HEAD = 64
Q_LORA = 384
KV_LORA = 256
MLA_WIDTH = MLA_HEADS * V_HEAD
ROPE_THETA = 10000.0
POOL_WINDOWS = (2, 4, 8, 16)
POOL_GROUP_DIM = 128
POOL_WIDTH = 512
D_FF = 2816
EPS = 1e-6
LOG2_E = 1.4426950408889634
MAX_UNSHIFTED_SCORE = 60.0
BF16_NORM_SLACK = 1.01

LANES = 128
SUBLANES = 8
BF16_ROWS = 16
HEAD_PAD = LANES
HEADS_W = MLA_HEADS * HEAD_PAD
HALF_ROPE = QK_ROPE // 2
POOL_HALO = 8
MXU_TILE = 256
KSUB = MXU_TILE
CHUNKS_PER_TRIP = 8
QSPLIT = 2
AHEAD = 1
RING = 2 * AHEAD
C_Q, C_KV, C_POOL = 0, Q_LORA, Q_LORA + KV_LORA
D_IN_R = C_POOL + POOL_WIDTH

VMEM_LIMIT = 56 * 1024 * 1024

F32 = jnp.float32
BF16 = jnp.bfloat16


def _rmsnorm(x, g):
    ms = jnp.mean(x * x, axis=-1, keepdims=True)
    return x * lax.rsqrt(ms + EPS) * g


def _dot(a, b):
    return jnp.dot(a, b, preferred_element_type=F32)


def _layer_spec(stacked, layer):
    tail = stacked.shape[1:]
    return pl.BlockSpec((None,) + tail, lambda *_: (layer,) + (0,) * len(tail), pipeline_mode=pl.Buffered(1))


def _params(*sem):
    return pltpu.CompilerParams(dimension_semantics=sem, vmem_limit_bytes=VMEM_LIMIT)


def _ffn_kernel(x_ref, g_ref, wgu_ref, wd_ref, *rest, tf, casts_next):
    if casts_next:
        ngu_ref, nd_ref, o_ref, ngu_out, nd_out = rest
        ngu_out[...] = ngu_ref[...].astype(BF16)
        nd_out[...] = nd_ref[...].astype(BF16)
    else:
        (o_ref,) = rest
    x = x_ref[...]
    h = _rmsnorm(x, g_ref[...]).astype(BF16)
    acc = jnp.zeros(x.shape, F32)
    for lo in range(0, D_FF, tf):
        hi = min(lo + tf, D_FF)
        gate = _dot(h, wgu_ref[:, lo:hi])
        up = _dot(h, wgu_ref[:, D_FF + lo:D_FF + hi])
        act = (gate * jax.nn.sigmoid(gate) * up).astype(BF16)
        acc = acc + _dot(act, wd_ref[lo:hi, :])
    o_ref[...] = x + 0.5 * acc


def _ffn(x, gain, layer, w_gu, w_down, nxt, *, tm, tf):
    seq = x.shape[0]
    steps = seq // tm
    assert seq % tm == 0 and tf % MXU_TILE == 0
    row = pl.BlockSpec((tm, D_MODEL), lambda i: (i, 0))
    in_specs = [row, _layer_spec(gain, layer), _layer_spec(w_gu, 0), _layer_spec(w_down, 0)]
    out_specs, out_shape, args = [row], [jax.ShapeDtypeStruct(x.shape, F32)], [x, gain, w_gu, w_down]
    if nxt is not None:
        n_gu, n_down, n_layer = nxt
        for w in (n_gu, n_down):
            rows, cols = w.shape[1] // steps, w.shape[2]
            assert w.shape[1] % steps == 0 and rows % BF16_ROWS == 0
            in_specs.append(pl.BlockSpec((None, rows, cols), lambda i: (n_layer, i, 0)))
            out_specs.append(pl.BlockSpec((None, rows, cols), lambda i: (0, i, 0)))
            out_shape.append(jax.ShapeDtypeStruct((1,) + w.shape[1:], BF16))
            args.append(w)
    out = pl.pallas_call(
        functools.partial(_ffn_kernel, tf=tf, casts_next=nxt is not None),
        grid=(steps,),
        in_specs=in_specs,
        out_specs=out_specs,
        out_shape=out_shape,
        compiler_params=_params("parallel"),
        name="ffn",
    )(*args)
    return out if nxt is not None else out[0]


def _swap_rope_rows(t):
    return jnp.concatenate([t[HALF_ROPE:], t[:HALF_ROPE]], axis=0)


def _rope_t(rope, main, swap):
    return rope * main[QK_NOPE:QK_HEAD] + _swap_rope_rows(rope) * swap[QK_NOPE:QK_HEAD]


def _head_t(nope, rope_rot, sumsq, main):
    r = lax.rsqrt(sumsq * (1.0 / QK_HEAD) + EPS)
    pad = jnp.zeros((HEAD_PAD - QK_HEAD, nope.shape[1]), F32)
    return jnp.concatenate([nope * main[:QK_NOPE] * r, rope_rot * r, pad], axis=0)


def _sumsq(t):
    return jnp.sum(t * t, axis=0, keepdims=True)


def _pre_kernel(x_ref, g_ref, win_ref, wpe_ref, qlat_ref, kvlat_ref, wuq_ref, wuk_ref, wuvt_ref, gains_ref,
                cos_ref, sin_ref, qt_ref, k_ref, vt_ref, zp_ref):
    nt = (((1,), (1,)), ((), ()))
    h = _rmsnorm(x_ref[...], g_ref[...]).astype(BF16)
    z = _dot(h, win_ref[...])
    cq = _rmsnorm(z[:, C_Q:C_KV], qlat_ref[...]).astype(BF16)
    ckv = _rmsnorm(z[:, C_KV:C_POOL], kvlat_ref[...]).astype(BF16)
    zp_ref[...] = z[:, C_POOL:D_IN_R]
    pe_t = lax.dot_general(wpe_ref[...], h, nt, preferred_element_type=F32)
    qq_t = lax.dot_general(wuq_ref[...], cq, nt, preferred_element_type=F32)
    kk_t = lax.dot_general(wuk_ref[...], ckv, nt, preferred_element_type=F32)
    vt = lax.dot_general(wuvt_ref[...], ckv, nt, preferred_element_type=F32)
    cos = cos_ref[...]
    sin = sin_ref[...]
    scale = QK_HEAD ** -0.5 * LOG2_E
    q_main, q_swap = gains_ref[:, 0:1] * cos * scale, gains_ref[:, 1:2] * sin * scale
    k_main, k_swap = gains_ref[:, 2:3] * cos, gains_ref[:, 3:4] * sin
    pe_rot, pe_sumsq = _rope_t(pe_t, k_main, k_swap), _sumsq(pe_t)
    for hd in range(MLA_HEADS):
        q = qq_t[hd * QK_HEAD:(hd + 1) * QK_HEAD]
        q_nope, q_rope = q[:QK_NOPE], q[QK_NOPE:]
        q_head = _head_t(q_nope, _rope_t(q_rope, q_main, q_swap), _sumsq(q), q_main)
        qt_ref[hd * HEAD_PAD:(hd + 1) * HEAD_PAD, :] = q_head.astype(BF16)
        k_nope = kk_t[hd * QK_NOPE:(hd + 1) * QK_NOPE]
        k_head = _head_t(k_nope, pe_rot, _sumsq(k_nope) + pe_sumsq, k_main)
        k_ref[:, hd * HEAD_PAD:(hd + 1) * HEAD_PAD] = k_head.T.astype(BF16)
        vt_ref[hd] = vt[hd * V_HEAD:(hd + 1) * V_HEAD, :].astype(BF16)


def _pre(x, gain, w_in_r, w_pe_t, q_lat, kv_lat, w_uq_t, w_uk_t, w_uvt, gains, cos, sin, layer, *, tm, tk):
    seq = x.shape[0]
    assert seq % tk == 0 and tk % tm == 0
    per = tk // tm
    row = pl.BlockSpec((tm, D_MODEL), lambda i: (i, 0))
    tab = pl.BlockSpec((HEAD_PAD, tm), lambda i: (0, i))
    vt_spec = pl.BlockSpec((MLA_HEADS, None, V_HEAD, tm), lambda i: (0, i // per, 0, i % per))
    params = (gain, w_in_r, w_pe_t, q_lat, kv_lat, w_uq_t, w_uk_t, w_uvt, gains)
    return pl.pallas_call(
        _pre_kernel,
        grid=(seq // tm,),
        in_specs=[row] + [_layer_spec(a, layer) for a in params] + [tab, tab],
        out_specs=[pl.BlockSpec((HEADS_W, tm), lambda i: (0, i)), pl.BlockSpec((tm, HEADS_W), lambda i: (i, 0)),
                   vt_spec, pl.BlockSpec((tm, POOL_WIDTH), lambda i: (i, 0))],
        out_shape=[jax.ShapeDtypeStruct((HEADS_W, seq), BF16), jax.ShapeDtypeStruct((seq, HEADS_W), BF16),
                   jax.ShapeDtypeStruct((MLA_HEADS, seq // tk, V_HEAD, tk), BF16),
                   jax.ShapeDtypeStruct((seq, POOL_WIDTH), F32)],
        compiler_params=_params("parallel"),
        name="pre",
    )(x, *params, cos, sin)


def _fold_rows(p):
    return jnp.sum(p.reshape(p.shape[0] // SUBLANES, SUBLANES, p.shape[1]), axis=0)


def _store_heads(o_ref, acc, den):
    o_ref[...] = (acc / jnp.sum(den, axis=0, keepdims=True)).astype(BF16)


def _attn_kernel(q_ref, k_ref, vt_ref, o_ref, s_ref, *, tk):
    tq = q_ref.shape[1]
    nk = k_ref.shape[0] // tk
    nsub = tk // KSUB
    qt = q_ref[...]

    def scores_sub(j, c, slot, cmax):
        kc = k_ref[pl.ds(pl.multiple_of(j * tk + c * KSUB, KSUB), KSUB), :]
        st = _dot(kc, qt)
        s_ref[slot, c * KSUB:(c + 1) * KSUB, :] = st
        return jnp.maximum(cmax, jnp.max(st.reshape(KSUB // SUBLANES, SUBLANES, tq), axis=0))

    def step(j, slot, cmax, m, den, acc, j_next):
        m_new = jnp.maximum(m, jnp.max(cmax, axis=0, keepdims=True))
        alpha = jnp.exp2(m - m_new)
        den, acc = alpha * den, alpha * acc
        cmax_next = jnp.full((SUBLANES, tq), -jnp.inf, F32)
        for c in range(nsub):
            cmax_next = scores_sub(j_next, c, 1 - slot, cmax_next)
            p = jnp.exp2(s_ref[slot, c * KSUB:(c + 1) * KSUB, :] - m_new)
            den = den + _fold_rows(p)
            acc = acc + _dot(vt_ref[j, :, c * KSUB:(c + 1) * KSUB], p.astype(BF16))
        return cmax_next, m_new, den, acc

    def pair(jj, carry):
        j = 2 * jj
        carry = step(j, 0, *carry, j + 1)
        return step(j + 1, 1, *carry, jnp.minimum(j + 2, nk - 1))

    cmax0 = jnp.full((SUBLANES, tq), -jnp.inf, F32)
    for c in range(nsub):
        cmax0 = scores_sub(0, c, 0, cmax0)
    m0 = jnp.full((1, tq), -jnp.inf, F32)
    den0 = jnp.zeros((SUBLANES, tq), F32)
    acc0 = jnp.zeros((V_HEAD, tq), F32)
    _, _, den, acc = lax.fori_loop(0, nk // 2, pair, (cmax0, m0, den0, acc0))
    _store_heads(o_ref, acc, den)


def _attn_unshifted_kernel(q_ref, k_ref, vt_ref, o_ref, s_ref, *, tk):
    tq = q_ref.shape[1]
    nk = k_ref.shape[0] // tk
    nsub = tk // KSUB
    last = k_ref.shape[0] // KSUB - 1
    qt = q_ref[...]

    nq = tq // QSPLIT

    def scores_sub(g, slot, h):
        start = g * KSUB if isinstance(g, int) else pl.multiple_of(g * KSUB, KSUB)
        s_ref[slot, :, h * nq:(h + 1) * nq] = _dot(k_ref[pl.ds(start, KSUB), :], qt[:, h * nq:(h + 1) * nq])

    per = min(CHUNKS_PER_TRIP, nk)
    trips = nk // per

    def trip(jj, carry):
        dens, accs = list(carry[0]), list(carry[1])
        for u in range(per * nsub):
            j, c = jj * per + u // nsub, u % nsub
            ahead = j * nsub + c + AHEAD
            for h in range(QSPLIT):
                if trips > 1:
                    scores_sub(jnp.minimum(ahead, last), (u + AHEAD) % RING, h)
                elif ahead <= last:
                    scores_sub(ahead, (u + AHEAD) % RING, h)
                p = jnp.exp2(s_ref[u % RING, :, h * nq:(h + 1) * nq])
                dens[h] = dens[h] + _fold_rows(p)
                accs[h] = accs[h] + _dot(vt_ref[j, :, c * KSUB:(c + 1) * KSUB], p.astype(BF16))
        return tuple(dens), tuple(accs)

    for g in range(AHEAD):
        for h in range(QSPLIT):
            scores_sub(g, g % RING, h)
    init = (tuple(jnp.zeros((SUBLANES, nq), F32) for _ in range(QSPLIT)),
            tuple(jnp.zeros((V_HEAD, nq), F32) for _ in range(QSPLIT)))
    dens, accs = trip(0, init) if trips == 1 else lax.fori_loop(0, trips, trip, init)
    _store_heads(o_ref, jnp.concatenate(accs, axis=1), jnp.concatenate(dens, axis=1))


def _attn(q, k, vt, *, tq, tk, shifted):
    seq = q.shape[1]
    nk = seq // tk
    assert seq % tq == 0 and seq % tk == 0 and nk % 2 == 0 and nk % min(CHUNKS_PER_TRIP, nk) == 0
    assert tk % (KSUB * RING) == 0 and tq % (QSPLIT * LANES) == 0
    qspec = pl.BlockSpec((HEAD_PAD, tq), lambda h, i: (h, i))
    kspec = pl.BlockSpec((seq, HEAD_PAD), lambda h, i: (0, h))
    vspec = pl.BlockSpec((None, seq // tk, V_HEAD, tk), lambda h, i: (h, 0, 0, 0))
    body = _attn_kernel if shifted else _attn_unshifted_kernel
    return pl.pallas_call(
        functools.partial(body, tk=tk),
        grid=(MLA_HEADS, seq // tq),
        in_specs=[qspec, kspec, vspec],
        out_specs=pl.BlockSpec((V_HEAD, tq), lambda h, i: (h, i)),
        out_shape=jax.ShapeDtypeStruct((MLA_WIDTH, seq), BF16),
        scratch_shapes=[pltpu.VMEM((2, tk, tq) if shifted else (RING, KSUB, tq), F32)],
        compiler_params=_params("parallel", "parallel"),
        name="attn" if shifted else "attn_unshifted",
    )(q, k, vt)


def _score_bound(q_gain, k_gain):
    return (LOG2_E * QK_HEAD ** 0.5) * jnp.max(jnp.abs(q_gain)) * jnp.max(jnp.abs(k_gain)) * BF16_NORM_SLACK


def _post_kernel(x_ref, o_ref, zc_ref, zprev_ref, znext_ref, wpool_ref, ps_ref, woa_ref, wob_ref,
                 out_ref, ext_ref, p2_ref, p4_ref, *, tm, seq):
    i = pl.program_id(0)
    last = pl.num_programs(0) - 1
    zc = zc_ref[...]
    gd, h = POOL_GROUP_DIM, POOL_HALO
    ext_ref[0:h, :] = jnp.where(i > 0, zprev_ref[...], 0.0)
    ext_ref[h:h + tm, :] = zc
    ext_ref[h + tm:2 * h + tm, :] = jnp.where(i < last, znext_ref[...], 0.0)
    ext_ref[2 * h + tm:, :] = jnp.zeros((2 * h, POOL_WIDTH), F32)
    n2, n4, n8 = tm + 3 * h, tm + 2 * h, tm + h
    p2_ref[...] = ext_ref[0:n2, gd:] + ext_ref[1:n2 + 1, gd:]
    p4_ref[...] = p2_ref[0:n4, gd:] + p2_ref[2:n4 + 2, gd:]
    p8 = p4_ref[0:n8, gd:] + p4_ref[4:n8 + 4, gd:]
    wsums = (ext_ref[h - 1:h - 1 + tm, 0:gd] + ext_ref[h:h + tm, 0:gd],
             p2_ref[h - 2:h - 2 + tm, 0:gd] + p2_ref[h:h + tm, 0:gd],
             p4_ref[h - 4:h - 4 + tm, 0:gd] + p4_ref[h:h + tm, 0:gd],
             p8[0:tm] + p8[h:h + tm])
    pos = i * tm + lax.broadcasted_iota(jnp.int32, (tm, 1), 0)
    parts = []
    for g, w in enumerate(POOL_WINDOWS):
        left = w // 2
        right = w - 1 - left
        c0 = g * gd
        cnt = (jnp.minimum(pos + right + 1, seq) - jnp.maximum(pos - left, 0)).astype(F32)
        mixed = (wsums[g] / cnt - zc[:, c0:c0 + gd]).astype(BF16)
        y = _dot(mixed, wpool_ref[g]) * ps_ref[:, c0:c0 + gd]
        parts.append(y.astype(BF16))
    b = jnp.concatenate(parts, axis=1)
    attn_part = lax.dot_general(o_ref[...], woa_ref[...], (((0,), (0,)), ((), ())), preferred_element_type=F32)
    out_ref[...] = x_ref[...] + (attn_part + _dot(b, wob_ref[...]))


def _post(x, o, zp, w_pool, pool_scale, w_out_a, w_out_b, layer, *, tm):
    seq = x.shape[0]
    assert seq % tm == 0 and tm % POOL_HALO == 0 and POOL_WINDOWS == (2, 4, 8, 16)
    per = tm // POOL_HALO
    nblk = seq // POOL_HALO
    row = pl.BlockSpec((tm, D_MODEL), lambda i: (i, 0))
    return pl.pallas_call(
        functools.partial(_post_kernel, tm=tm, seq=seq),
        grid=(seq // tm,),
        in_specs=[row,
                  pl.BlockSpec((MLA_WIDTH, tm), lambda i: (0, i)),
                  pl.BlockSpec((tm, POOL_WIDTH), lambda i: (i, 0)),
                  pl.BlockSpec((POOL_HALO, POOL_WIDTH), lambda i: (jnp.maximum(i * per - 1, 0), 0)),
                  pl.BlockSpec((POOL_HALO, POOL_WIDTH), lambda i: (jnp.minimum((i + 1) * per, nblk - 1), 0)),
                  ] + [_layer_spec(a, layer) for a in (w_pool, pool_scale, w_out_a, w_out_b)],
        out_specs=row,
        out_shape=jax.ShapeDtypeStruct(x.shape, F32),
        scratch_shapes=[pltpu.VMEM((tm + 4 * POOL_HALO, POOL_WIDTH), F32),
                        pltpu.VMEM((tm + 3 * POOL_HALO, POOL_WIDTH - POOL_GROUP_DIM), F32),
                        pltpu.VMEM((tm + 2 * POOL_HALO, POOL_WIDTH - 2 * POOL_GROUP_DIM), F32)],
        compiler_params=_params("parallel"),
        name="post",
    )(x, o, zp, zp, zp, w_pool, pool_scale, w_out_a, w_out_b)


def _head_rows(nope, rope):
    pad = jnp.zeros(nope.shape[:-2] + (HEAD_PAD - QK_HEAD, nope.shape[-1]), nope.dtype)
    return jnp.concatenate([nope, rope, pad], axis=-2)


def _layout_weights(w_in, w_uq, w_uk, w_uv, q_norm, k_norm, w_out):
    depth = w_in.shape[0]
    w_in, w_uq, w_uk, w_uv, w_out = (w.astype(BF16) for w in (w_in, w_uq, w_uk, w_uv, w_out))
    o_pe = Q_LORA + KV_LORA
    o_pool = o_pe + QK_ROPE
    w_in_r = jnp.concatenate([w_in[..., :o_pe], w_in[..., o_pool:]], axis=-1)
    w_pe_t = jnp.swapaxes(w_in[..., o_pe:o_pool], 1, 2)
    uq = jnp.transpose(w_uq.reshape(depth, Q_LORA, MLA_HEADS, QK_HEAD), (0, 2, 3, 1))
    w_uq_t = uq.reshape(depth, MLA_HEADS * QK_HEAD, Q_LORA)
    uk = jnp.transpose(w_uk.reshape(depth, KV_LORA, MLA_HEADS, QK_NOPE), (0, 2, 3, 1))
    w_uk_t = uk.reshape(depth, MLA_HEADS * QK_NOPE, KV_LORA)
    uvt = jnp.transpose(w_uv.reshape(depth, KV_LORA, MLA_HEADS, V_HEAD), (0, 2, 3, 1))
    w_uvt_r = uvt.reshape(depth, MLA_HEADS * V_HEAD, KV_LORA)

    def gain_cols(g):
        nope, rope = g[:, :QK_NOPE], g[:, QK_NOPE:]
        swapped = jnp.concatenate([rope[:, HALF_ROPE:], rope[:, :HALF_ROPE]], axis=-1)
        pad = jnp.zeros((depth, HEAD_PAD - QK_HEAD), g.dtype)
        return (jnp.concatenate([nope, rope, pad], axis=-1),
                jnp.concatenate([jnp.zeros_like(nope), swapped, pad], axis=-1))

    cols = gain_cols(q_norm) + gain_cols(k_norm)
    gains = jnp.stack(cols + (jnp.zeros_like(cols[0]),) * 4, axis=-1)
    return w_in_r, w_pe_t, w_uq_t, w_uk_t, w_uvt_r, gains, w_out[:, :MLA_WIDTH], w_out[:, MLA_WIDTH:]


def _rope_tables(seq):
    inv = ROPE_THETA ** (-jnp.arange(0, QK_ROPE, 2, dtype=F32) / QK_ROPE)
    ang = jnp.concatenate([inv, inv])[:, None] * jnp.arange(seq, dtype=F32)[None, :]
    sign = jnp.concatenate([-jnp.ones_like(inv), jnp.ones_like(inv)])[:, None]
    ones, zeros = jnp.ones((QK_NOPE, seq), F32), jnp.zeros((QK_NOPE, seq), F32)
    return _head_rows(ones, jnp.cos(ang)), _head_rows(zeros, jnp.sin(ang) * sign)


def _tiles(seq):
    tk = min(2048, seq // 2)
    return dict(tm=min(1024, tk), tq=min(1024, seq), tk=tk, tf=4 * MXU_TILE)


def kernel(x, ffn1_norm, ffn1_w_gu, ffn1_w_down, mix_norm, w_in, q_lat_norm, kv_lat_norm, w_uq, w_uk, w_uv,
           q_norm, k_norm, w_pool, pool_scale, w_out, ffn2_norm, ffn2_w_gu, ffn2_w_down):
    batch, seq, _ = x.shape
    depth = w_in.shape[0]
    t = _tiles(seq)
    cos, sin = _rope_tables(seq)
    w_in_r, w_pe_t, w_uq_t, w_uk_t, w_uvt_r, gains, w_out_a, w_out_b = _layout_weights(
        w_in, w_uq, w_uk, w_uv, q_norm, k_norm, w_out)
    gu, down = ffn1_w_gu[:1].astype(BF16), ffn1_w_down[:1].astype(BF16)
    w_pool_b = w_pool.astype(BF16)
    ffn1_g, ffn2_g, mix_g = ffn1_norm[:, None], ffn2_norm[:, None], mix_norm[:, None]
    q_lat_g, kv_lat_g, pool_g = q_lat_norm[:, None], kv_lat_norm[:, None], pool_scale[:, None]

    outs = []
    for b in range(batch):
        xb = x[b]
        for l in range(depth):
            xb, gu, down = _ffn(xb, ffn1_g, l, gu, down, (ffn2_w_gu, ffn2_w_down, l), tm=t["tm"], tf=t["tf"])
            q, k, vt, zp = _pre(xb, mix_g, w_in_r, w_pe_t, q_lat_g, kv_lat_g, w_uq_t, w_uk_t, w_uvt_r, gains,
                                cos, sin, l, tm=t["tm"], tk=t["tk"])
            o = lax.cond(_score_bound(q_norm[l], k_norm[l]) <= MAX_UNSHIFTED_SCORE,
                         functools.partial(_attn, tq=t["tq"], tk=t["tk"], shifted=False),
                         functools.partial(_attn, tq=t["tq"], tk=t["tk"], shifted=True),
                         q, k, vt)
            xb = _post(xb, o, zp, w_pool_b, pool_g, w_out_a, w_out_b, l, tm=t["tm"])
            if l + 1 < depth or b + 1 < batch:
                nxt = (ffn1_w_gu, ffn1_w_down, (l + 1) % depth)
                xb, gu, down = _ffn(xb, ffn2_g, l, gu, down, nxt, tm=t["tm"], tf=t["tf"])
            else:
                xb = _ffn(xb, ffn2_g, l, gu, down, None, tm=t["tm"], tf=t["tf"])
        outs.append(xb)
    return outs[0][None] if batch == 1 else jnp.stack(outs, axis=0)
```

```python
import functools

import jax
import jax.numpy as jnp
from jax import lax
from jax.experimental import pallas as pl
from jax.experimental.pallas import tpu as pltpu

D_MODEL = 1024
MLA_HEADS = 8
QK_NOPE = 64
QK_ROPE = 32
QK_HEAD = QK_NOPE + QK_ROPE
V_HEAD = 64
Q_LORA = 384
KV_LORA = 256
MLA_WIDTH = MLA_HEADS * V_HEAD
ROPE_THETA = 10000.0
POOL_WINDOWS = (2, 4, 8, 16)
POOL_GROUP_DIM = 128
POOL_WIDTH = 512
D_FF = 2816
EPS = 1e-6
LOG2_E = 1.4426950408889634
MAX_UNSHIFTED_SCORE = 60.0
BF16_NORM_SLACK = 1.01

LANES = 128
SUBLANES = 8
BF16_ROWS = 16
HEAD_PAD = LANES
HEADS_W = MLA_HEADS * HEAD_PAD
HALF_ROPE = QK_ROPE // 2
POOL_HALO = 8
MXU_TILE = 256
KSUB = MXU_TILE
CHUNKS_PER_TRIP = 8
QSPLIT = 2
AHEAD = 1
RING = 2 * AHEAD
C_Q, C_KV, C_POOL = 0, Q_LORA, Q_LORA + KV_LORA
D_IN_R = C_POOL + POOL_WIDTH

VMEM_LIMIT = 56 * 1024 * 1024

F32 = jnp.float32
BF16 = jnp.bfloat16


def _rmsnorm(x, g):
    ms = jnp.mean(x * x, axis=-1, keepdims=True)
    return x * lax.rsqrt(ms + EPS) * g


def _dot(a, b):
    return jnp.dot(a, b, preferred_element_type=F32)


def _layer_spec(stacked, layer):
    tail = stacked.shape[1:]
    return pl.BlockSpec((None,) + tail, lambda *_: (layer,) + (0,) * len(tail), pipeline_mode=pl.Buffered(1))


def _params(*sem):
    return pltpu.CompilerParams(dimension_semantics=sem, vmem_limit_bytes=VMEM_LIMIT)


def _ffn_kernel(x_ref, g_ref, wgu_ref, wd_ref, *rest, tf, casts_next):
    if casts_next:
        ngu_ref, nd_ref, o_ref, ngu_out, nd_out = rest
        ngu_out[...] = ngu_ref[...].astype(BF16)
        nd_out[...] = nd_ref[...].astype(BF16)
    else:
        (o_ref,) = rest
    x = x_ref[...]
    h = _rmsnorm(x, g_ref[...]).astype(BF16)
    acc = jnp.zeros(x.shape, F32)
    for lo in range(0, D_FF, tf):
        hi = min(lo + tf, D_FF)
        gate = _dot(h, wgu_ref[:, lo:hi])
        up = _dot(h, wgu_ref[:, D_FF + lo:D_FF + hi])
        act = (gate * jax.nn.sigmoid(gate) * up).astype(BF16)
        acc = acc + _dot(act, wd_ref[lo:hi, :])
    o_ref[...] = x + 0.5 * acc


def _ffn(x, gain, layer, w_gu, w_down, nxt, *, tm, tf):
    seq = x.shape[0]
    steps = seq // tm
    assert seq % tm == 0 and tf % MXU_TILE == 0
    row = pl.BlockSpec((tm, D_MODEL), lambda i: (i, 0))
    in_specs = [row, _layer_spec(gain, layer), _layer_spec(w_gu, 0), _layer_spec(w_down, 0)]
    out_specs, out_shape, args = [row], [jax.ShapeDtypeStruct(x.shape, F32)], [x, gain, w_gu, w_down]
    if nxt is not None:
        n_gu, n_down, n_layer = nxt
        for w in (n_gu, n_down):
            rows, cols = w.shape[1] // steps, w.shape[2]
            assert w.shape[1] % steps == 0 and rows % BF16_ROWS == 0
            in_specs.append(pl.BlockSpec((None, rows, cols), lambda i: (n_layer, i, 0)))
            out_specs.append(pl.BlockSpec((None, rows, cols), lambda i: (0, i, 0)))
            out_shape.append(jax.ShapeDtypeStruct((1,) + w.shape[1:], BF16))
            args.append(w)
    out = pl.pallas_call(
        functools.partial(_ffn_kernel, tf=tf, casts_next=nxt is not None),
        grid=(steps,),
        in_specs=in_specs,
        out_specs=out_specs,
        out_shape=out_shape,
        compiler_params=_params("parallel"),
        name="ffn",
    )(*args)
    return out if nxt is not None else out[0]


def _swap_rope_rows(t):
    return jnp.concatenate([t[HALF_ROPE:], t[:HALF_ROPE]], axis=0)


def _rope_t(rope, main, swap):
    return rope * main[QK_NOPE:QK_HEAD] + _swap_rope_rows(rope) * swap[QK_NOPE:QK_HEAD]


def _head_t(nope, rope_rot, sumsq, main):
    r = lax.rsqrt(sumsq * (1.0 / QK_HEAD) + EPS)
    pad = jnp.zeros((HEAD_PAD - QK_HEAD, nope.shape[1]), F32)
    return jnp.concatenate([nope * main[:QK_NOPE] * r, rope_rot * r, pad], axis=0)


def _sumsq(t):
    return jnp.sum(t * t, axis=0, keepdims=True)


def _pre_kernel(x_ref, g_ref, win_ref, wpe_ref, qlat_ref, kvlat_ref, wuq_ref, wuk_ref, wuvt_ref, gains_ref,
                cos_ref, sin_ref, qt_ref, k_ref, vt_ref, zp_ref):
    nt = (((1,), (1,)), ((), ()))
    h = _rmsnorm(x_ref[...], g_ref[...]).astype(BF16)
    z = _dot(h, win_ref[...])
    cq = _rmsnorm(z[:, C_Q:C_KV], qlat_ref[...]).astype(BF16)
    ckv = _rmsnorm(z[:, C_KV:C_POOL], kvlat_ref[...]).astype(BF16)
    zp_ref[...] = z[:, C_POOL:D_IN_R]
    pe_t = lax.dot_general(wpe_ref[...], h, nt, preferred_element_type=F32)
    qq_t = lax.dot_general(wuq_ref[...], cq, nt, preferred_element_type=F32)
    kk_t = lax.dot_general(wuk_ref[...], ckv, nt, preferred_element_type=F32)
    vt = lax.dot_general(wuvt_ref[...], ckv, nt, preferred_element_type=F32)
    cos = cos_ref[...]
    sin = sin_ref[...]
    scale = QK_HEAD ** -0.5 * LOG2_E
    q_main, q_swap = gains_ref[:, 0:1] * cos * scale, gains_ref[:, 1:2] * sin * scale
    k_main, k_swap = gains_ref[:, 2:3] * cos, gains_ref[:, 3:4] * sin
    pe_rot, pe_sumsq = _rope_t(pe_t, k_main, k_swap), _sumsq(pe_t)
    for hd in range(MLA_HEADS):
        q = qq_t[hd * QK_HEAD:(hd + 1) * QK_HEAD]
        q_nope, q_rope = q[:QK_NOPE], q[QK_NOPE:]
        q_head = _head_t(q_nope, _rope_t(q_rope, q_main, q_swap), _sumsq(q), q_main)
        qt_ref[hd * HEAD_PAD:(hd + 1) * HEAD_PAD, :] = q_head.astype(BF16)
        k_nope = kk_t[hd * QK_NOPE:(hd + 1) * QK_NOPE]
        k_head = _head_t(k_nope, pe_rot, _sumsq(k_nope) + pe_sumsq, k_main)
        k_ref[:, hd * HEAD_PAD:(hd + 1) * HEAD_PAD] = k_head.T.astype(BF16)
        vt_ref[hd] = vt[hd * V_HEAD:(hd + 1) * V_HEAD, :].astype(BF16)


def _pre(x, gain, w_in_r, w_pe_t, q_lat, kv_lat, w_uq_t, w_uk_t, w_uvt, gains, cos, sin, layer, *, tm, tk):
    seq = x.shape[0]
    assert seq % tk == 0 and tk % tm == 0
    per = tk // tm
    row = pl.BlockSpec((tm, D_MODEL), lambda i: (i, 0))
    tab = pl.BlockSpec((HEAD_PAD, tm), lambda i: (0, i))
    vt_spec = pl.BlockSpec((MLA_HEADS, None, V_HEAD, tm), lambda i: (0, i // per, 0, i % per))
    params = (gain, w_in_r, w_pe_t, q_lat, kv_lat, w_uq_t, w_uk_t, w_uvt, gains)
    return pl.pallas_call(
        _pre_kernel,
        grid=(seq // tm,),
        in_specs=[row] + [_layer_spec(a, layer) for a in params] + [tab, tab],
        out_specs=[pl.BlockSpec((HEADS_W, tm), lambda i: (0, i)), pl.BlockSpec((tm, HEADS_W), lambda i: (i, 0)),
                   vt_spec, pl.BlockSpec((tm, POOL_WIDTH), lambda i: (i, 0))],
        out_shape=[jax.ShapeDtypeStruct((HEADS_W, seq), BF16), jax.ShapeDtypeStruct((seq, HEADS_W), BF16),
                   jax.ShapeDtypeStruct((MLA_HEADS, seq // tk, V_HEAD, tk), BF16),
                   jax.ShapeDtypeStruct((seq, POOL_WIDTH), F32)],
        compiler_params=_params("parallel"),
        name="pre",
    )(x, *params, cos, sin)


def _fold_rows(p):
    return jnp.sum(p.reshape(p.shape[0] // SUBLANES, SUBLANES, p.shape[1]), axis=0)


def _store_heads(o_ref, acc, den):
    o_ref[...] = (acc / jnp.sum(den, axis=0, keepdims=True)).astype(BF16)


def _attn_kernel(q_ref, k_ref, vt_ref, o_ref, s_ref, *, tk):
    tq = q_ref.shape[1]
    nk = k_ref.shape[0] // tk
    nsub = tk // KSUB
    qt = q_ref[...]

    def scores_sub(j, c, slot, cmax):
        kc = k_ref[pl.ds(pl.multiple_of(j * tk + c * KSUB, KSUB), KSUB), :]
        st = _dot(kc, qt)
        s_ref[slot, c * KSUB:(c + 1) * KSUB, :] = st
        return jnp.maximum(cmax, jnp.max(st.reshape(KSUB // SUBLANES, SUBLANES, tq), axis=0))

    def step(j, slot, cmax, m, den, acc, j_next):
        m_new = jnp.maximum(m, jnp.max(cmax, axis=0, keepdims=True))
        alpha = jnp.exp2(m - m_new)
        den, acc = alpha * den, alpha * acc
        cmax_next = jnp.full((SUBLANES, tq), -jnp.inf, F32)
        for c in range(nsub):
            cmax_next = scores_sub(j_next, c, 1 - slot, cmax_next)
            p = jnp.exp2(s_ref[slot, c * KSUB:(c + 1) * KSUB, :] - m_new)
            den = den + _fold_rows(p)
            acc = acc + _dot(vt_ref[j, :, c * KSUB:(c + 1) * KSUB], p.astype(BF16))
        return cmax_next, m_new, den, acc

    def pair(jj, carry):
        j = 2 * jj
        carry = step(j, 0, *carry, j + 1)
        return step(j + 1, 1, *carry, jnp.minimum(j + 2, nk - 1))

    cmax0 = jnp.full((SUBLANES, tq), -jnp.inf, F32)
    for c in range(nsub):
        cmax0 = scores_sub(0, c, 0, cmax0)
    m0 = jnp.full((1, tq), -jnp.inf, F32)
    den0 = jnp.zeros((SUBLANES, tq), F32)
    acc0 = jnp.zeros((V_HEAD, tq), F32)
    _, _, den, acc = lax.fori_loop(0, nk // 2, pair, (cmax0, m0, den0, acc0))
    _store_heads(o_ref, acc, den)


def _attn_unshifted_kernel(q_ref, k_ref, vt_ref, o_ref, s_ref, *, tk):
    tq = q_ref.shape[1]
    nk = k_ref.shape[0] // tk
    nsub = tk // KSUB
    last = k_ref.shape[0] // KSUB - 1
    qt = q_ref[...]

    nq = tq // QSPLIT

    def scores_sub(g, slot, h):
        start = g * KSUB if isinstance(g, int) else pl.multiple_of(g * KSUB, KSUB)
        s_ref[slot, :, h * nq:(h + 1) * nq] = _dot(k_ref[pl.ds(start, KSUB), :], qt[:, h * nq:(h + 1) * nq])

    per = min(CHUNKS_PER_TRIP, nk)
    trips = nk // per

    def trip(jj, carry):
        dens, accs = list(carry[0]), list(carry[1])
        for u in range(per * nsub):
            j, c = jj * per + u // nsub, u % nsub
            ahead = j * nsub + c + AHEAD
            for h in range(QSPLIT):
                if trips > 1:
                    scores_sub(jnp.minimum(ahead, last), (u + AHEAD) % RING, h)
                elif ahead <= last:
                    scores_sub(ahead, (u + AHEAD) % RING, h)
                p = jnp.exp2(s_ref[u % RING, :, h * nq:(h + 1) * nq])
                dens[h] = dens[h] + _fold_rows(p)
                accs[h] = accs[h] + _dot(vt_ref[j, :, c * KSUB:(c + 1) * KSUB], p.astype(BF16))
        return tuple(dens), tuple(accs)

    for g in range(AHEAD):
        for h in range(QSPLIT):
            scores_sub(g, g % RING, h)
    init = (tuple(jnp.zeros((SUBLANES, nq), F32) for _ in range(QSPLIT)),
            tuple(jnp.zeros((V_HEAD, nq), F32) for _ in range(QSPLIT)))
    dens, accs = trip(0, init) if trips == 1 else lax.fori_loop(0, trips, trip, init)
    _store_heads(o_ref, jnp.concatenate(accs, axis=1), jnp.concatenate(dens, axis=1))


def _attn(q, k, vt, *, tq, tk, shifted):
    seq = q.shape[1]
    nk = seq // tk
    assert seq % tq == 0 and seq % tk == 0 and nk % 2 == 0 and nk % min(CHUNKS_PER_TRIP, nk) == 0
    assert tk % (KSUB * RING) == 0 and tq % (QSPLIT * LANES) == 0
    qspec = pl.BlockSpec((HEAD_PAD, tq), lambda h, i: (h, i))
    kspec = pl.BlockSpec((seq, HEAD_PAD), lambda h, i: (0, h))
    vspec = pl.BlockSpec((None, seq // tk, V_HEAD, tk), lambda h, i: (h, 0, 0, 0))
    body = _attn_kernel if shifted else _attn_unshifted_kernel
    return pl.pallas_call(
        functools.partial(body, tk=tk),
        grid=(MLA_HEADS, seq // tq),
        in_specs=[qspec, kspec, vspec],
        out_specs=pl.BlockSpec((V_HEAD, tq), lambda h, i: (h, i)),
        out_shape=jax.ShapeDtypeStruct((MLA_WIDTH, seq), BF16),
        scratch_shapes=[pltpu.VMEM((2, tk, tq) if shifted else (RING, KSUB, tq), F32)],
        compiler_params=_params("parallel", "parallel"),
        name="attn" if shifted else "attn_unshifted",
    )(q, k, vt)


def _score_bound(q_gain, k_gain):
    return (LOG2_E * QK_HEAD ** 0.5) * jnp.max(jnp.abs(q_gain)) * jnp.max(jnp.abs(k_gain)) * BF16_NORM_SLACK


def _post_kernel(x_ref, o_ref, zc_ref, zprev_ref, znext_ref, icnt_ref, wpool_ref, ps_ref, woa_ref, wob_ref,
                 out_ref, ext_ref, p2_ref, p4_ref, *, tm):
    i = pl.program_id(0)
    last = pl.num_programs(0) - 1
    zc = zc_ref[...]
    gd, h = POOL_GROUP_DIM, POOL_HALO
    ext_ref[0:h, :] = jnp.where(i > 0, zprev_ref[...], 0.0)
    ext_ref[h:h + tm, :] = zc
    ext_ref[h + tm:2 * h + tm, :] = jnp.where(i < last, znext_ref[...], 0.0)
    ext_ref[2 * h + tm:, :] = jnp.zeros((2 * h, POOL_WIDTH), F32)
    n2, n4, n8 = tm + 3 * h, tm + 2 * h, tm + h
    p2_ref[...] = ext_ref[0:n2, gd:] + ext_ref[1:n2 + 1, gd:]
    p4_ref[...] = p2_ref[0:n4, gd:] + p2_ref[2:n4 + 2, gd:]
    p8 = p4_ref[0:n8, gd:] + p4_ref[4:n8 + 4, gd:]
    wsums = (ext_ref[h - 1:h - 1 + tm, 0:gd] + ext_ref[h:h + tm, 0:gd],
             p2_ref[h - 2:h - 2 + tm, 0:gd] + p2_ref[h:h + tm, 0:gd],
             p4_ref[h - 4:h - 4 + tm, 0:gd] + p4_ref[h:h + tm, 0:gd],
             p8[0:tm] + p8[h:h + tm])
    parts = []
    for g in range(len(POOL_WINDOWS)):
        cols = slice(g * gd, (g + 1) * gd)
        mixed = (wsums[g] * icnt_ref[:, cols] - zc[:, cols]).astype(BF16)
        parts.append((_dot(mixed, wpool_ref[g]) * ps_ref[:, cols]).astype(BF16))
    b = jnp.concatenate(parts, axis=1)
    attn_part = lax.dot_general(o_ref[...], woa_ref[...], (((0,), (0,)), ((), ())), preferred_element_type=F32)
    out_ref[...] = x_ref[...] + (attn_part + _dot(b, wob_ref[...]))


def _post(x, o, zp, inv_count, w_pool, pool_scale, w_out_a, w_out_b, layer, *, tm):
    seq = x.shape[0]
    assert seq % tm == 0 and tm % POOL_HALO == 0 and POOL_WINDOWS == (2, 4, 8, 16)
    per = tm // POOL_HALO
    nblk = seq // POOL_HALO
    row = pl.BlockSpec((tm, D_MODEL), lambda i: (i, 0))
    return pl.pallas_call(
        functools.partial(_post_kernel, tm=tm),
        grid=(seq // tm,),
        in_specs=[row,
                  pl.BlockSpec((MLA_WIDTH, tm), lambda i: (0, i)),
                  pl.BlockSpec((tm, POOL_WIDTH), lambda i: (i, 0)),
                  pl.BlockSpec((POOL_HALO, POOL_WIDTH), lambda i: (jnp.maximum(i * per - 1, 0), 0)),
                  pl.BlockSpec((POOL_HALO, POOL_WIDTH), lambda i: (jnp.minimum((i + 1) * per, nblk - 1), 0)),
                  pl.BlockSpec((tm, POOL_WIDTH), lambda i: (i, 0)),
                  ] + [_layer_spec(a, layer) for a in (w_pool, pool_scale, w_out_a, w_out_b)],
        out_specs=row,
        out_shape=jax.ShapeDtypeStruct(x.shape, F32),
        scratch_shapes=[pltpu.VMEM((tm + 4 * POOL_HALO, POOL_WIDTH), F32),
                        pltpu.VMEM((tm + 3 * POOL_HALO, POOL_WIDTH - POOL_GROUP_DIM), F32),
                        pltpu.VMEM((tm + 2 * POOL_HALO, POOL_WIDTH - 2 * POOL_GROUP_DIM), F32)],
        compiler_params=_params("parallel"),
        name="post",
    )(x, o, zp, zp, zp, inv_count, w_pool, pool_scale, w_out_a, w_out_b)


def _head_rows(nope, rope):
    pad = jnp.zeros(nope.shape[:-2] + (HEAD_PAD - QK_HEAD, nope.shape[-1]), nope.dtype)
    return jnp.concatenate([nope, rope, pad], axis=-2)


def _layout_weights(w_in, w_uq, w_uk, w_uv, q_norm, k_norm, w_out):
    depth = w_in.shape[0]
    w_in, w_uq, w_uk, w_uv, w_out = (w.astype(BF16) for w in (w_in, w_uq, w_uk, w_uv, w_out))
    o_pe = Q_LORA + KV_LORA
    o_pool = o_pe + QK_ROPE
    w_in_r = jnp.concatenate([w_in[..., :o_pe], w_in[..., o_pool:]], axis=-1)
    w_pe_t = jnp.swapaxes(w_in[..., o_pe:o_pool], 1, 2)
    uq = jnp.transpose(w_uq.reshape(depth, Q_LORA, MLA_HEADS, QK_HEAD), (0, 2, 3, 1))
    w_uq_t = uq.reshape(depth, MLA_HEADS * QK_HEAD, Q_LORA)
    uk = jnp.transpose(w_uk.reshape(depth, KV_LORA, MLA_HEADS, QK_NOPE), (0, 2, 3, 1))
    w_uk_t = uk.reshape(depth, MLA_HEADS * QK_NOPE, KV_LORA)
    uvt = jnp.transpose(w_uv.reshape(depth, KV_LORA, MLA_HEADS, V_HEAD), (0, 2, 3, 1))
    w_uvt_r = uvt.reshape(depth, MLA_HEADS * V_HEAD, KV_LORA)

    def gain_cols(g):
        nope, rope = g[:, :QK_NOPE], g[:, QK_NOPE:]
        swapped = jnp.concatenate([rope[:, HALF_ROPE:], rope[:, :HALF_ROPE]], axis=-1)
        pad = jnp.zeros((depth, HEAD_PAD - QK_HEAD), g.dtype)
        return (jnp.concatenate([nope, rope, pad], axis=-1),
                jnp.concatenate([jnp.zeros_like(nope), swapped, pad], axis=-1))

    cols = gain_cols(q_norm) + gain_cols(k_norm)
    gains = jnp.stack(cols + (jnp.zeros_like(cols[0]),) * 4, axis=-1)
    return w_in_r, w_pe_t, w_uq_t, w_uk_t, w_uvt_r, gains, w_out[:, :MLA_WIDTH], w_out[:, MLA_WIDTH:]


def _rope_tables(seq):
    inv = ROPE_THETA ** (-jnp.arange(0, QK_ROPE, 2, dtype=F32) / QK_ROPE)
    ang = jnp.concatenate([inv, inv])[:, None] * jnp.arange(seq, dtype=F32)[None, :]
    sign = jnp.concatenate([-jnp.ones_like(inv), jnp.ones_like(inv)])[:, None]
    ones, zeros = jnp.ones((QK_NOPE, seq), F32), jnp.zeros((QK_NOPE, seq), F32)
    return _head_rows(ones, jnp.cos(ang)), _head_rows(zeros, jnp.sin(ang) * sign)


def _pool_inv_counts(seq):
    pos = jnp.arange(seq)
    cols = []
    for w in POOL_WINDOWS:
        left = w // 2
        right = w - 1 - left
        cnt = jnp.minimum(pos + right + 1, seq) - jnp.maximum(pos - left, 0)
        cols.append(jnp.broadcast_to((1.0 / cnt.astype(F32))[:, None], (seq, POOL_GROUP_DIM)))
    return jnp.concatenate(cols, axis=1)


def _tiles(seq):
    tk = min(2048, seq // 2)
    return dict(tm=min(1024, tk), tq=min(1024, seq), tk=tk, tf=4 * MXU_TILE)


def kernel(x, ffn1_norm, ffn1_w_gu, ffn1_w_down, mix_norm, w_in, q_lat_norm, kv_lat_norm, w_uq, w_uk, w_uv,
           q_norm, k_norm, w_pool, pool_scale, w_out, ffn2_norm, ffn2_w_gu, ffn2_w_down):
    batch, seq, _ = x.shape
    depth = w_in.shape[0]
    t = _tiles(seq)
    cos, sin = _rope_tables(seq)
    inv_count = _pool_inv_counts(seq)
    w_in_r, w_pe_t, w_uq_t, w_uk_t, w_uvt_r, gains, w_out_a, w_out_b = _layout_weights(
        w_in, w_uq, w_uk, w_uv, q_norm, k_norm, w_out)
    gu, down = ffn1_w_gu[:1].astype(BF16), ffn1_w_down[:1].astype(BF16)
    w_pool_b = w_pool.astype(BF16)
    ffn1_g, ffn2_g, mix_g = ffn1_norm[:, None], ffn2_norm[:, None], mix_norm[:, None]
    q_lat_g, kv_lat_g, pool_g = q_lat_norm[:, None], kv_lat_norm[:, None], pool_scale[:, None]

    outs = []
    for b in range(batch):
        xb = x[b]
        for l in range(depth):
            xb, gu, down = _ffn(xb, ffn1_g, l, gu, down, (ffn2_w_gu, ffn2_w_down, l), tm=t["tm"], tf=t["tf"])
            q, k, vt, zp = _pre(xb, mix_g, w_in_r, w_pe_t, q_lat_g, kv_lat_g, w_uq_t, w_uk_t, w_uvt_r, gains,
                                cos, sin, l, tm=t["tm"], tk=t["tk"])
            o = lax.cond(_score_bound(q_norm[l], k_norm[l]) <= MAX_UNSHIFTED_SCORE,
                         functools.partial(_attn, tq=t["tq"], tk=t["tk"], shifted=False),
                         functools.partial(_attn, tq=t["tq"], tk=t["tk"], shifted=True),
                         q, k, vt)
            xb = _post(xb, o, zp, inv_count, w_pool_b, pool_g, w_out_a, w_out_b, l, tm=t["tm"])
            if l + 1 < depth or b + 1 < batch:
                nxt = (ffn1_w_gu, ffn1_w_down, (l + 1) % depth)
                xb, gu, down = _ffn(xb, ffn2_g, l, gu, down, nxt, tm=t["tm"], tf=t["tf"])
            else:
                xb = _ffn(xb, ffn2_g, l, gu, down, None, tm=t["tm"], tf=t["tf"])
        outs.append(xb)
    return outs[0][None] if batch == 1 else jnp.stack(outs, axis=0)
```

```python
import functools

import jax
import jax.numpy as jnp
from jax import lax
from jax.experimental import pallas as pl
from jax.experimental.pallas import tpu as pltpu

D_MODEL = 1024
MLA_HEADS = 8
QK_NOPE = 64
QK_ROPE = 32
QK_HEAD = QK_NOPE + QK_ROPE
V_HEAD = 64
Q_LORA = 384
KV_LORA = 256
MLA_WIDTH = MLA_HEADS * V_HEAD
ROPE_THETA = 10000.0
POOL_WINDOWS = (2, 4, 8, 16)
POOL_GROUP_DIM = 128
POOL_WIDTH = 512
D_FF = 2816
EPS = 1e-6
LOG2_E = 1.4426950408889634
MAX_UNSHIFTED_SCORE = 60.0
BF16_NORM_SLACK = 1.01

LANES = 128
SUBLANES = 8
BF16_ROWS = 16
HEAD_PAD = LANES
HEADS_W = MLA_HEADS * HEAD_PAD
HALF_ROPE = QK_ROPE // 2
POOL_HALO = 8
X_AHEAD = 2
X_SLOTS = X_AHEAD + 1
MXU_TILE = 256
KSUB = MXU_TILE
CHUNKS_PER_TRIP = 8
QSPLIT = 2
AHEAD = 1
RING = 2 * AHEAD
C_Q, C_KV, C_POOL = 0, Q_LORA, Q_LORA + KV_LORA
D_IN_R = C_POOL + POOL_WIDTH

VMEM_LIMIT = 56 * 1024 * 1024

F32 = jnp.float32
BF16 = jnp.bfloat16


def _rmsnorm(x, g):
    ms = jnp.mean(x * x, axis=-1, keepdims=True)
    return x * lax.rsqrt(ms + EPS) * g


def _dot(a, b):
    return jnp.dot(a, b, preferred_element_type=F32)


def _layer_spec(stacked, layer):
    tail = stacked.shape[1:]
    return pl.BlockSpec((None,) + tail, lambda *_: (layer,) + (0,) * len(tail), pipeline_mode=pl.Buffered(1))


def _params(*sem):
    return pltpu.CompilerParams(dimension_semantics=sem, vmem_limit_bytes=VMEM_LIMIT)


def _ffn_kernel(x_ref, g_ref, wgu_ref, wd_ref, *rest, tf, casts_next):
    if casts_next:
        ngu_ref, nd_ref, o_ref, ngu_out, nd_out = rest
        ngu_out[...] = ngu_ref[...].astype(BF16)
        nd_out[...] = nd_ref[...].astype(BF16)
    else:
        (o_ref,) = rest
    x = x_ref[...]
    h = _rmsnorm(x, g_ref[...]).astype(BF16)
    acc = jnp.zeros(x.shape, F32)
    for lo in range(0, D_FF, tf):
        hi = min(lo + tf, D_FF)
        gate = _dot(h, wgu_ref[:, lo:hi])
        up = _dot(h, wgu_ref[:, D_FF + lo:D_FF + hi])
        act = (gate * jax.nn.sigmoid(gate) * up).astype(BF16)
        acc = acc + _dot(act, wd_ref[lo:hi, :])
    o_ref[...] = x + 0.5 * acc


def _ffn(x, gain, layer, w_gu, w_down, nxt, *, tm, tf):
    seq = x.shape[0]
    steps = seq // tm
    assert seq % tm == 0 and tf % MXU_TILE == 0
    row = pl.BlockSpec((tm, D_MODEL), lambda i: (i, 0))
    in_specs = [row, _layer_spec(gain, layer), _layer_spec(w_gu, 0), _layer_spec(w_down, 0)]
    out_specs, out_shape, args = [row], [jax.ShapeDtypeStruct(x.shape, F32)], [x, gain, w_gu, w_down]
    if nxt is not None:
        n_gu, n_down, n_layer = nxt
        for w in (n_gu, n_down):
            rows, cols = w.shape[1] // steps, w.shape[2]
            assert w.shape[1] % steps == 0 and rows % BF16_ROWS == 0
            in_specs.append(pl.BlockSpec((None, rows, cols), lambda i: (n_layer, i, 0)))
            out_specs.append(pl.BlockSpec((None, rows, cols), lambda i: (0, i, 0)))
            out_shape.append(jax.ShapeDtypeStruct((1,) + w.shape[1:], BF16))
            args.append(w)
    out = pl.pallas_call(
        functools.partial(_ffn_kernel, tf=tf, casts_next=nxt is not None),
        grid=(steps,),
        in_specs=in_specs,
        out_specs=out_specs,
        out_shape=out_shape,
        compiler_params=_params("parallel"),
        name="ffn",
    )(*args)
    return out if nxt is not None else out[0]


def _swap_rope_rows(t):
    return jnp.concatenate([t[HALF_ROPE:], t[:HALF_ROPE]], axis=0)


def _rope_t(rope, main, swap):
    return rope * main[QK_NOPE:QK_HEAD] + _swap_rope_rows(rope) * swap[QK_NOPE:QK_HEAD]


def _head_t(nope, rope_rot, sumsq, main):
    r = lax.rsqrt(sumsq * (1.0 / QK_HEAD) + EPS)
    pad = jnp.zeros((HEAD_PAD - QK_HEAD, nope.shape[1]), F32)
    return jnp.concatenate([nope * main[:QK_NOPE] * r, rope_rot * r, pad], axis=0)


def _sumsq(t):
    return jnp.sum(t * t, axis=0, keepdims=True)


def _pre_kernel(x_ref, g_ref, win_ref, wpe_ref, qlat_ref, kvlat_ref, wuq_ref, wuk_ref, wuvt_ref, gains_ref,
                cos_ref, sin_ref, qt_ref, k_ref, vt_ref, zp_ref):
    nt = (((1,), (1,)), ((), ()))
    h = _rmsnorm(x_ref[...], g_ref[...]).astype(BF16)
    z = _dot(h, win_ref[...])
    cq = _rmsnorm(z[:, C_Q:C_KV], qlat_ref[...]).astype(BF16)
    ckv = _rmsnorm(z[:, C_KV:C_POOL], kvlat_ref[...]).astype(BF16)
    zp_ref[...] = z[:, C_POOL:D_IN_R]
    pe_t = lax.dot_general(wpe_ref[...], h, nt, preferred_element_type=F32)
    qq_t = lax.dot_general(wuq_ref[...], cq, nt, preferred_element_type=F32)
    kk_t = lax.dot_general(wuk_ref[...], ckv, nt, preferred_element_type=F32)
    vt = lax.dot_general(wuvt_ref[...], ckv, nt, preferred_element_type=F32)
    cos = cos_ref[...]
    sin = sin_ref[...]
    scale = QK_HEAD ** -0.5 * LOG2_E
    q_main, q_swap = gains_ref[:, 0:1] * cos * scale, gains_ref[:, 1:2] * sin * scale
    k_main, k_swap = gains_ref[:, 2:3] * cos, gains_ref[:, 3:4] * sin
    pe_rot, pe_sumsq = _rope_t(pe_t, k_main, k_swap), _sumsq(pe_t)
    for hd in range(MLA_HEADS):
        q = qq_t[hd * QK_HEAD:(hd + 1) * QK_HEAD]
        q_nope, q_rope = q[:QK_NOPE], q[QK_NOPE:]
        q_head = _head_t(q_nope, _rope_t(q_rope, q_main, q_swap), _sumsq(q), q_main)
        qt_ref[hd * HEAD_PAD:(hd + 1) * HEAD_PAD, :] = q_head.astype(BF16)
        k_nope = kk_t[hd * QK_NOPE:(hd + 1) * QK_NOPE]
        k_head = _head_t(k_nope, pe_rot, _sumsq(k_nope) + pe_sumsq, k_main)
        k_ref[:, hd * HEAD_PAD:(hd + 1) * HEAD_PAD] = k_head.T.astype(BF16)
        vt_ref[hd] = vt[hd * V_HEAD:(hd + 1) * V_HEAD, :].astype(BF16)


def _pre(x, gain, w_in_r, w_pe_t, q_lat, kv_lat, w_uq_t, w_uk_t, w_uvt, gains, cos, sin, layer, *, tm, tk):
    seq = x.shape[0]
    assert seq % tk == 0 and tk % tm == 0
    per = tk // tm
    row = pl.BlockSpec((tm, D_MODEL), lambda i: (i, 0))
    tab = pl.BlockSpec((HEAD_PAD, tm), lambda i: (0, i))
    vt_spec = pl.BlockSpec((MLA_HEADS, None, V_HEAD, tm), lambda i: (0, i // per, 0, i % per))
    params = (gain, w_in_r, w_pe_t, q_lat, kv_lat, w_uq_t, w_uk_t, w_uvt, gains)
    return pl.pallas_call(
        _pre_kernel,
        grid=(seq // tm,),
        in_specs=[row] + [_layer_spec(a, layer) for a in params] + [tab, tab],
        out_specs=[pl.BlockSpec((HEADS_W, tm), lambda i: (0, i)), pl.BlockSpec((tm, HEADS_W), lambda i: (i, 0)),
                   vt_spec, pl.BlockSpec((tm, POOL_WIDTH), lambda i: (i, 0))],
        out_shape=[jax.ShapeDtypeStruct((HEADS_W, seq), BF16), jax.ShapeDtypeStruct((seq, HEADS_W), BF16),
                   jax.ShapeDtypeStruct((MLA_HEADS, seq // tk, V_HEAD, tk), BF16),
                   jax.ShapeDtypeStruct((seq, POOL_WIDTH), F32)],
        compiler_params=_params("parallel"),
        name="pre",
    )(x, *params, cos, sin)


def _fold_rows(p):
    return jnp.sum(p.reshape(p.shape[0] // SUBLANES, SUBLANES, p.shape[1]), axis=0)


def _store_heads(o_ref, acc, den):
    o_ref[...] = (acc / jnp.sum(den, axis=0, keepdims=True)).astype(BF16)


def _attn_kernel(q_ref, k_ref, vt_ref, o_ref, s_ref, *, tk):
    tq = q_ref.shape[1]
    nk = k_ref.shape[0] // tk
    nsub = tk // KSUB
    qt = q_ref[...]

    def scores_sub(j, c, slot, cmax):
        kc = k_ref[pl.ds(pl.multiple_of(j * tk + c * KSUB, KSUB), KSUB), :]
        st = _dot(kc, qt)
        s_ref[slot, c * KSUB:(c + 1) * KSUB, :] = st
        return jnp.maximum(cmax, jnp.max(st.reshape(KSUB // SUBLANES, SUBLANES, tq), axis=0))

    def step(j, slot, cmax, m, den, acc, j_next):
        m_new = jnp.maximum(m, jnp.max(cmax, axis=0, keepdims=True))
        alpha = jnp.exp2(m - m_new)
        den, acc = alpha * den, alpha * acc
        cmax_next = jnp.full((SUBLANES, tq), -jnp.inf, F32)
        for c in range(nsub):
            cmax_next = scores_sub(j_next, c, 1 - slot, cmax_next)
            p = jnp.exp2(s_ref[slot, c * KSUB:(c + 1) * KSUB, :] - m_new)
            den = den + _fold_rows(p)
            acc = acc + _dot(vt_ref[j, :, c * KSUB:(c + 1) * KSUB], p.astype(BF16))
        return cmax_next, m_new, den, acc

    def pair(jj, carry):
        j = 2 * jj
        carry = step(j, 0, *carry, j + 1)
        return step(j + 1, 1, *carry, jnp.minimum(j + 2, nk - 1))

    cmax0 = jnp.full((SUBLANES, tq), -jnp.inf, F32)
    for c in range(nsub):
        cmax0 = scores_sub(0, c, 0, cmax0)
    m0 = jnp.full((1, tq), -jnp.inf, F32)
    den0 = jnp.zeros((SUBLANES, tq), F32)
    acc0 = jnp.zeros((V_HEAD, tq), F32)
    _, _, den, acc = lax.fori_loop(0, nk // 2, pair, (cmax0, m0, den0, acc0))
    _store_heads(o_ref, acc, den)


def _attn_unshifted_kernel(q_ref, k_ref, vt_ref, o_ref, s_ref, *, tk):
    tq = q_ref.shape[1]
    nk = k_ref.shape[0] // tk
    nsub = tk // KSUB
    last = k_ref.shape[0] // KSUB - 1
    qt = q_ref[...]

    nq = tq // QSPLIT

    def scores_sub(g, slot, h):
        start = g * KSUB if isinstance(g, int) else pl.multiple_of(g * KSUB, KSUB)
        s_ref[slot, :, h * nq:(h + 1) * nq] = _dot(k_ref[pl.ds(start, KSUB), :], qt[:, h * nq:(h + 1) * nq])

    per = min(CHUNKS_PER_TRIP, nk)
    trips = nk // per

    def trip(jj, carry):
        dens, accs = list(carry[0]), list(carry[1])
        for u in range(per * nsub):
            j, c = jj * per + u // nsub, u % nsub
            ahead = j * nsub + c + AHEAD
            for h in range(QSPLIT):
                if trips > 1:
                    scores_sub(jnp.minimum(ahead, last), (u + AHEAD) % RING, h)
                elif ahead <= last:
                    scores_sub(ahead, (u + AHEAD) % RING, h)
                p = jnp.exp2(s_ref[u % RING, :, h * nq:(h + 1) * nq])
                dens[h] = dens[h] + _fold_rows(p)
                accs[h] = accs[h] + _dot(vt_ref[j, :, c * KSUB:(c + 1) * KSUB], p.astype(BF16))
        return tuple(dens), tuple(accs)

    for g in range(AHEAD):
        for h in range(QSPLIT):
            scores_sub(g, g % RING, h)
    init = (tuple(jnp.zeros((SUBLANES, nq), F32) for _ in range(QSPLIT)),
            tuple(jnp.zeros((V_HEAD, nq), F32) for _ in range(QSPLIT)))
    dens, accs = trip(0, init) if trips == 1 else lax.fori_loop(0, trips, trip, init)
    _store_heads(o_ref, jnp.concatenate(accs, axis=1), jnp.concatenate(dens, axis=1))


def _attn(q, k, vt, *, tq, tk, shifted):
    seq = q.shape[1]
    nk = seq // tk
    assert seq % tq == 0 and seq % tk == 0 and nk % 2 == 0 and nk % min(CHUNKS_PER_TRIP, nk) == 0
    assert tk % (KSUB * RING) == 0 and tq % (QSPLIT * LANES) == 0
    qspec = pl.BlockSpec((HEAD_PAD, tq), lambda h, i: (h, i))
    kspec = pl.BlockSpec((seq, HEAD_PAD), lambda h, i: (0, h))
    vspec = pl.BlockSpec((None, seq // tk, V_HEAD, tk), lambda h, i: (h, 0, 0, 0))
    body = _attn_kernel if shifted else _attn_unshifted_kernel
    return pl.pallas_call(
        functools.partial(body, tk=tk),
        grid=(MLA_HEADS, seq // tq),
        in_specs=[qspec, kspec, vspec],
        out_specs=pl.BlockSpec((V_HEAD, tq), lambda h, i: (h, i)),
        out_shape=jax.ShapeDtypeStruct((MLA_WIDTH, seq), BF16),
        scratch_shapes=[pltpu.VMEM((2, tk, tq) if shifted else (RING, KSUB, tq), F32)],
        compiler_params=_params("parallel", "parallel"),
        name="attn" if shifted else "attn_unshifted",
    )(q, k, vt)


def _score_bound(q_gain, k_gain):
    return (LOG2_E * QK_HEAD ** 0.5) * jnp.max(jnp.abs(q_gain)) * jnp.max(jnp.abs(k_gain)) * BF16_NORM_SLACK


def _post_kernel(x_hbm, o_ref, zc_ref, zprev_ref, znext_ref, wpool_ref, ps_ref, woa_ref, wob_ref,
                 out_ref, ext_ref, p2_ref, p4_ref, xring_ref, xsem, *, tm, seq):
    i = pl.program_id(0)
    steps = seq // tm
    last = steps - 1

    def x_copy(step, slot):
        return pltpu.make_async_copy(x_hbm.at[pl.ds(step * tm, tm), :], xring_ref.at[slot], xsem.at[slot])

    @pl.when(i == 0)
    def _():
        for s in range(min(X_AHEAD, steps)):
            x_copy(s, s).start()

    @pl.when(i + X_AHEAD < steps)
    def _():
        x_copy(i + X_AHEAD, (i + X_AHEAD) % X_SLOTS).start()

    slot = i % X_SLOTS
    x_copy(i, slot).wait()
    zc = zc_ref[...]
    gd, h = POOL_GROUP_DIM, POOL_HALO
    ext_ref[0:h, :] = jnp.where(i > 0, zprev_ref[...], 0.0)
    ext_ref[h:h + tm, :] = zc
    ext_ref[h + tm:2 * h + tm, :] = jnp.where(i < last, znext_ref[...], 0.0)
    ext_ref[2 * h + tm:, :] = jnp.zeros((2 * h, POOL_WIDTH), F32)
    n2, n4, n8 = tm + 3 * h, tm + 2 * h, tm + h
    p2_ref[...] = ext_ref[0:n2, gd:] + ext_ref[1:n2 + 1, gd:]
    p4_ref[...] = p2_ref[0:n4, gd:] + p2_ref[2:n4 + 2, gd:]
    p8 = p4_ref[0:n8, gd:] + p4_ref[4:n8 + 4, gd:]
    wsums = (ext_ref[h - 1:h - 1 + tm, 0:gd] + ext_ref[h:h + tm, 0:gd],
             p2_ref[h - 2:h - 2 + tm, 0:gd] + p2_ref[h:h + tm, 0:gd],
             p4_ref[h - 4:h - 4 + tm, 0:gd] + p4_ref[h:h + tm, 0:gd],
             p8[0:tm] + p8[h:h + tm])
    pos = i * tm + lax.broadcasted_iota(jnp.int32, (tm, 1), 0)
    parts = []
    for g, w in enumerate(POOL_WINDOWS):
        left = w // 2
        right = w - 1 - left
        c0 = g * gd
        cnt = (jnp.minimum(pos + right + 1, seq) - jnp.maximum(pos - left, 0)).astype(F32)
        mixed = (wsums[g] / cnt - zc[:, c0:c0 + gd]).astype(BF16)
        y = _dot(mixed, wpool_ref[g]) * ps_ref[:, c0:c0 + gd]
        parts.append(y.astype(BF16))
    b = jnp.concatenate(parts, axis=1)
    attn_part = lax.dot_general(o_ref[...], woa_ref[...], (((0,), (0,)), ((), ())), preferred_element_type=F32)
    out_ref[...] = xring_ref[slot] + (attn_part + _dot(b, wob_ref[...]))


def _post(x, o, zp, w_pool, pool_scale, w_out_a, w_out_b, layer, *, tm):
    seq = x.shape[0]
    assert seq % tm == 0 and tm % POOL_HALO == 0 and POOL_WINDOWS == (2, 4, 8, 16)
    per = tm // POOL_HALO
    nblk = seq // POOL_HALO
    row = pl.BlockSpec((tm, D_MODEL), lambda i: (i, 0))
    return pl.pallas_call(
        functools.partial(_post_kernel, tm=tm, seq=seq),
        grid=(seq // tm,),
        in_specs=[pl.BlockSpec(memory_space=pl.ANY),
                  pl.BlockSpec((MLA_WIDTH, tm), lambda i: (0, i)),
                  pl.BlockSpec((tm, POOL_WIDTH), lambda i: (i, 0)),
                  pl.BlockSpec((POOL_HALO, POOL_WIDTH), lambda i: (jnp.maximum(i * per - 1, 0), 0)),
                  pl.BlockSpec((POOL_HALO, POOL_WIDTH), lambda i: (jnp.minimum((i + 1) * per, nblk - 1), 0)),
                  ] + [_layer_spec(a, layer) for a in (w_pool, pool_scale, w_out_a, w_out_b)],
        out_specs=row,
        out_shape=jax.ShapeDtypeStruct(x.shape, F32),
        scratch_shapes=[pltpu.VMEM((tm + 4 * POOL_HALO, POOL_WIDTH), F32),
                        pltpu.VMEM((tm + 3 * POOL_HALO, POOL_WIDTH - POOL_GROUP_DIM), F32),
                        pltpu.VMEM((tm + 2 * POOL_HALO, POOL_WIDTH - 2 * POOL_GROUP_DIM), F32),
                        pltpu.VMEM((X_SLOTS, tm, D_MODEL), F32),
                        pltpu.SemaphoreType.DMA((X_SLOTS,))],
        compiler_params=_params("arbitrary"),
        name="post",
    )(x, o, zp, zp, zp, w_pool, pool_scale, w_out_a, w_out_b)


def _head_rows(nope, rope):
    pad = jnp.zeros(nope.shape[:-2] + (HEAD_PAD - QK_HEAD, nope.shape[-1]), nope.dtype)
    return jnp.concatenate([nope, rope, pad], axis=-2)


def _layout_weights(w_in, w_uq, w_uk, w_uv, q_norm, k_norm, w_out):
    depth = w_in.shape[0]
    w_in, w_uq, w_uk, w_uv, w_out = (w.astype(BF16) for w in (w_in, w_uq, w_uk, w_uv, w_out))
    o_pe = Q_LORA + KV_LORA
    o_pool = o_pe + QK_ROPE
    w_in_r = jnp.concatenate([w_in[..., :o_pe], w_in[..., o_pool:]], axis=-1)
    w_pe_t = jnp.swapaxes(w_in[..., o_pe:o_pool], 1, 2)
    uq = jnp.transpose(w_uq.reshape(depth, Q_LORA, MLA_HEADS, QK_HEAD), (0, 2, 3, 1))
    w_uq_t = uq.reshape(depth, MLA_HEADS * QK_HEAD, Q_LORA)
    uk = jnp.transpose(w_uk.reshape(depth, KV_LORA, MLA_HEADS, QK_NOPE), (0, 2, 3, 1))
    w_uk_t = uk.reshape(depth, MLA_HEADS * QK_NOPE, KV_LORA)
    uvt = jnp.transpose(w_uv.reshape(depth, KV_LORA, MLA_HEADS, V_HEAD), (0, 2, 3, 1))
    w_uvt_r = uvt.reshape(depth, MLA_HEADS * V_HEAD, KV_LORA)

    def gain_cols(g):
        nope, rope = g[:, :QK_NOPE], g[:, QK_NOPE:]
        swapped = jnp.concatenate([rope[:, HALF_ROPE:], rope[:, :HALF_ROPE]], axis=-1)
        pad = jnp.zeros((depth, HEAD_PAD - QK_HEAD), g.dtype)
        return (jnp.concatenate([nope, rope, pad], axis=-1),
                jnp.concatenate([jnp.zeros_like(nope), swapped, pad], axis=-1))

    cols = gain_cols(q_norm) + gain_cols(k_norm)
    gains = jnp.stack(cols + (jnp.zeros_like(cols[0]),) * 4, axis=-1)
    return w_in_r, w_pe_t, w_uq_t, w_uk_t, w_uvt_r, gains, w_out[:, :MLA_WIDTH], w_out[:, MLA_WIDTH:]


def _rope_tables(seq):
    inv = ROPE_THETA ** (-jnp.arange(0, QK_ROPE, 2, dtype=F32) / QK_ROPE)
    ang = jnp.concatenate([inv, inv])[:, None] * jnp.arange(seq, dtype=F32)[None, :]
    sign = jnp.concatenate([-jnp.ones_like(inv), jnp.ones_like(inv)])[:, None]
    ones, zeros = jnp.ones((QK_NOPE, seq), F32), jnp.zeros((QK_NOPE, seq), F32)
    return _head_rows(ones, jnp.cos(ang)), _head_rows(zeros, jnp.sin(ang) * sign)


def _tiles(seq):
    tk = min(2048, seq // 2)
    return dict(tm=min(1024, tk), tq=min(1024, seq), tk=tk, tf=4 * MXU_TILE)


def kernel(x, ffn1_norm, ffn1_w_gu, ffn1_w_down, mix_norm, w_in, q_lat_norm, kv_lat_norm, w_uq, w_uk, w_uv,
           q_norm, k_norm, w_pool, pool_scale, w_out, ffn2_norm, ffn2_w_gu, ffn2_w_down):
    batch, seq, _ = x.shape
    depth = w_in.shape[0]
    t = _tiles(seq)
    cos, sin = _rope_tables(seq)
    w_in_r, w_pe_t, w_uq_t, w_uk_t, w_uvt_r, gains, w_out_a, w_out_b = _layout_weights(
        w_in, w_uq, w_uk, w_uv, q_norm, k_norm, w_out)
    gu, down = ffn1_w_gu[:1].astype(BF16), ffn1_w_down[:1].astype(BF16)
    w_pool_b = w_pool.astype(BF16)
    ffn1_g, ffn2_g, mix_g = ffn1_norm[:, None], ffn2_norm[:, None], mix_norm[:, None]
    q_lat_g, kv_lat_g, pool_g = q_lat_norm[:, None], kv_lat_norm[:, None], pool_scale[:, None]

    outs = []
    for b in range(batch):
        xb = x[b]
        for l in range(depth):
            xb, gu, down = _ffn(xb, ffn1_g, l, gu, down, (ffn2_w_gu, ffn2_w_down, l), tm=t["tm"], tf=t["tf"])
            q, k, vt, zp = _pre(xb, mix_g, w_in_r, w_pe_t, q_lat_g, kv_lat_g, w_uq_t, w_uk_t, w_uvt_r, gains,
                                cos, sin, l, tm=t["tm"], tk=t["tk"])
            o = lax.cond(_score_bound(q_norm[l], k_norm[l]) <= MAX_UNSHIFTED_SCORE,
                         functools.partial(_attn, tq=t["tq"], tk=t["tk"], shifted=False),
                         functools.partial(_attn, tq=t["tq"], tk=t["tk"], shifted=True),
                         q, k, vt)
            xb = _post(xb, o, zp, w_pool_b, pool_g, w_out_a, w_out_b, l, tm=t["tm"])
            if l + 1 < depth or b + 1 < batch:
                nxt = (ffn1_w_gu, ffn1_w_down, (l + 1) % depth)
                xb, gu, down = _ffn(xb, ffn2_g, l, gu, down, nxt, tm=t["tm"], tf=t["tf"])
            else:
                xb = _ffn(xb, ffn2_g, l, gu, down, None, tm=t["tm"], tf=t["tf"])
        outs.append(xb)
    return outs[0][None] if batch == 1 else jnp.stack(outs, axis=0)
```

```python
import functools

import jax
import jax.numpy as jnp
from jax import lax
from jax.experimental import pallas as pl
from jax.experimental.pallas import tpu as pltpu

D_MODEL = 1024
MLA_HEADS = 8
QK_NOPE = 64
QK_ROPE = 32
QK_HEAD = QK_NOPE + QK_ROPE
V_HEAD = 64
Q_LORA = 384
KV_LORA = 256
MLA_WIDTH = MLA_HEADS * V_HEAD
ROPE_THETA = 10000.0
POOL_WINDOWS = (2, 4, 8, 16)
POOL_GROUP_DIM = 128
POOL_WIDTH = 512
D_FF = 2816
EPS = 1e-6
LOG2_E = 1.4426950408889634
MAX_UNSHIFTED_SCORE = 60.0
BF16_NORM_SLACK = 1.01

LANES = 128
SUBLANES = 8
BF16_ROWS = 16
HEAD_PAD = LANES
HEADS_W = MLA_HEADS * HEAD_PAD
HALF_ROPE = QK_ROPE // 2
POOL_HALO = 8
MXU_TILE = 256
KSUB = MXU_TILE
CHUNKS_PER_TRIP = 8
QBLOCK = 2 * MXU_TILE
AHEAD = 1
RING = 2 * AHEAD
C_Q, C_KV, C_POOL = 0, Q_LORA, Q_LORA + KV_LORA
D_IN_R = C_POOL + POOL_WIDTH

VMEM_LIMIT = 56 * 1024 * 1024

F32 = jnp.float32
BF16 = jnp.bfloat16


def _rmsnorm(x, g):
    ms = jnp.mean(x * x, axis=-1, keepdims=True)
    return x * lax.rsqrt(ms + EPS) * g


def _dot(a, b):
    return jnp.dot(a, b, preferred_element_type=F32)


def _layer_spec(stacked, layer):
    tail = stacked.shape[1:]
    return pl.BlockSpec((None,) + tail, lambda *_: (layer,) + (0,) * len(tail), pipeline_mode=pl.Buffered(1))


def _params(*sem):
    return pltpu.CompilerParams(dimension_semantics=sem, vmem_limit_bytes=VMEM_LIMIT)


def _ffn_kernel(x_ref, g_ref, wgu_ref, wd_ref, *rest, tf, casts_next):
    if casts_next:
        ngu_ref, nd_ref, o_ref, ngu_out, nd_out = rest
        ngu_out[...] = ngu_ref[...].astype(BF16)
        nd_out[...] = nd_ref[...].astype(BF16)
    else:
        (o_ref,) = rest
    x = x_ref[...]
    h = _rmsnorm(x, g_ref[...]).astype(BF16)
    acc = jnp.zeros(x.shape, F32)
    for lo in range(0, D_FF, tf):
        hi = min(lo + tf, D_FF)
        gate = _dot(h, wgu_ref[:, lo:hi])
        up = _dot(h, wgu_ref[:, D_FF + lo:D_FF + hi])
        act = (gate * jax.nn.sigmoid(gate) * up).astype(BF16)
        acc = acc + _dot(act, wd_ref[lo:hi, :])
    o_ref[...] = x + 0.5 * acc


def _ffn(x, gain, layer, w_gu, w_down, nxt, *, tm, tf):
    seq = x.shape[0]
    steps = seq // tm
    assert seq % tm == 0 and tf % MXU_TILE == 0
    row = pl.BlockSpec((tm, D_MODEL), lambda i: (i, 0))
    in_specs = [row, _layer_spec(gain, layer), _layer_spec(w_gu, 0), _layer_spec(w_down, 0)]
    out_specs, out_shape, args = [row], [jax.ShapeDtypeStruct(x.shape, F32)], [x, gain, w_gu, w_down]
    if nxt is not None:
        n_gu, n_down, n_layer = nxt
        for w in (n_gu, n_down):
            rows, cols = w.shape[1] // steps, w.shape[2]
            assert w.shape[1] % steps == 0 and rows % BF16_ROWS == 0
            in_specs.append(pl.BlockSpec((None, rows, cols), lambda i: (n_layer, i, 0)))
            out_specs.append(pl.BlockSpec((None, rows, cols), lambda i: (0, i, 0)))
            out_shape.append(jax.ShapeDtypeStruct((1,) + w.shape[1:], BF16))
            args.append(w)
    out = pl.pallas_call(
        functools.partial(_ffn_kernel, tf=tf, casts_next=nxt is not None),
        grid=(steps,),
        in_specs=in_specs,
        out_specs=out_specs,
        out_shape=out_shape,
        compiler_params=_params("parallel"),
        name="ffn",
    )(*args)
    return out if nxt is not None else out[0]


def _swap_rope_rows(t):
    return jnp.concatenate([t[HALF_ROPE:], t[:HALF_ROPE]], axis=0)


def _rope_t(rope, main, swap):
    return rope * main[QK_NOPE:QK_HEAD] + _swap_rope_rows(rope) * swap[QK_NOPE:QK_HEAD]


def _head_t(nope, rope_rot, sumsq, main):
    r = lax.rsqrt(sumsq * (1.0 / QK_HEAD) + EPS)
    pad = jnp.zeros((HEAD_PAD - QK_HEAD, nope.shape[1]), F32)
    return jnp.concatenate([nope * main[:QK_NOPE] * r, rope_rot * r, pad], axis=0)


def _sumsq(t):
    return jnp.sum(t * t, axis=0, keepdims=True)


def _pre_kernel(x_ref, g_ref, win_ref, wpe_ref, qlat_ref, kvlat_ref, wuq_ref, wuk_ref, wuvt_ref, gains_ref,
                cos_ref, sin_ref, qt_ref, k_ref, vt_ref, zp_ref):
    nt = (((1,), (1,)), ((), ()))
    h = _rmsnorm(x_ref[...], g_ref[...]).astype(BF16)
    z = _dot(h, win_ref[...])
    cq = _rmsnorm(z[:, C_Q:C_KV], qlat_ref[...]).astype(BF16)
    ckv = _rmsnorm(z[:, C_KV:C_POOL], kvlat_ref[...]).astype(BF16)
    zp_ref[...] = z[:, C_POOL:D_IN_R]
    pe_t = lax.dot_general(wpe_ref[...], h, nt, preferred_element_type=F32)
    qq_t = lax.dot_general(wuq_ref[...], cq, nt, preferred_element_type=F32)
    kk_t = lax.dot_general(wuk_ref[...], ckv, nt, preferred_element_type=F32)
    vt = lax.dot_general(wuvt_ref[...], ckv, nt, preferred_element_type=F32)
    cos = cos_ref[...]
    sin = sin_ref[...]
    scale = QK_HEAD ** -0.5 * LOG2_E
    q_main, q_swap = gains_ref[:, 0:1] * cos * scale, gains_ref[:, 1:2] * sin * scale
    k_main, k_swap = gains_ref[:, 2:3] * cos, gains_ref[:, 3:4] * sin
    pe_rot, pe_sumsq = _rope_t(pe_t, k_main, k_swap), _sumsq(pe_t)
    for hd in range(MLA_HEADS):
        q = qq_t[hd * QK_HEAD:(hd + 1) * QK_HEAD]
        q_nope, q_rope = q[:QK_NOPE], q[QK_NOPE:]
        q_head = _head_t(q_nope, _rope_t(q_rope, q_main, q_swap), _sumsq(q), q_main)
        qt_ref[hd * HEAD_PAD:(hd + 1) * HEAD_PAD, :] = q_head.astype(BF16)
        k_nope = kk_t[hd * QK_NOPE:(hd + 1) * QK_NOPE]
        k_head = _head_t(k_nope, pe_rot, _sumsq(k_nope) + pe_sumsq, k_main)
        k_ref[:, hd * HEAD_PAD:(hd + 1) * HEAD_PAD] = k_head.T.astype(BF16)
        vt_ref[hd] = vt[hd * V_HEAD:(hd + 1) * V_HEAD, :].astype(BF16)


def _pre(x, gain, w_in_r, w_pe_t, q_lat, kv_lat, w_uq_t, w_uk_t, w_uvt, gains, cos, sin, layer, *, tm, tk):
    seq = x.shape[0]
    assert seq % tk == 0 and tk % tm == 0
    per = tk // tm
    row = pl.BlockSpec((tm, D_MODEL), lambda i: (i, 0))
    tab = pl.BlockSpec((HEAD_PAD, tm), lambda i: (0, i))
    vt_spec = pl.BlockSpec((MLA_HEADS, None, V_HEAD, tm), lambda i: (0, i // per, 0, i % per))
    params = (gain, w_in_r, w_pe_t, q_lat, kv_lat, w_uq_t, w_uk_t, w_uvt, gains)
    return pl.pallas_call(
        _pre_kernel,
        grid=(seq // tm,),
        in_specs=[row] + [_layer_spec(a, layer) for a in params] + [tab, tab],
        out_specs=[pl.BlockSpec((HEADS_W, tm), lambda i: (0, i)), pl.BlockSpec((tm, HEADS_W), lambda i: (i, 0)),
                   vt_spec, pl.BlockSpec((tm, POOL_WIDTH), lambda i: (i, 0))],
        out_shape=[jax.ShapeDtypeStruct((HEADS_W, seq), BF16), jax.ShapeDtypeStruct((seq, HEADS_W), BF16),
                   jax.ShapeDtypeStruct((MLA_HEADS, seq // tk, V_HEAD, tk), BF16),
                   jax.ShapeDtypeStruct((seq, POOL_WIDTH), F32)],
        compiler_params=_params("parallel"),
        name="pre",
    )(x, *params, cos, sin)


def _fold_rows(p):
    return jnp.sum(p.reshape(p.shape[0] // SUBLANES, SUBLANES, p.shape[1]), axis=0)


def _store_heads(o_ref, acc, den):
    o_ref[...] = (acc / jnp.sum(den, axis=0, keepdims=True)).astype(BF16)


def _attn_kernel(q_ref, k_ref, vt_ref, o_ref, s_ref, *, tk):
    tq = q_ref.shape[1]
    nk = k_ref.shape[0] // tk
    nsub = tk // KSUB
    qt = q_ref[...]

    def scores_sub(j, c, slot, cmax):
        kc = k_ref[pl.ds(pl.multiple_of(j * tk + c * KSUB, KSUB), KSUB), :]
        st = _dot(kc, qt)
        s_ref[slot, c * KSUB:(c + 1) * KSUB, :] = st
        return jnp.maximum(cmax, jnp.max(st.reshape(KSUB // SUBLANES, SUBLANES, tq), axis=0))

    def step(j, slot, cmax, m, den, acc, j_next):
        m_new = jnp.maximum(m, jnp.max(cmax, axis=0, keepdims=True))
        alpha = jnp.exp2(m - m_new)
        den, acc = alpha * den, alpha * acc
        cmax_next = jnp.full((SUBLANES, tq), -jnp.inf, F32)
        for c in range(nsub):
            cmax_next = scores_sub(j_next, c, 1 - slot, cmax_next)
            p = jnp.exp2(s_ref[slot, c * KSUB:(c + 1) * KSUB, :] - m_new)
            den = den + _fold_rows(p)
            acc = acc + _dot(vt_ref[j, :, c * KSUB:(c + 1) * KSUB], p.astype(BF16))
        return cmax_next, m_new, den, acc

    def pair(jj, carry):
        j = 2 * jj
        carry = step(j, 0, *carry, j + 1)
        return step(j + 1, 1, *carry, jnp.minimum(j + 2, nk - 1))

    cmax0 = jnp.full((SUBLANES, tq), -jnp.inf, F32)
    for c in range(nsub):
        cmax0 = scores_sub(0, c, 0, cmax0)
    m0 = jnp.full((1, tq), -jnp.inf, F32)
    den0 = jnp.zeros((SUBLANES, tq), F32)
    acc0 = jnp.zeros((V_HEAD, tq), F32)
    _, _, den, acc = lax.fori_loop(0, nk // 2, pair, (cmax0, m0, den0, acc0))
    _store_heads(o_ref, acc, den)


def _attn_unshifted_kernel(q_ref, k_ref, vt_ref, o_ref, s_ref, *, tk):
    tq = q_ref.shape[1]
    nk = k_ref.shape[0] // tk
    nsub = tk // KSUB
    last = k_ref.shape[0] // KSUB - 1
    qt = q_ref[...]

    nq = min(QBLOCK, tq)
    splits = tq // nq

    def scores_sub(g, slot, h):
        start = g * KSUB if isinstance(g, int) else pl.multiple_of(g * KSUB, KSUB)
        s_ref[slot, :, h * nq:(h + 1) * nq] = _dot(k_ref[pl.ds(start, KSUB), :], qt[:, h * nq:(h + 1) * nq])

    per = min(CHUNKS_PER_TRIP, nk)
    trips = nk // per

    def trip(jj, carry):
        dens, accs = list(carry[0]), list(carry[1])
        for u in range(per * nsub):
            j, c = jj * per + u // nsub, u % nsub
            ahead = j * nsub + c + AHEAD
            for h in range(splits):
                if trips > 1:
                    scores_sub(jnp.minimum(ahead, last), (u + AHEAD) % RING, h)
                elif ahead <= last:
                    scores_sub(ahead, (u + AHEAD) % RING, h)
                p = jnp.exp2(s_ref[u % RING, :, h * nq:(h + 1) * nq])
                dens[h] = dens[h] + _fold_rows(p)
                accs[h] = accs[h] + _dot(vt_ref[j, :, c * KSUB:(c + 1) * KSUB], p.astype(BF16))
        return tuple(dens), tuple(accs)

    for g in range(AHEAD):
        for h in range(splits):
            scores_sub(g, g % RING, h)
    init = (tuple(jnp.zeros((SUBLANES, nq), F32) for _ in range(splits)),
            tuple(jnp.zeros((V_HEAD, nq), F32) for _ in range(splits)))
    dens, accs = trip(0, init) if trips == 1 else lax.fori_loop(0, trips, trip, init)
    _store_heads(o_ref, jnp.concatenate(accs, axis=1), jnp.concatenate(dens, axis=1))


def _attn(q, k, vt, *, tq, tk, shifted):
    seq = q.shape[1]
    nk = seq // tk
    assert seq % tq == 0 and seq % tk == 0 and nk % 2 == 0 and nk % min(CHUNKS_PER_TRIP, nk) == 0
    assert tk % (KSUB * RING) == 0 and tq % min(QBLOCK, tq) == 0 and tq % LANES == 0
    qspec = pl.BlockSpec((HEAD_PAD, tq), lambda h, i: (h, i))
    kspec = pl.BlockSpec((seq, HEAD_PAD), lambda h, i: (0, h))
    vspec = pl.BlockSpec((None, seq // tk, V_HEAD, tk), lambda h, i: (h, 0, 0, 0))
    body = _attn_kernel if shifted else _attn_unshifted_kernel
    return pl.pallas_call(
        functools.partial(body, tk=tk),
        grid=(MLA_HEADS, seq // tq),
        in_specs=[qspec, kspec, vspec],
        out_specs=pl.BlockSpec((V_HEAD, tq), lambda h, i: (h, i)),
        out_shape=jax.ShapeDtypeStruct((MLA_WIDTH, seq), BF16),
        scratch_shapes=[pltpu.VMEM((2, tk, tq) if shifted else (RING, KSUB, tq), F32)],
        compiler_params=_params("parallel", "parallel"),
        name="attn" if shifted else "attn_unshifted",
    )(q, k, vt)


def _score_bound(q_gain, k_gain):
    return (LOG2_E * QK_HEAD ** 0.5) * jnp.max(jnp.abs(q_gain)) * jnp.max(jnp.abs(k_gain)) * BF16_NORM_SLACK


def _post_kernel(x_ref, o_ref, zc_ref, zprev_ref, znext_ref, wpool_ref, ps_ref, woa_ref, wob_ref,
                 out_ref, ext_ref, p2_ref, p4_ref, *, tm, seq):
    i = pl.program_id(0)
    last = pl.num_programs(0) - 1
    zc = zc_ref[...]
    gd, h = POOL_GROUP_DIM, POOL_HALO
    ext_ref[0:h, :] = jnp.where(i > 0, zprev_ref[...], 0.0)
    ext_ref[h:h + tm, :] = zc
    ext_ref[h + tm:2 * h + tm, :] = jnp.where(i < last, znext_ref[...], 0.0)
    ext_ref[2 * h + tm:, :] = jnp.zeros((2 * h, POOL_WIDTH), F32)
    n2, n4, n8 = tm + 3 * h, tm + 2 * h, tm + h
    p2_ref[...] = ext_ref[0:n2, gd:] + ext_ref[1:n2 + 1, gd:]
    p4_ref[...] = p2_ref[0:n4, gd:] + p2_ref[2:n4 + 2, gd:]
    p8 = p4_ref[0:n8, gd:] + p4_ref[4:n8 + 4, gd:]
    wsums = (ext_ref[h - 1:h - 1 + tm, 0:gd] + ext_ref[h:h + tm, 0:gd],
             p2_ref[h - 2:h - 2 + tm, 0:gd] + p2_ref[h:h + tm, 0:gd],
             p4_ref[h - 4:h - 4 + tm, 0:gd] + p4_ref[h:h + tm, 0:gd],
             p8[0:tm] + p8[h:h + tm])
    pos = i * tm + lax.broadcasted_iota(jnp.int32, (tm, 1), 0)
    parts = []
    for g, w in enumerate(POOL_WINDOWS):
        left = w // 2
        right = w - 1 - left
        c0 = g * gd
        cnt = (jnp.minimum(pos + right + 1, seq) - jnp.maximum(pos - left, 0)).astype(F32)
        mixed = (wsums[g] / cnt - zc[:, c0:c0 + gd]).astype(BF16)
        y = _dot(mixed, wpool_ref[g]) * ps_ref[:, c0:c0 + gd]
        parts.append(y.astype(BF16))
    b = jnp.concatenate(parts, axis=1)
    attn_part = lax.dot_general(o_ref[...], woa_ref[...], (((0,), (0,)), ((), ())), preferred_element_type=F32)
    out_ref[...] = x_ref[...] + (attn_part + _dot(b, wob_ref[...]))


def _post(x, o, zp, w_pool, pool_scale, w_out_a, w_out_b, layer, *, tm):
    seq = x.shape[0]
    assert seq % tm == 0 and tm % POOL_HALO == 0 and POOL_WINDOWS == (2, 4, 8, 16)
    per = tm // POOL_HALO
    nblk = seq // POOL_HALO
    row = pl.BlockSpec((tm, D_MODEL), lambda i: (i, 0))
    return pl.pallas_call(
        functools.partial(_post_kernel, tm=tm, seq=seq),
        grid=(seq // tm,),
        in_specs=[row,
                  pl.BlockSpec((MLA_WIDTH, tm), lambda i: (0, i)),
                  pl.BlockSpec((tm, POOL_WIDTH), lambda i: (i, 0)),
                  pl.BlockSpec((POOL_HALO, POOL_WIDTH), lambda i: (jnp.maximum(i * per - 1, 0), 0)),
                  pl.BlockSpec((POOL_HALO, POOL_WIDTH), lambda i: (jnp.minimum((i + 1) * per, nblk - 1), 0)),
                  ] + [_layer_spec(a, layer) for a in (w_pool, pool_scale, w_out_a, w_out_b)],
        out_specs=row,
        out_shape=jax.ShapeDtypeStruct(x.shape, F32),
        scratch_shapes=[pltpu.VMEM((tm + 4 * POOL_HALO, POOL_WIDTH), F32),
                        pltpu.VMEM((tm + 3 * POOL_HALO, POOL_WIDTH - POOL_GROUP_DIM), F32),
                        pltpu.VMEM((tm + 2 * POOL_HALO, POOL_WIDTH - 2 * POOL_GROUP_DIM), F32)],
        compiler_params=_params("parallel"),
        name="post",
    )(x, o, zp, zp, zp, w_pool, pool_scale, w_out_a, w_out_b)


def _head_rows(nope, rope):
    pad = jnp.zeros(nope.shape[:-2] + (HEAD_PAD - QK_HEAD, nope.shape[-1]), nope.dtype)
    return jnp.concatenate([nope, rope, pad], axis=-2)


def _layout_weights(w_in, w_uq, w_uk, w_uv, q_norm, k_norm, w_out):
    depth = w_in.shape[0]
    w_in, w_uq, w_uk, w_uv, w_out = (w.astype(BF16) for w in (w_in, w_uq, w_uk, w_uv, w_out))
    o_pe = Q_LORA + KV_LORA
    o_pool = o_pe + QK_ROPE
    w_in_r = jnp.concatenate([w_in[..., :o_pe], w_in[..., o_pool:]], axis=-1)
    w_pe_t = jnp.swapaxes(w_in[..., o_pe:o_pool], 1, 2)
    uq = jnp.transpose(w_uq.reshape(depth, Q_LORA, MLA_HEADS, QK_HEAD), (0, 2, 3, 1))
    w_uq_t = uq.reshape(depth, MLA_HEADS * QK_HEAD, Q_LORA)
    uk = jnp.transpose(w_uk.reshape(depth, KV_LORA, MLA_HEADS, QK_NOPE), (0, 2, 3, 1))
    w_uk_t = uk.reshape(depth, MLA_HEADS * QK_NOPE, KV_LORA)
    uvt = jnp.transpose(w_uv.reshape(depth, KV_LORA, MLA_HEADS, V_HEAD), (0, 2, 3, 1))
    w_uvt_r = uvt.reshape(depth, MLA_HEADS * V_HEAD, KV_LORA)

    def gain_cols(g):
        nope, rope = g[:, :QK_NOPE], g[:, QK_NOPE:]
        swapped = jnp.concatenate([rope[:, HALF_ROPE:], rope[:, :HALF_ROPE]], axis=-1)
        pad = jnp.zeros((depth, HEAD_PAD - QK_HEAD), g.dtype)
        return (jnp.concatenate([nope, rope, pad], axis=-1),
                jnp.concatenate([jnp.zeros_like(nope), swapped, pad], axis=-1))

    cols = gain_cols(q_norm) + gain_cols(k_norm)
    gains = jnp.stack(cols + (jnp.zeros_like(cols[0]),) * 4, axis=-1)
    return w_in_r, w_pe_t, w_uq_t, w_uk_t, w_uvt_r, gains, w_out[:, :MLA_WIDTH], w_out[:, MLA_WIDTH:]


def _rope_tables(seq):
    inv = ROPE_THETA ** (-jnp.arange(0, QK_ROPE, 2, dtype=F32) / QK_ROPE)
    ang = jnp.concatenate([inv, inv])[:, None] * jnp.arange(seq, dtype=F32)[None, :]
    sign = jnp.concatenate([-jnp.ones_like(inv), jnp.ones_like(inv)])[:, None]
    ones, zeros = jnp.ones((QK_NOPE, seq), F32), jnp.zeros((QK_NOPE, seq), F32)
    return _head_rows(ones, jnp.cos(ang)), _head_rows(zeros, jnp.sin(ang) * sign)


def _tiles(seq):
    tk = min(2048, seq // 2)
    return dict(tm=min(1024, tk), tq=min(1024, seq), tq_fast=min(2048, seq), tk=tk, tf=4 * MXU_TILE)


def kernel(x, ffn1_norm, ffn1_w_gu, ffn1_w_down, mix_norm, w_in, q_lat_norm, kv_lat_norm, w_uq, w_uk, w_uv,
           q_norm, k_norm, w_pool, pool_scale, w_out, ffn2_norm, ffn2_w_gu, ffn2_w_down):
    batch, seq, _ = x.shape
    depth = w_in.shape[0]
    t = _tiles(seq)
    cos, sin = _rope_tables(seq)
    w_in_r, w_pe_t, w_uq_t, w_uk_t, w_uvt_r, gains, w_out_a, w_out_b = _layout_weights(
        w_in, w_uq, w_uk, w_uv, q_norm, k_norm, w_out)
    gu, down = ffn1_w_gu[:1].astype(BF16), ffn1_w_down[:1].astype(BF16)
    w_pool_b = w_pool.astype(BF16)
    ffn1_g, ffn2_g, mix_g = ffn1_norm[:, None], ffn2_norm[:, None], mix_norm[:, None]
    q_lat_g, kv_lat_g, pool_g = q_lat_norm[:, None], kv_lat_norm[:, None], pool_scale[:, None]

    outs = []
    for b in range(batch):
        xb = x[b]
        for l in range(depth):
            xb, gu, down = _ffn(xb, ffn1_g, l, gu, down, (ffn2_w_gu, ffn2_w_down, l), tm=t["tm"], tf=t["tf"])
            q, k, vt, zp = _pre(xb, mix_g, w_in_r, w_pe_t, q_lat_g, kv_lat_g, w_uq_t, w_uk_t, w_uvt_r, gains,
                                cos, sin, l, tm=t["tm"], tk=t["tk"])
            o = lax.cond(_score_bound(q_norm[l], k_norm[l]) <= MAX_UNSHIFTED_SCORE,
                         functools.partial(_attn, tq=t["tq_fast"], tk=t["tk"], shifted=False),
                         functools.partial(_attn, tq=t["tq"], tk=t["tk"], shifted=True),
                         q, k, vt)
            xb = _post(xb, o, zp, w_pool_b, pool_g, w_out_a, w_out_b, l, tm=t["tm"])
            if l + 1 < depth or b + 1 < batch:
                nxt = (ffn1_w_gu, ffn1_w_down, (l + 1) % depth)
                xb, gu, down = _ffn(xb, ffn2_g, l, gu, down, nxt, tm=t["tm"], tf=t["tf"])
            else:
                xb = _ffn(xb, ffn2_g, l, gu, down, None, tm=t["tm"], tf=t["tf"])
        outs.append(xb)
    return outs[0][None] if batch == 1 else jnp.stack(outs, axis=0)
```
